```python
import jax, jax.numpy as jnp
from jax import lax
import numpy as np

D_MODEL = 2048
BATCH = 2
SEQ = 4096
DEPTH = 4
DEC_BATCH = 8
DEC_SEQ = 8
PAST_LEN = 16384
PAGE_SIZE = 128

N_A_LAYERS = DEPTH // 2
N_B_LAYERS = DEPTH - N_A_LAYERS
A_EXPAND = 128
A_HEADS = D_MODEL // A_EXPAND
A_DK = A_EXPAND
A_DV = D_MODEL // A_HEADS
A_CHUNK = 64
B_HEADS = 16
B_HEAD_DIM = D_MODEL // B_HEADS
Q_BLOCK = 128
D_FF = 4 * D_MODEL
EPS = 1e-6
MASK_VALUE = -1e30
TINY = 1e-30

kernel_name = 'yoco_hgrn2_fox_decoder_step'


def rms_norm(x, g):
    xf = x.astype(jnp.float32)
    y = xf * lax.rsqrt(jnp.mean(xf * xf, axis=-1, keepdims=True) + EPS)
    return (y * g.astype(jnp.float32)).astype(x.dtype)


def ada_mod(c, w, b):
    m = (jax.nn.silu(c) @ w + b)[:, None, :]
    shift, scale, gate = jnp.split(m, 3, axis=-1)
    return shift, scale, gate


def hgrn2_lower_bounds(lb_logits):
    p = jax.nn.softmax(lb_logits.astype(jnp.float32), axis=0)
    return jnp.cumsum(p, axis=0) - p[0:1]


def hgrn2_recurrence(q, k, v, logf, s0):
    b_, L, H, DK = q.shape
    DV = v.shape[-1]
    C = min(A_CHUNK, L)
    pad = (-L) % C
    n = (L + pad) // C

    def blocks(t):
        t = jnp.pad(t.astype(jnp.float32), ((0, 0), (0, pad), (0, 0), (0, 0)))
        return t.reshape(b_, n, C, H, t.shape[-1]).transpose(1, 0, 3, 2, 4)

    causal = jnp.tril(jnp.ones((C, C), dtype=bool))
    scale = DK ** -0.5

    def step(S, inp):
        qc, kc, vc, gc = inp
        bcum = jnp.cumsum(gc, axis=2)
        inter = jnp.einsum('bhtk,bhkv->bhtv', qc * jnp.exp(bcum), S)
        diff = bcum[:, :, :, None, :] - bcum[:, :, None, :, :]
        decay = jnp.exp(jnp.where(causal[:, :, None], diff, MASK_VALUE))
        scores = jnp.einsum('bhtk,bhtsk,bhsk->bhts', qc, decay, kc)
        intra = jnp.einsum('bhts,bhsv->bhtv', scores, vc)
        blast = bcum[:, :, -1, :]
        S_new = jnp.exp(blast)[..., None] * S + jnp.einsum(
            'bhsk,bhsv->bhkv', kc * jnp.exp(blast[:, :, None, :] - bcum), vc)
        return S_new, (inter + intra) * scale

    S_fin, o = lax.scan(step, s0.astype(jnp.float32), (blocks(q), blocks(k), blocks(v), blocks(logf)))
    o = o.transpose(1, 0, 3, 2, 4).reshape(b_, n * C, H, DV)[:, :L]
    return o, S_fin


def hgrn2_mixer(h, w_in, w_out, lb, out_g, s0):
    B_, L, _ = h.shape
    q, fz, i, g = jnp.split(h @ w_in, 4, axis=-1)
    heads = lambda t: t.reshape(B_, L, A_HEADS, -1)
    fz = fz.astype(jnp.float32)
    logf = jnp.logaddexp(jnp.log(jnp.maximum(lb, TINY)), jnp.log1p(-lb) + jax.nn.log_sigmoid(fz))
    k = (1.0 - lb) * jax.nn.sigmoid(-fz)
    o, s_new = hgrn2_recurrence(heads(jax.nn.silu(q)), heads(k), heads(i), heads(logf), s0)
    o = rms_norm(o, out_g.reshape(A_HEADS, A_DV)) * jax.nn.silu(heads(g).astype(jnp.float32))
    return o.reshape(B_, L, D_MODEL).astype(h.dtype) @ w_out, s_new


def shared_kv(x, kv_norm, w_kv, b_forget):
    B_, L, _ = x.shape
    z = rms_norm(x, kv_norm) @ w_kv
    k = z[..., :D_MODEL].reshape(B_, L, B_HEADS, B_HEAD_DIM)
    v = z[..., D_MODEL:2 * D_MODEL].reshape(B_, L, B_HEADS, B_HEAD_DIM)
    logf = jax.nn.log_sigmoid(z[..., 2 * D_MODEL:].astype(jnp.float32) + b_forget.astype(jnp.float32))
    return k, v, logf


def fox_logits(q, k, fq, fk):
    s = jnp.einsum('bqhd,bkhd->bhqk', q, k).astype(jnp.float32) * (B_HEAD_DIM ** -0.5)
    return s + jnp.swapaxes(fq, 1, 2)[..., :, None] - jnp.swapaxes(fk, 1, 2)[..., None, :]


def fox_attend_prompt(h, w_q, w_o, shared):
    k, v, F = shared
    B_, S, _ = h.shape
    q = (h @ w_q).reshape(B_, S, B_HEADS, B_HEAD_DIM)
    blk = min(Q_BLOCK, S)
    nb = S // blk
    qb = q.reshape(B_, nb, blk, B_HEADS, B_HEAD_DIM).transpose(1, 0, 2, 3, 4)
    fb = F.reshape(B_, nb, blk, B_HEADS).transpose(1, 0, 2, 3)
    pb = jnp.arange(S).reshape(nb, blk)
    kpos = jnp.arange(S)

    def one(args):
        qi, fi, pi = args
        logits = jnp.where(pi[:, None] >= kpos[None, :], fox_logits(qi, k, fi, F), MASK_VALUE)
        p = jax.nn.softmax(logits, axis=-1)
        return jnp.einsum('bhqk,bkhd->bqhd', p.astype(v.dtype), v)

    o = lax.map(one, (qb, fb, pb))
    o = o.transpose(1, 0, 2, 3, 4).reshape(B_, S, D_MODEL)
    return o @ w_o


def fox_attend_sample(h, w_q, w_o, shared):
    past_k, past_v, f_past, k, v, f_new = shared
    B_, L, _ = h.shape
    P = past_k.shape[1]
    q = (h @ w_q).reshape(B_, L, B_HEADS, B_HEAD_DIM)
    lp = fox_logits(q, past_k, f_new, f_past)
    ln = jnp.where(jnp.tril(jnp.ones((L, L), dtype=bool)), fox_logits(q, k, f_new, f_new), MASK_VALUE)
    p = jax.nn.softmax(jnp.concatenate([lp, ln], axis=-1), axis=-1).astype(v.dtype)
    o = (jnp.einsum('bhqk,bkhd->bqhd', p[..., :P], past_v)
         + jnp.einsum('bhqk,bkhd->bqhd', p[..., P:], v))
    return o.reshape(B_, L, D_MODEL) @ w_o


def trunk(x, c, s0, make_shared, attend, norm_pre, norm_post, w_ada, b_ada,
          a_w_in, a_w_out, a_lb_logits, a_out_norm, b_w_q, b_w_o, mlp_w_in, mlp_w_out):
    lb_all = hgrn2_lower_bounds(a_lb_logits)
    states = []
    shared = None
    rows = None
    for l in range(DEPTH):
        shift, scale, gate = ada_mod(c, w_ada[l, 0], b_ada[l, 0])
        h = rms_norm(x, norm_pre[l, 0]) * (1 + scale) + shift
        if l < N_A_LAYERS:
            y, s_new = hgrn2_mixer(h, a_w_in[l], a_w_out[l], lb_all[l], a_out_norm[l], s0[l])
            states.append(s_new)
        else:
            if shared is None:
                shared, rows = make_shared(x)
            j = l - N_A_LAYERS
            y = attend(h, b_w_q[j], b_w_o[j], shared)
        x = x + gate * rms_norm(y, norm_post[l, 0])
        shift, scale, gate = ada_mod(c, w_ada[l, 1], b_ada[l, 1])
        h = rms_norm(x, norm_pre[l, 1]) * (1 + scale) + shift
        y = jnp.square(jax.nn.relu(h @ mlp_w_in[l])) @ mlp_w_out[l]
        x = x + gate * rms_norm(y, norm_post[l, 1])
    return x, jnp.stack(states), rows


def setup_inputs(seed: int = 0) -> dict:
    key = jax.random.key(seed)
    ks = jax.random.split(key, 24)
    n_pages = PAST_LEN // PAGE_SIZE
    n_used = DEC_BATCH * n_pages
    n_phys = n_used + max(1, n_used // 4)
    f32 = jnp.float32
    nrm = lambda k, shape, s=1.0: jax.random.normal(k, shape, f32) * s
    page_table = jax.random.permutation(ks[9], n_phys)[:n_used].reshape(DEC_BATCH, n_pages).astype(jnp.int32)
    return {
        'x_prompt': nrm(ks[0], (BATCH, SEQ, D_MODEL)),
        'x_sample': nrm(ks[1], (DEC_BATCH, DEC_SEQ, D_MODEL)),
        'c_prompt': nrm(ks[2], (BATCH, D_MODEL)),
        'c_sample': nrm(ks[3], (DEC_BATCH, D_MODEL)),
        'state_hgrn': nrm(ks[4], (N_A_LAYERS, DEC_BATCH, A_HEADS, A_DK, A_DV), 0.5),
        'cache_k': nrm(ks[5], (n_phys, PAGE_SIZE, B_HEADS, B_HEAD_DIM)),
        'cache_v': nrm(ks[6], (n_phys, PAGE_SIZE, B_HEADS, B_HEAD_DIM)),
        'cache_logf': jax.nn.log_sigmoid(nrm(ks[7], (n_phys, PAGE_SIZE, B_HEADS)) + 4.0),
        'page_table': page_table,
        'norm_pre': 1.0 + nrm(ks[8], (DEPTH, 2, D_MODEL), 0.05),
        'norm_post': 1.0 + nrm(ks[10], (DEPTH, 2, D_MODEL), 0.05),
        'w_ada': nrm(ks[11], (DEPTH, 2, D_MODEL, 3 * D_MODEL), 0.5 * D_MODEL ** -0.5),
        'b_ada': nrm(ks[12], (DEPTH, 2, 3 * D_MODEL), 0.02),
        'a_w_in': nrm(ks[13], (N_A_LAYERS, D_MODEL, 4 * D_MODEL), D_MODEL ** -0.5),
        'a_w_out': nrm(ks[14], (N_A_LAYERS, D_MODEL, D_MODEL), D_MODEL ** -0.5),
        'a_lb_logits': nrm(ks[15], (N_A_LAYERS, D_MODEL), 0.5),
        'a_out_norm': 1.0 + nrm(ks[16], (N_A_LAYERS, D_MODEL), 0.05),
        'kv_norm': 1.0 + nrm(ks[17], (D_MODEL,), 0.05),
        'w_kv': nrm(ks[18], (D_MODEL, 2 * D_MODEL + B_HEADS), D_MODEL ** -0.5),
        'b_forget': jnp.linspace(1.0, 6.0, B_HEADS, dtype=f32) + nrm(ks[19], (B_HEADS,), 0.1),
        'b_w_q': nrm(ks[20], (N_B_LAYERS, D_MODEL, D_MODEL), D_MODEL ** -0.5),
        'b_w_o': nrm(ks[21], (N_B_LAYERS, D_MODEL, D_MODEL), D_MODEL ** -0.5),
        'mlp_w_in': nrm(ks[22], (DEPTH, D_MODEL, D_FF), D_MODEL ** -0.5),
        'mlp_w_out': nrm(ks[23], (DEPTH, D_FF, D_MODEL), D_FF ** -0.5),
    }


def reference(x_prompt, x_sample, c_prompt, c_sample, state_hgrn, cache_k, cache_v, cache_logf, page_table,
              norm_pre, norm_post, w_ada, b_ada, a_w_in, a_w_out, a_lb_logits, a_out_norm,
              kv_norm, w_kv, b_forget, b_w_q, b_w_o, mlp_w_in, mlp_w_out):
    weights = (norm_pre, norm_post, w_ada, b_ada, a_w_in, a_w_out, a_lb_logits, a_out_norm,
               b_w_q, b_w_o, mlp_w_in, mlp_w_out)

    def make_shared_prompt(xs):
        k, v, logf = shared_kv(xs, kv_norm, w_kv, b_forget)
        return (k, v, jnp.cumsum(logf, axis=1)), (k, v, logf)

    s0_prompt = jnp.zeros((N_A_LAYERS, x_prompt.shape[0], A_HEADS, A_DK, A_DV), jnp.float32)
    y_prompt, state_hgrn_prompt, rows_p = trunk(
        x_prompt, c_prompt, s0_prompt, make_shared_prompt, fox_attend_prompt, *weights)
    k_prompt, v_prompt, logf_prompt = rows_p

    db = x_sample.shape[0]
    past_len = page_table.shape[1] * cache_k.shape[1]
    past_k = cache_k[page_table].reshape(db, past_len, B_HEADS, B_HEAD_DIM)
    past_v = cache_v[page_table].reshape(db, past_len, B_HEADS, B_HEAD_DIM)
    past_logf = cache_logf[page_table].reshape(db, past_len, B_HEADS).astype(jnp.float32)

    def make_shared_sample(xs):
        k, v, logf = shared_kv(xs, kv_norm, w_kv, b_forget)
        F = jnp.cumsum(jnp.concatenate([past_logf, logf], axis=1), axis=1)
        return (past_k, past_v, F[:, :past_len], k, v, F[:, past_len:]), (k, v, logf)

    y_sample, state_hgrn_sample, rows_s = trunk(
        x_sample, c_sample, state_hgrn, make_shared_sample, fox_attend_sample, *weights)
    k_sample, v_sample, logf_sample = rows_s

    return (y_prompt, y_sample, state_hgrn_prompt, state_hgrn_sample,
            k_prompt, v_prompt, logf_prompt, k_sample, v_sample, logf_sample)
```

```python
import functools

import jax
import jax.numpy as jnp
from jax import lax
from jax.experimental import pallas as pl
from jax.experimental.pallas import tpu as pltpu

EPS = 1e-6
MASK_VALUE = -1e30
TINY = 1e-30
F32 = jnp.float32
BF16 = jnp.bfloat16
HIGHEST = lax.Precision.HIGHEST

LANES = 128
HGRN_CHUNK = 128
HGRN_SUB = 16
MIB = 2 ** 20


def _params(n_axes, vmem_mib):
    return pltpu.CompilerParams(
        dimension_semantics=("arbitrary",) * n_axes, vmem_limit_bytes=vmem_mib * MIB)


def _tile(n, pref):
    t = min(n, pref)
    while n % t:
        t -= 1
    return t


def _sigmoid(x):
    return 1.0 / (1.0 + jnp.exp(-x))


def _log_sigmoid(x):
    return jnp.minimum(x, 0.0) - jnp.log1p(jnp.exp(-jnp.abs(x)))


def _rms(x, g):
    return x * lax.rsqrt(jnp.mean(x * x, axis=-1, keepdims=True) + EPS) * g


def _dot(a, b, precision=None):
    return jnp.dot(a, b, preferred_element_type=F32, precision=precision)


def _dot_nt(a, b, precision=None):
    return lax.dot_general(a, b, (((1,), (1,)), ((), ())), preferred_element_type=F32, precision=precision)


def _ada_kernel(c_ref, w_ref, b_ref, o_ref):
    c = c_ref[...]
    s = (c * _sigmoid(c)).astype(BF16)
    o_ref[0] = _dot(s, w_ref[0].astype(BF16)) + b_ref[0]


def _ada_all(c_all, w_ada, b_ada):
    n, d, n3 = w_ada.shape
    r = c_all.shape[0]
    tn = _tile(n3, 1024)
    return pl.pallas_call(
        _ada_kernel,
        grid=(n, n3 // tn),
        in_specs=[
            pl.BlockSpec((r, d), lambda i, j: (0, 0)),
            pl.BlockSpec((1, d, tn), lambda i, j: (i, 0, j)),
            pl.BlockSpec((1, 1, tn), lambda i, j: (i, 0, j)),
        ],
        out_specs=pl.BlockSpec((1, r, tn), lambda i, j: (i, 0, j)),
        out_shape=jax.ShapeDtypeStruct((n, r, n3), F32),
        compiler_params=_params(2, 40),
    )(c_all, w_ada, b_ada)


def _norm_mm_kernel(*refs, has_mod):
    if has_mod:
        x_ref, g_ref, sh_ref, sc_ref, w_ref, o_ref, h_ref = refs
    else:
        x_ref, g_ref, w_ref, o_ref, h_ref = refs

    @pl.when(pl.program_id(1) == 0)
    def _():
        h = _rms(x_ref[...], g_ref[...])
        if has_mod:
            h = h * (1.0 + sc_ref[0]) + sh_ref[0]
        h_ref[...] = h.astype(BF16)

    o_ref[...] = _dot(h_ref[...], w_ref[...].astype(BF16)).astype(o_ref.dtype)


def _norm_mm(x, g, mod, w, *, rows_per_mod, n_out=None, col_off=0, tm_pref=1024, tn_pref=512):
    m, d = x.shape
    n_out = w.shape[1] if n_out is None else n_out
    tm = _tile(m, tm_pref)
    tn = _tile(n_out, tn_pref)
    assert col_off % tn == 0
    joff = col_off // tn
    in_specs = [pl.BlockSpec((tm, d), lambda i, j: (i, 0)), pl.BlockSpec((1, d), lambda i, j: (0, 0))]
    args = [x, g]
    if mod is not None:
        r = mod.shape[1]
        assert rows_per_mod % tm == 0 and r in (1, tm)
        bidx = lambda i: (i * tm) // rows_per_mod
        in_specs += [pl.BlockSpec((1, r, d), lambda i, j: (bidx(i), 0, 0)),
                     pl.BlockSpec((1, r, d), lambda i, j: (bidx(i), 0, 1))]
        args += [mod, mod]
    in_specs.append(pl.BlockSpec((d, tn), lambda i, j: (0, j + joff)))
    args.append(w)
    return pl.pallas_call(
        functools.partial(_norm_mm_kernel, has_mod=mod is not None),
        grid=(m // tm, n_out // tn),
        in_specs=in_specs,
        out_specs=pl.BlockSpec((tm, tn), lambda i, j: (i, j)),
        out_shape=jax.ShapeDtypeStruct((m, n_out), F32),
        scratch_shapes=[pltpu.VMEM((tm, d), BF16)],
        compiler_params=_params(2, 48),
    )(*args)


def _mm_post_kernel(a_ref, w_ref, x_ref, gate_ref, g_ref, o_ref, acc_ref):
    k = pl.program_id(1)

    @pl.when(k == 0)
    def _():
        acc_ref[...] = jnp.zeros_like(acc_ref)

    acc_ref[...] += _dot(a_ref[...].astype(BF16), w_ref[...].astype(BF16))

    @pl.when(k == pl.num_programs(1) - 1)
    def _():
        o_ref[...] = x_ref[...] + gate_ref[0] * _rms(acc_ref[...], g_ref[...])


def _mm_post(a, w, x, mod, g, *, rows_per_mod, tm_pref=512, tk_pref=512):
    m, kdim = a.shape
    d = w.shape[1]
    tm = _tile(m, tm_pref)
    tk = _tile(kdim, tk_pref)
    r = mod.shape[1]
    assert rows_per_mod % tm == 0 and r in (1, tm)
    bidx = lambda i: (i * tm) // rows_per_mod
    return pl.pallas_call(
        _mm_post_kernel,
        grid=(m // tm, kdim // tk),
        in_specs=[
            pl.BlockSpec((tm, tk), lambda i, k: (i, k)),
            pl.BlockSpec((tk, d), lambda i, k: (k, 0)),
            pl.BlockSpec((tm, d), lambda i, k: (i, 0)),
            pl.BlockSpec((1, r, d), lambda i, k: (bidx(i), 0, 2)),
            pl.BlockSpec((1, d), lambda i, k: (0, 0)),
        ],
        out_specs=pl.BlockSpec((tm, d), lambda i, k: (i, 0)),
        out_shape=jax.ShapeDtypeStruct((m, d), F32),
        scratch_shapes=[pltpu.VMEM((tm, d), F32)],
        compiler_params=_params(2, 48),
    )(a, w, x, mod, g)


def _mlp_kernel(x_ref, gpre_ref, sh_ref, sc_ref, gate_ref, w1_ref, w2_ref, gpost_ref, o_ref, h_ref, acc_ref):
    f = pl.program_id(1)

    @pl.when(f == 0)
    def _():
        h = _rms(x_ref[...], gpre_ref[...]) * (1.0 + sc_ref[0]) + sh_ref[0]
        h_ref[...] = h.astype(BF16)
        acc_ref[...] = jnp.zeros_like(acc_ref)

    u = jnp.maximum(_dot(h_ref[...], w1_ref[...].astype(BF16)), 0.0)
    acc_ref[...] += _dot((u * u).astype(BF16), w2_ref[...].astype(BF16))

    @pl.when(f == pl.num_programs(1) - 1)
    def _():
        o_ref[...] = x_ref[...] + gate_ref[0] * _rms(acc_ref[...], gpost_ref[...])


def _mlp(x, gpre, mod, w1, w2, gpost, *, rows_per_mod, tm_pref=512, tf_pref=512):
    m, d = x.shape
    ff = w1.shape[1]
    tm = _tile(m, tm_pref)
    tf = _tile(ff, tf_pref)
    r = mod.shape[1]
    assert rows_per_mod % tm == 0 and r in (1, tm)
    bidx = lambda i: (i * tm) // rows_per_mod
    return pl.pallas_call(
        _mlp_kernel,
        grid=(m // tm, ff // tf),
        in_specs=[
            pl.BlockSpec((tm, d), lambda i, f: (i, 0)),
            pl.BlockSpec((1, d), lambda i, f: (0, 0)),
            pl.BlockSpec((1, r, d), lambda i, f: (bidx(i), 0, 0)),
            pl.BlockSpec((1, r, d), lambda i, f: (bidx(i), 0, 1)),
            pl.BlockSpec((1, r, d), lambda i, f: (bidx(i), 0, 2)),
            pl.BlockSpec((d, tf), lambda i, f: (0, f)),
            pl.BlockSpec((tf, d), lambda i, f: (f, 0)),
            pl.BlockSpec((1, d), lambda i, f: (0, 0)),
        ],
        out_specs=pl.BlockSpec((tm, d), lambda i, f: (i, 0)),
        out_shape=jax.ShapeDtypeStruct((m, d), F32),
        scratch_shapes=[pltpu.VMEM((tm, d), BF16), pltpu.VMEM((tm, d), F32)],
        compiler_params=_params(2, 56),
    )(x, gpre, mod, mod, mod, w1, w2, gpost)


def _hgrn_kernel(q_ref, f_ref, i_ref, g_ref, lbl_ref, og_ref, s0_ref, o_ref, s_ref, st_ref,
                 *, layer, rows, n_inner):
    c = HGRN_CHUNK
    dk = q_ref.shape[-1]
    step = pl.program_id(2)

    @pl.when(step == 0)
    def _():
        st_ref[...] = s0_ref[0, 0].T

    z = lbl_ref[...]
    e = jnp.exp(z - jnp.max(z, axis=0, keepdims=True))
    p = e / jnp.sum(e, axis=0, keepdims=True)
    cs = p[0:1]
    for r in range(1, layer + 1):
        cs = cs + p[r:r + 1]
    lb = cs - p[0:1]
    log_lb = jnp.log(jnp.maximum(lb, TINY))
    log_1m_lb = jnp.log1p(-lb)

    tril = (lax.broadcasted_iota(jnp.int32, (c, c), 0) >= lax.broadcasted_iota(jnp.int32, (c, c), 1)).astype(F32)
    sub_row = lax.broadcasted_iota(jnp.int32, (HGRN_SUB, 1), 0)
    scale = dk ** -0.5
    og = og_ref[...]

    def pad(t):
        if rows == c:
            return t
        return jnp.concatenate([t, jnp.zeros((c - rows, t.shape[1]), t.dtype)], axis=0)

    def chunk(j, carry):
        if n_inner == 1:
            sl = pl.ds(0, rows)
        else:
            sl = pl.ds(pl.multiple_of(j * c, c), c)
        qz = q_ref[0, sl, :]
        fz = f_ref[0, sl, :]
        q = pad(qz * _sigmoid(qz))
        a = log_1m_lb + _log_sigmoid(fz)
        logf = pad(jnp.maximum(log_lb, a) + jnp.log1p(jnp.exp(-jnp.abs(log_lb - a))))
        kk = pad((1.0 - lb) * _sigmoid(-fz))
        v = pad(i_ref[0, sl, :])
        bcum = _dot(tril, logf, HIGHEST)
        st = st_ref[...]
        inter = _dot_nt((q * jnp.exp(bcum)).astype(BF16), st.astype(BF16))
        kb = kk.astype(BF16)
        vb = v.astype(BF16)
        outs = []
        for i in range(c // HGRN_SUB):
            lo = i * HGRN_SUB
            qi, bi, ki, vi = (t[lo:lo + HGRN_SUB] for t in (q, bcum, kk, v))
            acc = inter[lo:lo + HGRN_SUB]
            for s in range(HGRN_SUB):
                dlt = jnp.where(sub_row >= s, bi - bi[s:s + 1], MASK_VALUE)
                w = jnp.sum(qi * jnp.exp(dlt) * ki[s:s + 1], axis=-1, keepdims=True)
                acc = acc + w * vi[s:s + 1]
            if i > 0:
                ref_b = bi[0:1]
                qs = (qi * jnp.exp(bi - ref_b)).astype(BF16)
                kt = (kk[:lo] * jnp.exp(ref_b - bcum[:lo])).astype(BF16)
                sc = _dot_nt(qs, kt)
                acc = acc + _dot(sc.astype(BF16), vb[:lo])
            outs.append(acc)
        o = jnp.concatenate(outs, axis=0) * scale
        blast = bcum[c - 1:c]
        kd = (kk * jnp.exp(blast - bcum)).astype(BF16)
        st_ref[...] = st * jnp.exp(blast) + _dot(v.T.astype(BF16), kd)
        del kb
        gz = g_ref[0, sl, :]
        o = _rms(o[:rows], og) * (gz * _sigmoid(gz))
        o_ref[0, sl, :] = o
        return carry

    lax.fori_loop(0, n_inner, chunk, 0)

    @pl.when(step == pl.num_programs(2) - 1)
    def _():
        s_ref[0, 0] = st_ref[...].T


def _hgrn(proj, lb_logits, out_g, s0, *, layer, heads):
    b, l, _ = proj.shape
    dk = s0.shape[2]
    dv = s0.shape[3]
    assert dk == LANES and dv == LANES
    c = HGRN_CHUNK
    if l >= c:
        assert l % c == 0
        rows = c
        n_inner = _tile(l // c, 4)
        blk = rows * n_inner
    else:
        assert l % 8 == 0
        rows, n_inner, blk = l, 1, l
    nl = lb_logits.shape[0]
    col = lambda part: pl.BlockSpec((1, blk, dk), lambda bi, h, s: (bi, s, part * heads + h))
    o, s_new = pl.pallas_call(
        functools.partial(_hgrn_kernel, layer=layer, rows=rows, n_inner=n_inner),
        grid=(b, heads, l // blk),
        in_specs=[
            col(0), col(1), col(2), col(3),
            pl.BlockSpec((nl, dk), lambda bi, h, s: (0, h)),
            pl.BlockSpec((1, dv), lambda bi, h, s: (0, h)),
            pl.BlockSpec((1, 1, dk, dv), lambda bi, h, s: (bi, h, 0, 0)),
        ],
        out_specs=[
            pl.BlockSpec((1, blk, dv), lambda bi, h, s: (bi, s, h)),
            pl.BlockSpec((1, 1, dk, dv), lambda bi, h, s: (bi, h, 0, 0)),
        ],
        out_shape=[jax.ShapeDtypeStruct((b, l, heads * dv), F32),
                   jax.ShapeDtypeStruct((b, heads, dk, dv), F32)],
        scratch_shapes=[pltpu.VMEM((dv, dk), F32)],
        compiler_params=_params(3, 32),
    )(proj, proj, proj, proj, lb_logits, out_g, s0)
    return o, s_new


def _logf_kernel(x_ref, g_ref, wft_ref, bft_ref, c0_ref, lf_ref, ft_ref, car_ref):
    tm = x_ref.shape[0]

    @pl.when(pl.program_id(1) == 0)
    def _():
        car_ref[...] = c0_ref[0]

    h = _rms(x_ref[...], g_ref[...])
    zt = _dot_nt(wft_ref[...], h, HIGHEST) + bft_ref[...]
    lft = _log_sigmoid(zt)
    lf_ref[0] = lft
    triu = (lax.broadcasted_iota(jnp.int32, (tm, tm), 0) <= lax.broadcasted_iota(jnp.int32, (tm, tm), 1)).astype(F32)
    ft = _dot(lft, triu, HIGHEST) + car_ref[...]
    ft_ref[0] = ft
    car_ref[...] = ft[:, tm - 1:tm]


def _logf(x, g, wft, bft, c0t, *, batch):
    m, d = x.shape
    l = m // batch
    hh = wft.shape[0]
    tm = _tile(l, 512)
    nt = l // tm
    out = jax.ShapeDtypeStruct((batch, hh, l), F32)
    return pl.pallas_call(
        _logf_kernel,
        grid=(batch, nt),
        in_specs=[
            pl.BlockSpec((tm, d), lambda b, i: (b * nt + i, 0)),
            pl.BlockSpec((1, d), lambda b, i: (0, 0)),
            pl.BlockSpec((hh, d), lambda b, i: (0, 0)),
            pl.BlockSpec((hh, 1), lambda b, i: (0, 0)),
            pl.BlockSpec((1, hh, 1), lambda b, i: (b, 0, 0)),
        ],
        out_specs=[pl.BlockSpec((1, hh, tm), lambda b, i: (b, 0, i)),
                   pl.BlockSpec((1, hh, tm), lambda b, i: (b, 0, i))],
        out_shape=[out, out],
        scratch_shapes=[pltpu.VMEM((hh, 1), F32)],
        compiler_params=_params(2, 32),
    )(x, g, wft, bft, c0t)


def _fox_prompt_kernel(q_ref, k_ref, v_ref, fq_ref, fk_ref, o_ref, m_ref, l_ref, acc_ref, *, tq):
    i = pl.program_id(2)
    hd = q_ref.shape[-1]
    q = (q_ref[...] * (hd ** -0.5)).astype(BF16)
    fq = fq_ref[0, 0]
    m_ref[...] = jnp.full_like(m_ref, MASK_VALUE)
    l_ref[...] = jnp.zeros_like(l_ref)
    acc_ref[...] = jnp.zeros_like(acc_ref)
    rpos = i * tq + lax.broadcasted_iota(jnp.int32, (tq, tq), 0)
    cio = lax.broadcasted_iota(jnp.int32, (tq, tq), 1)

    def body(j, carry):
        ks = pl.ds(pl.multiple_of(j * tq, tq), tq)
        kt = k_ref[ks, :].astype(BF16)
        vt = v_ref[ks, :].astype(BF16)
        s = _dot_nt(q, kt) + fq - fk_ref[0, 0, :, ks]
        s = jnp.where(rpos >= j * tq + cio, s, MASK_VALUE)
        m_old = m_ref[...]
        m_new = jnp.maximum(m_old, jnp.max(s, axis=-1, keepdims=True))
        p = jnp.exp(s - m_new)
        alpha = jnp.exp(m_old - m_new)
        l_ref[...] = alpha * l_ref[...] + jnp.sum(p, axis=-1, keepdims=True)
        acc_ref[...] = alpha * acc_ref[...] + _dot(p.astype(BF16), vt)
        m_ref[...] = m_new
        return carry

    lax.fori_loop(0, i + 1, body, 0)
    o_ref[...] = acc_ref[...] / l_ref[...]


def _fox_prompt(q, k, v, ft, *, batch, heads):
    m, d = q.shape
    l = m // batch
    hd = d // heads
    tq = _tile(l, 512)
    nq = l // tq
    fcol = ft[..., None]
    frow = ft[:, :, None, :]
    return pl.pallas_call(
        functools.partial(_fox_prompt_kernel, tq=tq),
        grid=(batch, heads, nq),
        in_specs=[
            pl.BlockSpec((tq, hd), lambda b, h, i: (b * nq + i, h)),
            pl.BlockSpec((l, hd), lambda b, h, i: (b, h)),
            pl.BlockSpec((l, hd), lambda b, h, i: (b, h)),
            pl.BlockSpec((1, 1, tq, 1), lambda b, h, i: (b, h, i, 0)),
            pl.BlockSpec((1, 1, 1, l), lambda b, h, i: (b, h, 0, 0)),
        ],
        out_specs=pl.BlockSpec((tq, hd), lambda b, h, i: (b * nq + i, h)),
        out_shape=jax.ShapeDtypeStruct((m, d), F32),
        scratch_shapes=[pltpu.VMEM((tq, 1), F32), pltpu.VMEM((tq, 1), F32), pltpu.VMEM((tq, hd), F32)],
        compiler_params=_params(3, 32),
    )(q, k, v, fcol, frow)


def _fpast_kernel(pt_ref, lft_ref, o_ref, car_ref):
    page = lft_ref.shape[-1]

    @pl.when(pl.program_id(1) == 0)
    def _():
        car_ref[...] = jnp.zeros_like(car_ref)

    triu = (lax.broadcasted_iota(jnp.int32, (page, page), 0) <= lax.broadcasted_iota(jnp.int32, (page, page), 1)).astype(F32)
    ft = _dot(lft_ref[0], triu, HIGHEST) + car_ref[...]
    o_ref[0] = ft
    car_ref[...] = ft[:, page - 1:page]


def _fpast(page_table, cache_logf_t):
    b, n_pages = page_table.shape
    _, hh, page = cache_logf_t.shape
    return pl.pallas_call(
        _fpast_kernel,
        grid_spec=pltpu.PrefetchScalarGridSpec(
            num_scalar_prefetch=1,
            grid=(b, n_pages),
            in_specs=[pl.BlockSpec((1, hh, page), lambda bi, p, pt: (pt[bi, p], 0, 0))],
            out_specs=pl.BlockSpec((1, hh, page), lambda bi, p, pt: (bi, 0, p)),
            scratch_shapes=[pltpu.VMEM((hh, 1), F32)],
        ),
        out_shape=jax.ShapeDtypeStruct((b, hh, n_pages * page), F32),
        compiler_params=_params(2, 32),
    )(page_table, cache_logf_t)


def _fox_sample_kernel(pt_ref, q_ref, kc_ref, vc_ref, fp_ref, kn_ref, vn_ref, fq_ref, fn_ref, o_ref,
                       qbd_ref, m_ref, l_ref, acc_ref, *, heads):
    p = pl.program_id(1)
    t, d = q_ref.shape
    hd = d // heads
    r = heads * t
    page = kc_ref.shape[1]

    @pl.when(p == 0)
    def _():
        qt = jnp.concatenate([q_ref[...] * (hd ** -0.5)] * heads, axis=0)
        rh = lax.broadcasted_iota(jnp.int32, (r, d), 0) // t
        ch = lax.broadcasted_iota(jnp.int32, (r, d), 1) // hd
        qbd_ref[...] = jnp.where(rh == ch, qt, 0.0).astype(BF16)
        m_ref[...] = jnp.full_like(m_ref, MASK_VALUE)
        l_ref[...] = jnp.zeros_like(l_ref)
        acc_ref[...] = jnp.zeros_like(acc_ref)

    def expand(f):
        return jnp.concatenate([jnp.broadcast_to(f[h:h + 1], (t, page)) for h in range(heads)], axis=0)

    def update(kp, vp, fk, mask):
        s = _dot_nt(qbd_ref[...], kp) + fq_ref[0] - expand(fk)
        if mask is not None:
            s = jnp.where(mask, s, MASK_VALUE)
        m_old = m_ref[...]
        m_new = jnp.maximum(m_old, jnp.max(s, axis=-1, keepdims=True))
        pr = jnp.exp(s - m_new)
        alpha = jnp.exp(m_old - m_new)
        l_ref[...] = alpha * l_ref[...] + jnp.sum(pr, axis=-1, keepdims=True)
        m_ref[...] = m_new
        pv = _dot(pr.astype(BF16), vp)
        for h in range(heads):
            rs = slice(h * t, (h + 1) * t)
            cs = slice(h * hd, (h + 1) * hd)
            acc_ref[:, cs] = alpha[rs] * acc_ref[:, cs] + pv[rs, cs]

    update(kc_ref[0].astype(BF16), vc_ref[0].astype(BF16), fp_ref[0], None)

    @pl.when(p == pl.num_programs(1) - 1)
    def _():
        zeros = jnp.zeros((page - t, d), F32)
        kn = jnp.concatenate([kn_ref[...], zeros], axis=0).astype(BF16)
        vn = jnp.concatenate([vn_ref[...], zeros], axis=0).astype(BF16)
        key = lax.broadcasted_iota(jnp.int32, (r, page), 1)
        tok = lax.broadcasted_iota(jnp.int32, (r, page), 0) % t
        update(kn, vn, fn_ref[0], key <= tok)
        lsum = l_ref[...]
        for h in range(heads):
            rs = slice(h * t, (h + 1) * t)
            cs = slice(h * hd, (h + 1) * hd)
            o_ref[:, cs] = acc_ref[:, cs] / lsum[rs]


def _fox_sample(q, cache_k, cache_v, page_table, fpast_t, k_new, v_new, fq_col, fnew_t_pad, *, heads):
    b, n_pages = page_table.shape
    m, d = q.shape
    t = m // b
    page = cache_k.shape[1]
    r = heads * t
    return pl.pallas_call(
        functools.partial(_fox_sample_kernel, heads=heads),
        grid_spec=pltpu.PrefetchScalarGridSpec(
            num_scalar_prefetch=1,
            grid=(b, n_pages),
            in_specs=[
                pl.BlockSpec((t, d), lambda bi, p, pt: (bi, 0)),
                pl.BlockSpec((1, page, d), lambda bi, p, pt: (pt[bi, p], 0, 0)),
                pl.BlockSpec((1, page, d), lambda bi, p, pt: (pt[bi, p], 0, 0)),
                pl.BlockSpec((1, heads, page), lambda bi, p, pt: (bi, 0, p)),
                pl.BlockSpec((t, d), lambda bi, p, pt: (bi, 0)),
                pl.BlockSpec((t, d), lambda bi, p, pt: (bi, 0)),
                pl.BlockSpec((1, r, 1), lambda bi, p, pt: (bi, 0, 0)),
                pl.BlockSpec((1, heads, page), lambda bi, p, pt: (bi, 0, 0)),
            ],
            out_specs=pl.BlockSpec((t, d), lambda bi, p, pt: (bi, 0)),
            scratch_shapes=[pltpu.VMEM((r, d), BF16), pltpu.VMEM((r, 1), F32), pltpu.VMEM((r, 1), F32),
                            pltpu.VMEM((t, d), F32)],
        ),
        out_shape=jax.ShapeDtypeStruct((m, d), F32),
        compiler_params=_params(2, 32),
    )(page_table, q, cache_k, cache_v, fpast_t, k_new, v_new, fq_col, fnew_t_pad)


def _trunk(x3, mods, s0, attend_ctx, wts):
    (norm_pre, norm_post, a_w_in, a_w_out, a_lb_logits, a_out_norm, kv_norm, w_kv, b_forget,
     b_w_q, b_w_o, mlp_w_in, mlp_w_out) = wts
    b, l, d = x3.shape
    m = b * l
    depth = norm_pre.shape[0]
    n_a = a_w_in.shape[0]
    a_heads = s0.shape[2]
    b_heads = b_forget.shape[0]
    x = x3.reshape(m, d)
    if m <= 512:
        rows_per_mod = m
        expand = lambda t: jnp.repeat(t, l, axis=0)[None]
    else:
        rows_per_mod = l
        expand = lambda t: t[:, None, :]
    row = lambda t: t.reshape(1, -1)
    states = []
    shared = None
    for layer in range(depth):
        mod = expand(mods[layer, 0])
        if layer < n_a:
            proj = _norm_mm(x, row(norm_pre[layer, 0]), mod, a_w_in[layer], rows_per_mod=rows_per_mod)
            o, s_new = _hgrn(proj.reshape(b, l, -1), a_lb_logits, row(a_out_norm[layer]), s0[layer],
                             layer=layer, heads=a_heads)
            states.append(s_new)
            y_in, w_o = o.reshape(m, d), a_w_out[layer]
        else:
            j = layer - n_a
            if shared is None:
                kvg = row(kv_norm)
                k = _norm_mm(x, kvg, None, w_kv, rows_per_mod=rows_per_mod, n_out=d, col_off=0)
                v = _norm_mm(x, kvg, None, w_kv, rows_per_mod=rows_per_mod, n_out=d, col_off=d)
                wft = w_kv[:, 2 * d:].T
                shared = attend_ctx["make"](x, kvg, wft, b_forget.reshape(b_heads, 1), k, v)
            q = _norm_mm(x, row(norm_pre[layer, 0]), mod, b_w_q[j], rows_per_mod=rows_per_mod)
            y_in, w_o = attend_ctx["attend"](q, shared), b_w_o[j]
        x = _mm_post(y_in, w_o, x, mod, row(norm_post[layer, 0]), rows_per_mod=rows_per_mod)
        mod = expand(mods[layer, 1])
        x = _mlp(x, row(norm_pre[layer, 1]), mod, mlp_w_in[layer], mlp_w_out[layer], row(norm_post[layer, 1]),
                 rows_per_mod=rows_per_mod)
    k, v, lft = shared["rows"]
    kv_shape = (b, l, b_heads, d // b_heads)
    return (x.reshape(b, l, d), jnp.stack(states), k.reshape(kv_shape), v.reshape(kv_shape),
            jnp.swapaxes(lft, 1, 2))


def kernel(x_prompt, x_sample, c_prompt, c_sample, state_hgrn, cache_k, cache_v, cache_logf, page_table,
           norm_pre, norm_post, w_ada, b_ada, a_w_in, a_w_out, a_lb_logits, a_out_norm,
           kv_norm, w_kv, b_forget, b_w_q, b_w_o, mlp_w_in, mlp_w_out):
    bp, _, d = x_prompt.shape
    bs, ts, _ = x_sample.shape
    depth = norm_pre.shape[0]
    heads = b_forget.shape[0]
    n_phys, page, _, _ = cache_k.shape
    wts = (norm_pre, norm_post, a_w_in, a_w_out, a_lb_logits, a_out_norm, kv_norm, w_kv, b_forget,
           b_w_q, b_w_o, mlp_w_in, mlp_w_out)

    n_c = bp + bs
    c_rows = -(-n_c // 8) * 8
    c_all = jnp.concatenate([c_prompt, c_sample, jnp.zeros((c_rows - n_c, d), F32)], axis=0)
    mods = _ada_all(c_all, w_ada.reshape(depth * 2, d, 3 * d), b_ada.reshape(depth * 2, 1, 3 * d))
    mods = mods.reshape(depth, 2, c_rows, 3 * d)
    mods_p, mods_s = mods[:, :, :bp], mods[:, :, bp:n_c]

    def make_prompt(x, kvg, wft, bft, k, v):
        lft, ft = _logf(x, kvg, wft, bft, jnp.zeros((bp, heads, 1), F32), batch=bp)
        return {"rows": (k, v, lft), "k": k, "v": v, "ft": ft}

    def attend_prompt(q, sh):
        return _fox_prompt(q, sh["k"], sh["v"], sh["ft"], batch=bp, heads=heads)

    s0_prompt = jnp.zeros((a_w_in.shape[0], bp) + state_hgrn.shape[2:], F32)
    y_p, st_p, k_p, v_p, lf_p = _trunk(x_prompt, mods_p, s0_prompt,
                                       {"make": make_prompt, "attend": attend_prompt}, wts)

    ck = cache_k.reshape(n_phys, page, d)
    cv = cache_v.reshape(n_phys, page, d)
    fpast_t = _fpast(page_table, jnp.swapaxes(cache_logf.astype(F32), 1, 2))

    def make_sample(x, kvg, wft, bft, k, v):
        lft, fnew_t = _logf(x, kvg, wft, bft, fpast_t[:, :, -1:], batch=bs)
        fq_col = fnew_t.reshape(bs, heads * ts, 1)
        fnew_pad = jnp.pad(fnew_t, ((0, 0), (0, 0), (0, page - ts)))
        return {"rows": (k, v, lft), "k": k, "v": v, "fq": fq_col, "fn": fnew_pad}

    def attend_sample(q, sh):
        return _fox_sample(q, ck, cv, page_table, fpast_t, sh["k"], sh["v"], sh["fq"], sh["fn"], heads=heads)

    y_s, st_s, k_s, v_s, lf_s = _trunk(x_sample, mods_s, state_hgrn,
                                       {"make": make_sample, "attend": attend_sample}, wts)

    return (y_p, y_s, st_p, st_s, k_p, v_p, lf_p, k_s, v_s, lf_s)
```

```python
import functools
import math

import jax
import jax.numpy as jnp
from jax import lax
from jax.experimental import pallas as pl
from jax.experimental.pallas import tpu as pltpu

EPS = 1e-6
MASK_VALUE = -1e30
TINY = 1e-30
LOG2E = math.log2(math.e)
F32 = jnp.float32
BF16 = jnp.bfloat16
HIGHEST = lax.Precision.HIGHEST

LANES = 128
HGRN_CHUNK = 128
HGRN_SUB = 8
PAGES_PER_STEP = 4
HGRN_HEADS_PER_STEP = 4
ATTN_HEADS_PER_STEP = 4
MIB = 2 ** 20


def _params(n_axes, vmem_mib):
    return pltpu.CompilerParams(
        dimension_semantics=("arbitrary",) * n_axes, vmem_limit_bytes=vmem_mib * MIB)


def _tile(n, pref):
    t = min(n, pref)
    while n % t:
        t -= 1
    return t


def _sigmoid(x):
    return 1.0 / (1.0 + jnp.exp(-x))


def _log_sigmoid(x):
    return jnp.minimum(x, 0.0) - jnp.log1p(jnp.exp(-jnp.abs(x)))


def _rms(x, g):
    return x * lax.rsqrt(jnp.mean(x * x, axis=-1, keepdims=True) + EPS) * g


def _dot(a, b, precision=None):
    return jnp.dot(a, b, preferred_element_type=F32, precision=precision)


def _dot_nt(a, b, precision=None):
    return lax.dot_general(a, b, (((1,), (1,)), ((), ())), preferred_element_type=F32, precision=precision)


def _iota(shape, dim):
    return lax.broadcasted_iota(jnp.int32, shape, dim)


def _ada_kernel(c_ref, w_ref, b_ref, o_ref):
    c = c_ref[...]
    s = (c * _sigmoid(c)).astype(BF16)
    o_ref[0] = _dot(s, w_ref[0].astype(BF16)) + b_ref[0]


def _ada_all(c_all, w_ada, b_ada):
    n, d, n3 = w_ada.shape
    r = c_all.shape[0]
    tn = _tile(n3, 1024)
    return pl.pallas_call(
        _ada_kernel,
        grid=(n, n3 // tn),
        in_specs=[
            pl.BlockSpec((r, d), lambda i, j: (0, 0)),
            pl.BlockSpec((1, d, tn), lambda i, j: (i, 0, j)),
            pl.BlockSpec((1, 1, tn), lambda i, j: (i, 0, j)),
        ],
        out_specs=pl.BlockSpec((1, r, tn), lambda i, j: (i, 0, j)),
        out_shape=jax.ShapeDtypeStruct((n, r, n3), F32),
        compiler_params=_params(2, 40),
    )(c_all, w_ada, b_ada)


def _norm_mm_kernel(*refs, has_mod, n_out):
    if has_mod:
        x_ref, g_ref, sh_ref, sc_ref, w_ref = refs[:5]
        rest = refs[5:]
    else:
        x_ref, g_ref, w_ref = refs[:3]
        rest = refs[3:]
    o_refs, h_ref = rest[:n_out], rest[n_out]

    @pl.when(pl.program_id(1) == 0)
    def _():
        h = _rms(x_ref[...], g_ref[...])
        if has_mod:
            h = h * (1.0 + sc_ref[0]) + sh_ref[0]
        h_ref[...] = h.astype(BF16)

    y = _dot(h_ref[...], w_ref[...])
    for o_ref in o_refs:
        o_ref[...] = y.astype(o_ref.dtype)


def _norm_mm(x, g, mod, w, layer, *, rows_per_mod, n_cols=None, col_off=0, out_dtypes=(F32,),
             tm_pref=1024, tn_pref=1024):
    m, d = x.shape
    n_cols = w.shape[2] if n_cols is None else n_cols
    tm = _tile(m, tm_pref)
    tn = _tile(n_cols, tn_pref)
    assert col_off % tn == 0
    joff = col_off // tn
    in_specs = [pl.BlockSpec((tm, d), lambda i, j: (i, 0)), pl.BlockSpec((1, d), lambda i, j: (0, 0))]
    args = [x, g]
    if mod is not None:
        r = mod.shape[1]
        assert rows_per_mod % tm == 0 and r in (1, tm)
        bidx = lambda i: (i * tm) // rows_per_mod
        in_specs += [pl.BlockSpec((1, r, d), lambda i, j: (bidx(i), 0, 0)),
                     pl.BlockSpec((1, r, d), lambda i, j: (bidx(i), 0, 1))]
        args += [mod, mod]
    in_specs.append(pl.BlockSpec((None, d, tn), lambda i, j: (layer, 0, j + joff)))
    args.append(w)
    outs = pl.pallas_call(
        functools.partial(_norm_mm_kernel, has_mod=mod is not None, n_out=len(out_dtypes)),
        grid=(m // tm, n_cols // tn),
        in_specs=in_specs,
        out_specs=[pl.BlockSpec((tm, tn), lambda i, j: (i, j)) for _ in out_dtypes],
        out_shape=[jax.ShapeDtypeStruct((m, n_cols), dt) for dt in out_dtypes],
        scratch_shapes=[pltpu.VMEM((tm, d), BF16)],
        compiler_params=_params(2, 56),
    )(*args)
    return outs if len(outs) > 1 else outs[0]


def _mm_post_kernel(a_ref, w_ref, x_ref, gate_ref, g_ref, o_ref):
    y = _dot(a_ref[...].astype(BF16), w_ref[...])
    o_ref[...] = x_ref[...] + gate_ref[0] * _rms(y, g_ref[...])


def _mm_post(a, w, layer, x, mod, g, *, rows_per_mod, tm_pref=512):
    m, kdim = a.shape
    d = w.shape[2]
    tm = _tile(m, tm_pref)
    r = mod.shape[1]
    assert rows_per_mod % tm == 0 and r in (1, tm)
    bidx = lambda i: (i * tm) // rows_per_mod
    return pl.pallas_call(
        _mm_post_kernel,
        grid=(m // tm,),
        in_specs=[
            pl.BlockSpec((tm, kdim), lambda i: (i, 0)),
            pl.BlockSpec((None, kdim, d), lambda i: (layer, 0, 0)),
            pl.BlockSpec((tm, d), lambda i: (i, 0)),
            pl.BlockSpec((1, r, d), lambda i: (bidx(i), 0, 2)),
            pl.BlockSpec((1, d), lambda i: (0, 0)),
        ],
        out_specs=pl.BlockSpec((tm, d), lambda i: (i, 0)),
        out_shape=jax.ShapeDtypeStruct((m, d), F32),
        compiler_params=_params(1, 56),
    )(a, w, x, mod, g)


def _mlp_kernel(x_ref, gpre_ref, sh_ref, sc_ref, gate_ref, w1_ref, w2_ref, gpost_ref, o_ref, h_ref, acc_ref):
    f = pl.program_id(1)

    @pl.when(f == 0)
    def _():
        h = _rms(x_ref[...], gpre_ref[...]) * (1.0 + sc_ref[0]) + sh_ref[0]
        h_ref[...] = h.astype(BF16)
        acc_ref[...] = jnp.zeros_like(acc_ref)

    u = jnp.maximum(_dot(h_ref[...], w1_ref[...]), 0.0)
    acc_ref[...] += _dot((u * u).astype(BF16), w2_ref[...])

    @pl.when(f == pl.num_programs(1) - 1)
    def _():
        o_ref[...] = x_ref[...] + gate_ref[0] * _rms(acc_ref[...], gpost_ref[...])


def _mlp(x, gpre, mod, w1, w2, layer, gpost, *, rows_per_mod, tm_pref=512, tf_pref=512):
    m, d = x.shape
    ff = w1.shape[2]
    tm = _tile(m, tm_pref)
    tf = _tile(ff, tf_pref)
    r = mod.shape[1]
    assert rows_per_mod % tm == 0 and r in (1, tm)
    bidx = lambda i: (i * tm) // rows_per_mod
    return pl.pallas_call(
        _mlp_kernel,
        grid=(m // tm, ff // tf),
        in_specs=[
            pl.BlockSpec((tm, d), lambda i, f: (i, 0)),
            pl.BlockSpec((1, d), lambda i, f: (0, 0)),
            pl.BlockSpec((1, r, d), lambda i, f: (bidx(i), 0, 0)),
            pl.BlockSpec((1, r, d), lambda i, f: (bidx(i), 0, 1)),
            pl.BlockSpec((1, r, d), lambda i, f: (bidx(i), 0, 2)),
            pl.BlockSpec((None, d, tf), lambda i, f: (layer, 0, f)),
            pl.BlockSpec((None, tf, d), lambda i, f: (layer, f, 0)),
            pl.BlockSpec((1, d), lambda i, f: (0, 0)),
        ],
        out_specs=pl.BlockSpec((tm, d), lambda i, f: (i, 0)),
        out_shape=jax.ShapeDtypeStruct((m, d), F32),
        scratch_shapes=[pltpu.VMEM((tm, d), BF16), pltpu.VMEM((tm, d), F32)],
        compiler_params=_params(2, 56),
    )(x, gpre, mod, mod, mod, w1, w2, gpost)


def _hgrn_kernel(q_ref, f_ref, i_ref, g_ref, lbl_ref, og_ref, s0_ref, o_ref, s_ref, st_ref,
                 *, layer, rows, n_inner, hpb):
    c = HGRN_CHUNK
    sub = HGRN_SUB
    dk = q_ref.shape[-1] // hpb
    step = pl.program_id(2)

    @pl.when(step == 0)
    def _():
        for h in range(hpb):
            st_ref[h] = s0_ref[0, h].T

    z = lbl_ref[...]
    e = jnp.exp(z - jnp.max(z, axis=0, keepdims=True))
    p = e / jnp.sum(e, axis=0, keepdims=True)
    cs = p[0:1]
    for r in range(1, layer + 1):
        cs = cs + p[r:r + 1]
    lb_all = cs - p[0:1]
    og_all = og_ref[...]

    tril = (_iota((c, c), 0) >= _iota((c, c), 1)).astype(BF16)
    lane = _iota((sub, c), 1)
    sub_row = _iota((sub, c), 0)
    scale = dk ** -0.5

    def pad(t, fill=0.0):
        if rows == c:
            return t
        return jnp.concatenate([t, jnp.full((c - rows, t.shape[1]), fill, t.dtype)], axis=0)

    def cumsum_rows(x):
        hi = x.astype(BF16)
        r1 = x - hi.astype(F32)
        mid = r1.astype(BF16)
        lo = (r1 - mid.astype(F32)).astype(BF16)
        return _dot(tril, hi) + _dot(tril, mid) + _dot(tril, lo)

    def head_chunk(h, sl):
        hs = slice(h * dk, (h + 1) * dk)
        lb = lb_all[:, hs]
        lb_floor = jnp.maximum(lb, TINY)
        one_m_lb = 1.0 - lb
        log2_one_m_lb = jnp.log(one_m_lb) * LOG2E
        qz = q_ref[0, sl, hs]
        fz = f_ref[0, sl, hs]
        q = pad(qz * _sigmoid(qz))
        e = jnp.exp(-jnp.abs(fz))
        t = 1.0 + e
        r = 1.0 / t
        sig = jnp.where(fz >= 0.0, r, e * r)
        logf = pad(jnp.log(lb_floor + one_m_lb * sig))
        lk = pad(log2_one_m_lb + (jnp.minimum(-fz, 0.0) - jnp.log(t)) * LOG2E, -jnp.inf)
        v = pad(i_ref[0, sl, hs])
        b2 = cumsum_rows(logf) * LOG2E
        c2 = b2 - lk
        st = st_ref[h]
        inter = _dot_nt((q * jnp.exp2(b2)).astype(BF16), st.astype(BF16))
        blocks = []
        for i in range(c // sub):
            lo = i * sub
            qi, bi, ci = q[lo:lo + sub], b2[lo:lo + sub], c2[lo:lo + sub]
            if i > 0:
                ref_b = bi[0:1]
                qs = (qi * jnp.exp2(bi - ref_b)).astype(BF16)
                kt = jnp.exp2(ref_b - c2[:lo]).astype(BF16)
                kt = jnp.concatenate([kt, jnp.zeros((c - lo, dk), BF16)], axis=0)
                s_i = _dot_nt(qs, kt)
            else:
                s_i = jnp.zeros((sub, c), F32)
            for s in range(sub):
                w = jnp.sum(qi * jnp.exp2(bi - ci[s:s + 1]), axis=-1, keepdims=True)
                s_i = jnp.where(lane == lo + s, w, s_i)
            blocks.append(jnp.where(lane <= lo + sub_row, s_i, 0.0))
        scores = jnp.concatenate(blocks, axis=0).astype(BF16)
        o = (inter + _dot(scores, v.astype(BF16))) * scale
        blast = b2[c - 1:c]
        kd = jnp.exp2(blast - c2).astype(BF16)
        st_ref[h] = st * jnp.exp2(blast) + _dot(v.T.astype(BF16), kd)
        gz = g_ref[0, sl, hs]
        o_ref[0, sl, hs] = _rms(o[:rows], og_all[:, hs]) * (gz * _sigmoid(gz))

    def chunk(j, carry):
        if n_inner == 1:
            sl = pl.ds(0, rows)
        else:
            sl = pl.ds(pl.multiple_of(j * c, c), c)
        for h in range(hpb):
            head_chunk(h, sl)
        return carry

    lax.fori_loop(0, n_inner, chunk, 0)

    @pl.when(step == pl.num_programs(2) - 1)
    def _():
        for h in range(hpb):
            s_ref[0, h] = st_ref[h].T


def _hgrn(proj, lb_logits, out_g, s0, *, layer, heads):
    b, l, _ = proj.shape
    dk = s0.shape[2]
    dv = s0.shape[3]
    assert dk == LANES and dv == LANES
    c = HGRN_CHUNK
    if l >= c:
        assert l % c == 0
        rows = c
        n_inner = _tile(l // c, 4)
        blk = rows * n_inner
    else:
        assert l % 8 == 0
        rows, n_inner, blk = l, 1, l
    nl = lb_logits.shape[0]
    hpb = _tile(heads, HGRN_HEADS_PER_STEP)
    ng = heads // hpb
    col = lambda part: pl.BlockSpec((1, blk, hpb * dk), lambda bi, h, s: (bi, s, part * ng + h))
    o, s_new = pl.pallas_call(
        functools.partial(_hgrn_kernel, layer=layer, rows=rows, n_inner=n_inner, hpb=hpb),
        grid=(b, ng, l // blk),
        in_specs=[
            col(0), col(1), col(2), col(3),
            pl.BlockSpec((nl, hpb * dk), lambda bi, h, s: (0, h)),
            pl.BlockSpec((1, hpb * dv), lambda bi, h, s: (0, h)),
            pl.BlockSpec((1, hpb, dk, dv), lambda bi, h, s: (bi, h, 0, 0)),
        ],
        out_specs=[
            pl.BlockSpec((1, blk, hpb * dv), lambda bi, h, s: (bi, s, h)),
            pl.BlockSpec((1, hpb, dk, dv), lambda bi, h, s: (bi, h, 0, 0)),
        ],
        out_shape=[jax.ShapeDtypeStruct((b, l, heads * dv), F32),
                   jax.ShapeDtypeStruct((b, heads, dk, dv), F32)],
        scratch_shapes=[pltpu.VMEM((hpb, dv, dk), F32)],
        compiler_params=_params(3, 32),
    )(proj, proj, proj, proj, lb_logits, out_g, s0)
    return o, s_new


def _logf_kernel(x_ref, g_ref, wft_ref, bft_ref, c0_ref, lf_ref, ft_ref, car_ref):
    tm = x_ref.shape[0]

    @pl.when(pl.program_id(1) == 0)
    def _():
        car_ref[...] = c0_ref[0]

    h = _rms(x_ref[...], g_ref[...])
    zt = _dot_nt(wft_ref[...], h, HIGHEST) + bft_ref[...]
    lft = _log_sigmoid(zt)
    lf_ref[0] = lft
    triu = (_iota((tm, tm), 0) <= _iota((tm, tm), 1)).astype(F32)
    ft = _dot(lft, triu, HIGHEST) + car_ref[...]
    ft_ref[0] = ft
    car_ref[...] = ft[:, tm - 1:tm]


def _logf(x, g, wft, bft, c0t, *, batch):
    m, d = x.shape
    l = m // batch
    hh = wft.shape[0]
    tm = _tile(l, 512)
    nt = l // tm
    out = jax.ShapeDtypeStruct((batch, hh, l), F32)
    return pl.pallas_call(
        _logf_kernel,
        grid=(batch, nt),
        in_specs=[
            pl.BlockSpec((tm, d), lambda b, i: (b * nt + i, 0)),
            pl.BlockSpec((1, d), lambda b, i: (0, 0)),
            pl.BlockSpec((hh, d), lambda b, i: (0, 0)),
            pl.BlockSpec((hh, 1), lambda b, i: (0, 0)),
            pl.BlockSpec((1, hh, 1), lambda b, i: (b, 0, 0)),
        ],
        out_specs=[pl.BlockSpec((1, hh, tm), lambda b, i: (b, 0, i)),
                   pl.BlockSpec((1, hh, tm), lambda b, i: (b, 0, i))],
        out_shape=[out, out],
        scratch_shapes=[pltpu.VMEM((hh, 1), F32)],
        compiler_params=_params(2, 32),
    )(x, g, wft, bft, c0t)


def _fox_prompt_kernel(q_ref, k_ref, v_ref, fq_ref, fk_ref, o_ref, m_ref, l_ref, acc_ref, *, tq, hpb):
    i = pl.program_id(2)
    hd = q_ref.shape[-1] // hpb
    n_lt = tq // LANES
    m_ref[...] = jnp.full_like(m_ref, MASK_VALUE)
    l_ref[...] = jnp.zeros_like(l_ref)
    acc_ref[...] = jnp.zeros_like(acc_ref)
    qs, fqs = [], []
    for h in range(hpb):
        qs.append((q_ref[:, h * hd:(h + 1) * hd].astype(F32) * (hd ** -0.5 * LOG2E)).astype(BF16))
        fqs.append(jnp.broadcast_to(fq_ref[0, h] * LOG2E, (tq, LANES)))

    def tile(j, masked):
        ks = pl.ds(pl.multiple_of(j * tq, tq), tq)
        for h in range(hpb):
            cs = slice(h * hd, (h + 1) * hd)
            s = _dot_nt(qs[h], k_ref[ks, cs])
            fk = fk_ref[0, h, :, ks] * LOG2E
            us = []
            for c in range(n_lt):
                ls = slice(c * LANES, (c + 1) * LANES)
                u = s[:, ls] - fk[:, ls]
                if masked:
                    u = jnp.where(_iota((tq, LANES), 0) >= c * LANES + _iota((tq, LANES), 1), u, MASK_VALUE)
                us.append(u)
            row_max = jnp.max(functools.reduce(jnp.maximum, us), axis=-1, keepdims=True)
            m_old = m_ref[h]
            m_new = jnp.maximum(m_old, jnp.broadcast_to(row_max, (tq, LANES)) + fqs[h])
            shift = fqs[h] - m_new
            ps = [jnp.exp2(u + shift) for u in us]
            alpha = jnp.exp2(m_old - m_new)
            m_ref[h] = m_new
            l_ref[h] = alpha * l_ref[h] + functools.reduce(jnp.add, ps)
            p_all = jnp.concatenate([p.astype(BF16) for p in ps], axis=1)
            acc_ref[h] = alpha * acc_ref[h] + _dot(p_all, v_ref[ks, cs])

    def body(j, carry):
        tile(j, False)
        return carry

    lax.fori_loop(0, i, body, 0)
    tile(i, True)
    for h in range(hpb):
        o_ref[:, h * hd:(h + 1) * hd] = acc_ref[h] / jnp.sum(l_ref[h], axis=-1, keepdims=True)


def _fox_prompt(q, k, v, ft, *, batch, heads):
    m, d = q.shape
    l = m // batch
    hd = d // heads
    assert hd == LANES
    hpb = ATTN_HEADS_PER_STEP
    tq = _tile(l, 512)
    nq = l // tq
    fcol = ft[..., None]
    frow = ft[:, :, None, :]
    return pl.pallas_call(
        functools.partial(_fox_prompt_kernel, tq=tq, hpb=hpb),
        grid=(batch, heads // hpb, nq),
        in_specs=[
            pl.BlockSpec((tq, hpb * hd), lambda b, h, i: (b * nq + i, h)),
            pl.BlockSpec((l, hpb * hd), lambda b, h, i: (b, h)),
            pl.BlockSpec((l, hpb * hd), lambda b, h, i: (b, h)),
            pl.BlockSpec((1, hpb, tq, 1), lambda b, h, i: (b, h, i, 0)),
            pl.BlockSpec((1, hpb, 1, l), lambda b, h, i: (b, h, 0, 0)),
        ],
        out_specs=pl.BlockSpec((tq, hpb * hd), lambda b, h, i: (b * nq + i, h)),
        out_shape=jax.ShapeDtypeStruct((m, d), F32),
        scratch_shapes=[pltpu.VMEM((hpb, tq, LANES), F32), pltpu.VMEM((hpb, tq, LANES), F32),
                        pltpu.VMEM((hpb, tq, hd), F32)],
        compiler_params=_params(3, 40),
    )(q, k, v, fcol, frow)


def _fpast_kernel(pt_ref, lf_ref, o_ref, car_ref, *, heads):
    rows = lf_ref.shape[1]

    @pl.when(pl.program_id(1) == 0)
    def _():
        car_ref[...] = jnp.zeros_like(car_ref)

    li, lj = _iota((LANES, LANES), 0), _iota((LANES, LANES), 1)
    same_head = (li % heads) == (lj % heads)
    prefix = (same_head & (li // heads <= lj // heads)).astype(F32)
    last = (same_head & (li // heads == LANES // heads - 1)).astype(F32)
    strict = (_iota((rows, rows), 0) > _iota((rows, rows), 1)).astype(F32)
    y = _dot(lf_ref[0], prefix, HIGHEST)
    tot = _dot(y, last, HIGHEST)
    o_ref[0] = y + _dot(strict, tot, HIGHEST) + car_ref[...]
    car_ref[...] += jnp.sum(tot, axis=0, keepdims=True)


def _fpast(page_table, cache_logf_flat, *, heads):
    b, n_pages = page_table.shape
    _, rows, _ = cache_logf_flat.shape
    return pl.pallas_call(
        functools.partial(_fpast_kernel, heads=heads),
        grid_spec=pltpu.PrefetchScalarGridSpec(
            num_scalar_prefetch=1,
            grid=(b, n_pages),
            in_specs=[pl.BlockSpec((1, rows, LANES), lambda bi, p, pt: (pt[bi, p], 0, 0))],
            out_specs=pl.BlockSpec((1, rows, LANES), lambda bi, p, pt: (bi, p, 0)),
            scratch_shapes=[pltpu.VMEM((1, LANES), F32)],
        ),
        out_shape=jax.ShapeDtypeStruct((b, n_pages * rows, LANES), F32),
        compiler_params=_params(2, 32),
    )(page_table, cache_logf_flat)


def _fox_sample_kernel(pt_ref, q_ref, *refs, heads, n_pg):
    kc_refs, vc_refs, fp_refs = refs[:n_pg], refs[n_pg:2 * n_pg], refs[2 * n_pg:3 * n_pg]
    kn_ref, vn_ref, fq_ref, fn_ref, o_ref, qa_ref, m_ref, l_ref, acc_ref = refs[3 * n_pg:]
    step = pl.program_id(1)
    t, d = q_ref.shape
    hd = d // heads
    r = heads * t

    @pl.when(step == 0)
    def _():
        q = q_ref[...] * (hd ** -0.5 * LOG2E)
        qa_ref[...] = jnp.concatenate([q[:, h * hd:(h + 1) * hd] for h in range(heads)], axis=0).astype(BF16)
        m_ref[...] = jnp.full_like(m_ref, MASK_VALUE)
        l_ref[...] = jnp.zeros_like(l_ref)
        acc_ref[...] = jnp.zeros_like(acc_ref)

    fq = fq_ref[0] * LOG2E
    row_head = _iota((r, LANES), 0) // t
    lane_head = _iota((r, LANES), 1) % heads
    head_match = row_head == lane_head

    def update(k_flat, v_flat, fk_rows, mask):
        n = k_flat.shape[0]
        s = _dot_nt(qa_ref[...], k_flat)
        us = []
        for c in range(n // LANES):
            u = s[:, c * LANES:(c + 1) * LANES] - fk_rows[c:c + 1] * LOG2E
            us.append(jnp.where(mask, u, MASK_VALUE))
        umax = functools.reduce(jnp.maximum, us)
        m_old = m_ref[...]
        m_new = jnp.maximum(m_old, jnp.max(umax, axis=-1, keepdims=True) + fq)
        shift = fq - m_new
        ps = [jnp.exp2(u + shift) for u in us]
        alpha = jnp.exp2(m_old - m_new)
        l_ref[...] = alpha * l_ref[...] + jnp.sum(functools.reduce(jnp.add, ps), axis=-1, keepdims=True)
        m_ref[...] = m_new
        p_all = jnp.concatenate([p.astype(BF16) for p in ps], axis=1) if len(ps) > 1 else ps[0].astype(BF16)
        acc_ref[...] = alpha * acc_ref[...] + _dot(p_all, v_flat)

    for kc_ref, vc_ref, fp_ref in zip(kc_refs, vc_refs, fp_refs):
        page = kc_ref.shape[1]
        update(kc_ref[0].reshape(page * heads, hd).astype(BF16),
               vc_ref[0].reshape(page * heads, hd).astype(BF16), fp_ref[0], head_match)

    @pl.when(step == pl.num_programs(1) - 1)
    def _():
        key = _iota((r, LANES), 1) // heads
        tok = _iota((r, LANES), 0) % t
        update(kn_ref[0].reshape(t * heads, hd).astype(BF16), vn_ref[0].reshape(t * heads, hd).astype(BF16),
               fn_ref[0], head_match & (key <= tok))
        res = acc_ref[...] / l_ref[...]
        for h in range(heads):
            o_ref[:, h * hd:(h + 1) * hd] = res[h * t:(h + 1) * t]


def _fox_sample(q, cache_k, cache_v, page_table, fpast, k_new, v_new, fq_col, fnew_flat):
    b, n_pages = page_table.shape
    m, d = q.shape
    t = m // b
    _, page, heads, hd = cache_k.shape
    assert t * heads == LANES and hd == LANES
    r = heads * t
    n_pg = _tile(n_pages, PAGES_PER_STEP)
    frows = page * heads // LANES
    page_spec = lambda o: pl.BlockSpec((1, page, heads, hd), lambda bi, s, pt: (pt[bi, s * n_pg + o], 0, 0, 0))
    fp_spec = lambda o: pl.BlockSpec((1, frows, LANES), lambda bi, s, pt: (bi, s * n_pg + o, 0))
    new_spec = pl.BlockSpec((1, t, heads, hd), lambda bi, s, pt: (bi, 0, 0, 0))
    in_specs = ([pl.BlockSpec((t, d), lambda bi, s, pt: (bi, 0))]
                + [page_spec(o) for o in range(n_pg)] * 2
                + [fp_spec(o) for o in range(n_pg)]
                + [new_spec, new_spec,
                   pl.BlockSpec((1, r, 1), lambda bi, s, pt: (bi, 0, 0)),
                   pl.BlockSpec((1, 1, LANES), lambda bi, s, pt: (bi, 0, 0))])
    return pl.pallas_call(
        functools.partial(_fox_sample_kernel, heads=heads, n_pg=n_pg),
        grid_spec=pltpu.PrefetchScalarGridSpec(
            num_scalar_prefetch=1,
            grid=(b, n_pages // n_pg),
            in_specs=in_specs,
            out_specs=pl.BlockSpec((t, d), lambda bi, s, pt: (bi, 0)),
            scratch_shapes=[pltpu.VMEM((r, hd), BF16), pltpu.VMEM((r, 1), F32), pltpu.VMEM((r, 1), F32),
                            pltpu.VMEM((r, hd), F32)],
        ),
        out_shape=jax.ShapeDtypeStruct((m, d), F32),
        compiler_params=_params(2, 40),
    )(page_table, q, *([cache_k] * n_pg), *([cache_v] * n_pg), *([fpast] * n_pg), k_new, v_new, fq_col, fnew_flat)


def _trunk(x3, mods, s0, attend_ctx, wts):
    (norm_pre, norm_post, a_w_in, a_w_out, a_lb_logits, a_out_norm, kv_norm, w_kv, wft, b_forget,
     b_w_q, b_w_o, mlp_w_in, mlp_w_out) = wts
    b, l, d = x3.shape
    m = b * l
    depth = norm_pre.shape[0]
    n_a = a_w_in.shape[0]
    a_heads = s0.shape[2]
    b_heads = b_forget.shape[0]
    x = x3.reshape(m, d)
    if m <= 512:
        rows_per_mod = m
        expand = lambda t: jnp.repeat(t, l, axis=0)[None]
    else:
        rows_per_mod = l
        expand = lambda t: t[:, None, :]
    row = lambda t: t.reshape(1, -1)
    states = []
    shared = None
    for layer in range(depth):
        mod = expand(mods[layer, 0])
        if layer < n_a:
            proj = _norm_mm(x, row(norm_pre[layer, 0]), mod, a_w_in, layer, rows_per_mod=rows_per_mod)
            o, s_new = _hgrn(proj.reshape(b, l, -1), a_lb_logits, row(a_out_norm[layer]), s0[layer],
                             layer=layer, heads=a_heads)
            states.append(s_new)
            y_in, w_o, w_idx = o.reshape(m, d), a_w_out, layer
        else:
            j = layer - n_a
            if shared is None:
                kvg = row(kv_norm)
                k, k16 = _norm_mm(x, kvg, None, w_kv, 0, rows_per_mod=rows_per_mod, n_cols=d, col_off=0,
                                  out_dtypes=(F32, BF16))
                v, v16 = _norm_mm(x, kvg, None, w_kv, 0, rows_per_mod=rows_per_mod, n_cols=d, col_off=d,
                                  out_dtypes=(F32, BF16))
                shared = attend_ctx["make"](x, kvg, wft, b_forget.reshape(b_heads, 1), k, v, k16, v16)
            q = _norm_mm(x, row(norm_pre[layer, 0]), mod, b_w_q, j, rows_per_mod=rows_per_mod,
                         out_dtypes=(attend_ctx["q_dtype"],))
            y_in, w_o, w_idx = attend_ctx["attend"](q, shared), b_w_o, j
        x = _mm_post(y_in, w_o, w_idx, x, mod, row(norm_post[layer, 0]), rows_per_mod=rows_per_mod)
        mod = expand(mods[layer, 1])
        x = _mlp(x, row(norm_pre[layer, 1]), mod, mlp_w_in, mlp_w_out, layer, row(norm_post[layer, 1]),
                 rows_per_mod=rows_per_mod)
    k, v, lft = shared["rows"]
    kv_shape = (b, l, b_heads, d // b_heads)
    return (x.reshape(b, l, d), jnp.stack(states), k.reshape(kv_shape), v.reshape(kv_shape),
            jnp.swapaxes(lft, 1, 2))


def kernel(x_prompt, x_sample, c_prompt, c_sample, state_hgrn, cache_k, cache_v, cache_logf, page_table,
           norm_pre, norm_post, w_ada, b_ada, a_w_in, a_w_out, a_lb_logits, a_out_norm,
           kv_norm, w_kv, b_forget, b_w_q, b_w_o, mlp_w_in, mlp_w_out):
    bp, _, d = x_prompt.shape
    bs, ts, _ = x_sample.shape
    depth = norm_pre.shape[0]
    n_phys, page, heads, hd = cache_k.shape
    wts = (norm_pre, norm_post, a_w_in.astype(BF16), a_w_out.astype(BF16), a_lb_logits, a_out_norm, kv_norm,
           w_kv.astype(BF16)[None], w_kv[:, 2 * d:].T, b_forget,
           b_w_q.astype(BF16), b_w_o.astype(BF16), mlp_w_in.astype(BF16), mlp_w_out.astype(BF16))

    n_c = bp + bs
    c_rows = -(-n_c // 8) * 8
    c_all = jnp.concatenate([c_prompt, c_sample, jnp.zeros((c_rows - n_c, d), F32)], axis=0)
    mods = _ada_all(c_all, w_ada.reshape(depth * 2, d, 3 * d), b_ada.reshape(depth * 2, 1, 3 * d))
    mods = mods.reshape(depth, 2, c_rows, 3 * d)
    mods_p, mods_s = mods[:, :, :bp], mods[:, :, bp:n_c]

    def make_prompt(x, kvg, wft, bft, k, v, k16, v16):
        lft, ft = _logf(x, kvg, wft, bft, jnp.zeros((bp, heads, 1), F32), batch=bp)
        return {"rows": (k, v, lft), "k": k16, "v": v16, "ft": ft}

    def attend_prompt(q, sh):
        return _fox_prompt(q, sh["k"], sh["v"], sh["ft"], batch=bp, heads=heads)

    s0_prompt = jnp.zeros((a_w_in.shape[0], bp) + state_hgrn.shape[2:], F32)
    y_p, st_p, k_p, v_p, lf_p = _trunk(
        x_prompt, mods_p, s0_prompt, {"make": make_prompt, "attend": attend_prompt, "q_dtype": BF16}, wts)

    frows = page * heads // LANES
    fpast = _fpast(page_table, cache_logf.astype(F32).reshape(n_phys, frows, LANES), heads=heads)
    f_total = fpast[:, -1, LANES - heads:].reshape(bs, heads, 1)

    def make_sample(x, kvg, wft, bft, k, v, k16, v16):
        lft, fnew_t = _logf(x, kvg, wft, bft, f_total, batch=bs)
        fq_col = fnew_t.reshape(bs, heads * ts, 1)
        fnew_flat = jnp.swapaxes(fnew_t, 1, 2).reshape(bs, 1, ts * heads)
        shape4 = (bs, ts, heads, hd)
        return {"rows": (k, v, lft), "k": k.reshape(shape4), "v": v.reshape(shape4), "fq": fq_col, "fn": fnew_flat}

    def attend_sample(q, sh):
        return _fox_sample(q, cache_k, cache_v, page_table, fpast, sh["k"], sh["v"], sh["fq"], sh["fn"])

    y_s, st_s, k_s, v_s, lf_s = _trunk(
        x_sample, mods_s, state_hgrn, {"make": make_sample, "attend": attend_sample, "q_dtype": F32}, wts)

    return (y_p, y_s, st_p, st_s, k_p, v_p, lf_p, k_s, v_s, lf_s)
```

```python
import functools
import math

import jax
import jax.numpy as jnp
from jax import lax
from jax.experimental import pallas as pl
from jax.experimental.pallas import tpu as pltpu

EPS = 1e-6
MASK_VALUE = -1e30
TINY = 1e-30
LOG2E = math.log2(math.e)
F32 = jnp.float32
BF16 = jnp.bfloat16
HIGHEST = lax.Precision.HIGHEST

LANES = 128
HGRN_CHUNK = 128
HGRN_SUB = 8
SUBLANES = 8
HEAD_GROUP = SUBLANES
PAGES_PER_STEP = 4
FPAST_PAGES_PER_STEP = 8
HGRN_HEADS_PER_STEP = 4
ATTN_HEADS_PER_STEP = 4
MIB = 2 ** 20


def _params(n_axes, vmem_mib):
    return pltpu.CompilerParams(
        dimension_semantics=("arbitrary",) * n_axes, vmem_limit_bytes=vmem_mib * MIB)


def _tile(n, pref):
    t = min(n, pref)
    while n % t:
        t -= 1
    return t


def _sigmoid(x):
    return 1.0 / (1.0 + jnp.exp(-x))


def _log_sigmoid(x):
    return jnp.minimum(x, 0.0) - jnp.log1p(jnp.exp(-jnp.abs(x)))


def _rms(x, g):
    return x * lax.rsqrt(jnp.mean(x * x, axis=-1, keepdims=True) + EPS) * g


def _dot(a, b, precision=None):
    return jnp.dot(a, b, preferred_element_type=F32, precision=precision)


def _dot_nt(a, b, precision=None):
    return lax.dot_general(a, b, (((1,), (1,)), ((), ())), preferred_element_type=F32, precision=precision)


def _iota(shape, dim):
    return lax.broadcasted_iota(jnp.int32, shape, dim)


def _ada_kernel(c_ref, w_ref, b_ref, o_ref):
    c = c_ref[...]
    s = (c * _sigmoid(c)).astype(BF16)
    o_ref[0] = _dot(s, w_ref[0].astype(BF16)) + b_ref[0]


def _ada_all(c_all, w_ada, b_ada):
    n, d, n3 = w_ada.shape
    r = c_all.shape[0]
    tn = _tile(n3, 1024)
    return pl.pallas_call(
        _ada_kernel,
        grid=(n, n3 // tn),
        in_specs=[
            pl.BlockSpec((r, d), lambda i, j: (0, 0)),
            pl.BlockSpec((1, d, tn), lambda i, j: (i, 0, j)),
            pl.BlockSpec((1, 1, tn), lambda i, j: (i, 0, j)),
        ],
        out_specs=pl.BlockSpec((1, r, tn), lambda i, j: (i, 0, j)),
        out_shape=jax.ShapeDtypeStruct((n, r, n3), F32),
        compiler_params=_params(2, 40),
    )(c_all, w_ada, b_ada)


def _norm_mm_kernel(*refs, has_mod, n_out):
    if has_mod:
        x_ref, g_ref, sh_ref, sc_ref, w_ref = refs[:5]
        rest = refs[5:]
    else:
        x_ref, g_ref, w_ref = refs[:3]
        rest = refs[3:]
    o_refs, h_ref = rest[:n_out], rest[n_out]

    @pl.when(pl.program_id(1) == 0)
    def _():
        h = _rms(x_ref[...], g_ref[...])
        if has_mod:
            h = h * (1.0 + sc_ref[0]) + sh_ref[0]
        h_ref[...] = h.astype(BF16)

    y = _dot(h_ref[...], w_ref[...])
    for o_ref in o_refs:
        o_ref[...] = y.astype(o_ref.dtype)


def _norm_mm(x, g, mod, w, layer, *, rows_per_mod, n_cols=None, col_off=0, out_dtypes=(F32,),
             tm_pref=1024, tn_pref=1024):
    m, d = x.shape
    n_cols = w.shape[2] if n_cols is None else n_cols
    tm = _tile(m, tm_pref)
    tn = _tile(n_cols, tn_pref)
    assert col_off % tn == 0
    joff = col_off // tn
    in_specs = [pl.BlockSpec((tm, d), lambda i, j: (i, 0)), pl.BlockSpec((1, d), lambda i, j: (0, 0))]
    args = [x, g]
    if mod is not None:
        r = mod.shape[1]
        assert rows_per_mod % tm == 0 and r in (1, tm)
        bidx = lambda i: (i * tm) // rows_per_mod
        in_specs += [pl.BlockSpec((1, r, d), lambda i, j: (bidx(i), 0, 0)),
                     pl.BlockSpec((1, r, d), lambda i, j: (bidx(i), 0, 1))]
        args += [mod, mod]
    in_specs.append(pl.BlockSpec((None, d, tn), lambda i, j: (layer, 0, j + joff)))
    args.append(w)
    outs = pl.pallas_call(
        functools.partial(_norm_mm_kernel, has_mod=mod is not None, n_out=len(out_dtypes)),
        grid=(m // tm, n_cols // tn),
        in_specs=in_specs,
        out_specs=[pl.BlockSpec((tm, tn), lambda i, j: (i, j)) for _ in out_dtypes],
        out_shape=[jax.ShapeDtypeStruct((m, n_cols), dt) for dt in out_dtypes],
        scratch_shapes=[pltpu.VMEM((tm, d), BF16)],
        compiler_params=_params(2, 56),
    )(*args)
    return outs if len(outs) > 1 else outs[0]


def _mm_post_kernel(a_ref, w_ref, x_ref, gate_ref, g_ref, o_ref):
    y = _dot(a_ref[...].astype(BF16), w_ref[...])
    o_ref[...] = x_ref[...] + gate_ref[0] * _rms(y, g_ref[...])


def _mm_post(a, w, layer, x, mod, g, *, rows_per_mod, tm_pref=512):
    m, kdim = a.shape
    d = w.shape[2]
    tm = _tile(m, tm_pref)
    r = mod.shape[1]
    assert rows_per_mod % tm == 0 and r in (1, tm)
    bidx = lambda i: (i * tm) // rows_per_mod
    return pl.pallas_call(
        _mm_post_kernel,
        grid=(m // tm,),
        in_specs=[
            pl.BlockSpec((tm, kdim), lambda i: (i, 0)),
            pl.BlockSpec((None, kdim, d), lambda i: (layer, 0, 0)),
            pl.BlockSpec((tm, d), lambda i: (i, 0)),
            pl.BlockSpec((1, r, d), lambda i: (bidx(i), 0, 2)),
            pl.BlockSpec((1, d), lambda i: (0, 0)),
        ],
        out_specs=pl.BlockSpec((tm, d), lambda i: (i, 0)),
        out_shape=jax.ShapeDtypeStruct((m, d), F32),
        compiler_params=_params(1, 56),
    )(a, w, x, mod, g)


def _mlp_kernel(x_ref, gpre_ref, sh_ref, sc_ref, gate_ref, w1_ref, w2_ref, gpost_ref, o_ref, h_ref, acc_ref):
    f = pl.program_id(1)

    @pl.when(f == 0)
    def _():
        h = _rms(x_ref[...], gpre_ref[...]) * (1.0 + sc_ref[0]) + sh_ref[0]
        h_ref[...] = h.astype(BF16)
        acc_ref[...] = jnp.zeros_like(acc_ref)

    u = jnp.maximum(_dot(h_ref[...], w1_ref[...]), 0.0)
    acc_ref[...] += _dot((u * u).astype(BF16), w2_ref[...])

    @pl.when(f == pl.num_programs(1) - 1)
    def _():
        o_ref[...] = x_ref[...] + gate_ref[0] * _rms(acc_ref[...], gpost_ref[...])


def _mlp(x, gpre, mod, w1, w2, layer, gpost, *, rows_per_mod, tm_pref=512, tf_pref=512):
    m, d = x.shape
    ff = w1.shape[2]
    tm = _tile(m, tm_pref)
    tf = _tile(ff, tf_pref)
    r = mod.shape[1]
    assert rows_per_mod % tm == 0 and r in (1, tm)
    bidx = lambda i: (i * tm) // rows_per_mod
    return pl.pallas_call(
        _mlp_kernel,
        grid=(m // tm, ff // tf),
        in_specs=[
            pl.BlockSpec((tm, d), lambda i, f: (i, 0)),
            pl.BlockSpec((1, d), lambda i, f: (0, 0)),
            pl.BlockSpec((1, r, d), lambda i, f: (bidx(i), 0, 0)),
            pl.BlockSpec((1, r, d), lambda i, f: (bidx(i), 0, 1)),
            pl.BlockSpec((1, r, d), lambda i, f: (bidx(i), 0, 2)),
            pl.BlockSpec((None, d, tf), lambda i, f: (layer, 0, f)),
            pl.BlockSpec((None, tf, d), lambda i, f: (layer, f, 0)),
            pl.BlockSpec((1, d), lambda i, f: (0, 0)),
        ],
        out_specs=pl.BlockSpec((tm, d), lambda i, f: (i, 0)),
        out_shape=jax.ShapeDtypeStruct((m, d), F32),
        scratch_shapes=[pltpu.VMEM((tm, d), BF16), pltpu.VMEM((tm, d), F32)],
        compiler_params=_params(2, 56),
    )(x, gpre, mod, mod, mod, w1, w2, gpost)


def _hgrn_kernel(q_ref, f_ref, i_ref, g_ref, lbl_ref, og_ref, s0_ref, o_ref, s_ref, st_ref,
                 *, layer, rows, n_inner, hpb):
    c = HGRN_CHUNK
    sub = HGRN_SUB
    dk = q_ref.shape[-1] // hpb
    step = pl.program_id(2)

    @pl.when(step == 0)
    def _():
        for h in range(hpb):
            st_ref[h] = s0_ref[0, h].T

    z = lbl_ref[...]
    e = jnp.exp(z - jnp.max(z, axis=0, keepdims=True))
    p = e / jnp.sum(e, axis=0, keepdims=True)
    cs = p[0:1]
    for r in range(1, layer + 1):
        cs = cs + p[r:r + 1]
    lb_all = cs - p[0:1]
    og_all = og_ref[...]

    tril = (_iota((c, c), 0) >= _iota((c, c), 1)).astype(BF16)
    lane = _iota((sub, c), 1)
    sub_row = _iota((sub, c), 0)
    scale = dk ** -0.5

    def pad(t, fill=0.0):
        if rows == c:
            return t
        return jnp.concatenate([t, jnp.full((c - rows, t.shape[1]), fill, t.dtype)], axis=0)

    def cumsum_rows(x):
        hi = x.astype(BF16)
        r1 = x - hi.astype(F32)
        mid = r1.astype(BF16)
        lo = (r1 - mid.astype(F32)).astype(BF16)
        return _dot(tril, hi) + _dot(tril, mid) + _dot(tril, lo)

    def head_chunk(h, sl):
        hs = slice(h * dk, (h + 1) * dk)
        lb = lb_all[:, hs]
        lb_floor = jnp.maximum(lb, TINY)
        one_m_lb = 1.0 - lb
        log2_one_m_lb = jnp.log(one_m_lb) * LOG2E
        qz = q_ref[0, sl, hs]
        fz = f_ref[0, sl, hs]
        q = pad(qz * _sigmoid(qz))
        e = jnp.exp(-jnp.abs(fz))
        t = 1.0 + e
        r = 1.0 / t
        sig = jnp.where(fz >= 0.0, r, e * r)
        logf = pad(jnp.log(lb_floor + one_m_lb * sig))
        lk = pad(log2_one_m_lb + (jnp.minimum(-fz, 0.0) - jnp.log(t)) * LOG2E, -jnp.inf)
        v = pad(i_ref[0, sl, hs])
        b2 = cumsum_rows(logf) * LOG2E
        c2 = b2 - lk
        st = st_ref[h]
        inter = _dot_nt((q * jnp.exp2(b2)).astype(BF16), st.astype(BF16))
        blocks = []
        for i in range(c // sub):
            lo = i * sub
            qi, bi, ci = q[lo:lo + sub], b2[lo:lo + sub], c2[lo:lo + sub]
            if i > 0:
                ref_b = bi[0:1]
                qs = (qi * jnp.exp2(bi - ref_b)).astype(BF16)
                kt = jnp.exp2(ref_b - c2[:lo]).astype(BF16)
                kt = jnp.concatenate([kt, jnp.zeros((c - lo, dk), BF16)], axis=0)
                s_i = _dot_nt(qs, kt)
            else:
                s_i = jnp.zeros((sub, c), F32)
            for s in range(sub):
                w = jnp.sum(qi * jnp.exp2(bi - ci[s:s + 1]), axis=-1, keepdims=True)
                s_i = jnp.where(lane == lo + s, w, s_i)
            blocks.append(jnp.where(lane <= lo + sub_row, s_i, 0.0))
        scores = jnp.concatenate(blocks, axis=0).astype(BF16)
        o = (inter + _dot(scores, v.astype(BF16))) * scale
        blast = b2[c - 1:c]
        kd = jnp.exp2(blast - c2).astype(BF16)
        st_ref[h] = st * jnp.exp2(blast) + _dot(v.T.astype(BF16), kd)
        gz = g_ref[0, sl, hs]
        o_ref[0, sl, hs] = _rms(o[:rows], og_all[:, hs]) * (gz * _sigmoid(gz))

    def chunk(j, carry):
        if n_inner == 1:
            sl = pl.ds(0, rows)
        else:
            sl = pl.ds(pl.multiple_of(j * c, c), c)
        for h in range(hpb):
            head_chunk(h, sl)
        return carry

    lax.fori_loop(0, n_inner, chunk, 0)

    @pl.when(step == pl.num_programs(2) - 1)
    def _():
        for h in range(hpb):
            s_ref[0, h] = st_ref[h].T


def _hgrn(proj, lb_logits, out_g, s0, *, layer, heads):
    b, l, _ = proj.shape
    dk = s0.shape[2]
    dv = s0.shape[3]
    assert dk == LANES and dv == LANES
    c = HGRN_CHUNK
    if l >= c:
        assert l % c == 0
        rows = c
        n_inner = _tile(l // c, 4)
        blk = rows * n_inner
    else:
        assert l % 8 == 0
        rows, n_inner, blk = l, 1, l
    nl = lb_logits.shape[0]
    hpb = _tile(heads, HGRN_HEADS_PER_STEP)
    ng = heads // hpb
    col = lambda part: pl.BlockSpec((1, blk, hpb * dk), lambda bi, h, s: (bi, s, part * ng + h))
    o, s_new = pl.pallas_call(
        functools.partial(_hgrn_kernel, layer=layer, rows=rows, n_inner=n_inner, hpb=hpb),
        grid=(b, ng, l // blk),
        in_specs=[
            col(0), col(1), col(2), col(3),
            pl.BlockSpec((nl, hpb * dk), lambda bi, h, s: (0, h)),
            pl.BlockSpec((1, hpb * dv), lambda bi, h, s: (0, h)),
            pl.BlockSpec((1, hpb, dk, dv), lambda bi, h, s: (bi, h, 0, 0)),
        ],
        out_specs=[
            pl.BlockSpec((1, blk, hpb * dv), lambda bi, h, s: (bi, s, h)),
            pl.BlockSpec((1, hpb, dk, dv), lambda bi, h, s: (bi, h, 0, 0)),
        ],
        out_shape=[jax.ShapeDtypeStruct((b, l, heads * dv), F32),
                   jax.ShapeDtypeStruct((b, heads, dk, dv), F32)],
        scratch_shapes=[pltpu.VMEM((hpb, dv, dk), F32)],
        compiler_params=_params(3, 32),
    )(proj, proj, proj, proj, lb_logits, out_g, s0)
    return o, s_new


def _logf_kernel(x_ref, g_ref, wft_ref, bft_ref, c0_ref, lf_ref, ft_ref, car_ref):
    tm = x_ref.shape[0]

    @pl.when(pl.program_id(1) == 0)
    def _():
        car_ref[...] = c0_ref[0]

    h = _rms(x_ref[...], g_ref[...])
    zt = _dot_nt(wft_ref[...], h, HIGHEST) + bft_ref[...]
    lft = _log_sigmoid(zt)
    lf_ref[0] = lft
    triu = (_iota((tm, tm), 0) <= _iota((tm, tm), 1)).astype(F32)
    ft = _dot(lft, triu, HIGHEST) + car_ref[...]
    ft_ref[0] = ft
    car_ref[...] = ft[:, tm - 1:tm]


def _logf(x, g, wft, bft, c0t, *, batch):
    m, d = x.shape
    l = m // batch
    hh = wft.shape[0]
    tm = _tile(l, 512)
    nt = l // tm
    out = jax.ShapeDtypeStruct((batch, hh, l), F32)
    return pl.pallas_call(
        _logf_kernel,
        grid=(batch, nt),
        in_specs=[
            pl.BlockSpec((tm, d), lambda b, i: (b * nt + i, 0)),
            pl.BlockSpec((1, d), lambda b, i: (0, 0)),
            pl.BlockSpec((hh, d), lambda b, i: (0, 0)),
            pl.BlockSpec((hh, 1), lambda b, i: (0, 0)),
            pl.BlockSpec((1, hh, 1), lambda b, i: (b, 0, 0)),
        ],
        out_specs=[pl.BlockSpec((1, hh, tm), lambda b, i: (b, 0, i)),
                   pl.BlockSpec((1, hh, tm), lambda b, i: (b, 0, i))],
        out_shape=[out, out],
        scratch_shapes=[pltpu.VMEM((hh, 1), F32)],
        compiler_params=_params(2, 32),
    )(x, g, wft, bft, c0t)


def _fox_prompt_kernel(q_ref, k_ref, v_ref, fq_ref, fk_ref, o_ref, m_ref, l_ref, acc_ref, *, tq, hpb):
    i = pl.program_id(2)
    hd = q_ref.shape[-1] // hpb
    n_lt = tq // LANES
    m_ref[...] = jnp.full_like(m_ref, MASK_VALUE)
    l_ref[...] = jnp.zeros_like(l_ref)
    acc_ref[...] = jnp.zeros_like(acc_ref)
    qs, fqs = [], []
    for h in range(hpb):
        qs.append((q_ref[:, h * hd:(h + 1) * hd].astype(F32) * (hd ** -0.5 * LOG2E)).astype(BF16))
        fqs.append(jnp.broadcast_to(fq_ref[0, h] * LOG2E, (tq, LANES)))

    def tile(j, masked):
        ks = pl.ds(pl.multiple_of(j * tq, tq), tq)
        for h in range(hpb):
            cs = slice(h * hd, (h + 1) * hd)
            s = _dot_nt(qs[h], k_ref[ks, cs])
            fk = fk_ref[0, h, :, ks] * LOG2E
            us = []
            for c in range(n_lt):
                ls = slice(c * LANES, (c + 1) * LANES)
                u = s[:, ls] - fk[:, ls]
                if masked:
                    u = jnp.where(_iota((tq, LANES), 0) >= c * LANES + _iota((tq, LANES), 1), u, MASK_VALUE)
                us.append(u)
            row_max = jnp.max(functools.reduce(jnp.maximum, us), axis=-1, keepdims=True)
            m_old = m_ref[h]
            m_new = jnp.maximum(m_old, jnp.broadcast_to(row_max, (tq, LANES)) + fqs[h])
            shift = fqs[h] - m_new
            ps = [jnp.exp2(u + shift) for u in us]
            alpha = jnp.exp2(m_old - m_new)
            m_ref[h] = m_new
            l_ref[h] = alpha * l_ref[h] + functools.reduce(jnp.add, ps)
            p_all = jnp.concatenate([p.astype(BF16) for p in ps], axis=1)
            acc_ref[h] = alpha * acc_ref[h] + _dot(p_all, v_ref[ks, cs])

    def body(j, carry):
        tile(j, False)
        return carry

    lax.fori_loop(0, i, body, 0)
    tile(i, True)
    for h in range(hpb):
        o_ref[:, h * hd:(h + 1) * hd] = acc_ref[h] / jnp.sum(l_ref[h], axis=-1, keepdims=True)


def _fox_prompt(q, k, v, ft, *, batch, heads):
    m, d = q.shape
    l = m // batch
    hd = d // heads
    assert hd == LANES
    hpb = ATTN_HEADS_PER_STEP
    tq = _tile(l, 512)
    nq = l // tq
    fcol = ft[..., None]
    frow = ft[:, :, None, :]
    return pl.pallas_call(
        functools.partial(_fox_prompt_kernel, tq=tq, hpb=hpb),
        grid=(batch, heads // hpb, nq),
        in_specs=[
            pl.BlockSpec((tq, hpb * hd), lambda b, h, i: (b * nq + i, h)),
            pl.BlockSpec((l, hpb * hd), lambda b, h, i: (b, h)),
            pl.BlockSpec((l, hpb * hd), lambda b, h, i: (b, h)),
            pl.BlockSpec((1, hpb, tq, 1), lambda b, h, i: (b, h, i, 0)),
            pl.BlockSpec((1, hpb, 1, l), lambda b, h, i: (b, h, 0, 0)),
        ],
        out_specs=pl.BlockSpec((tq, hpb * hd), lambda b, h, i: (b * nq + i, h)),
        out_shape=jax.ShapeDtypeStruct((m, d), F32),
        scratch_shapes=[pltpu.VMEM((hpb, tq, LANES), F32), pltpu.VMEM((hpb, tq, LANES), F32),
                        pltpu.VMEM((hpb, tq, hd), F32)],
        compiler_params=_params(3, 40),
    )(q, k, v, fcol, frow)


def _fpast_kernel(pt_ref, lf_ref, o_ref, car_ref, *, n_pg):
    b, g = pl.program_id(0), pl.program_id(1)
    n_grp, prow = lf_ref.shape[1], lf_ref.shape[2]
    rows = n_pg * prow
    keys_per_row = LANES // HEAD_GROUP

    @pl.when(g == 0)
    def _():
        car_ref[...] = jnp.zeros_like(car_ref)

    li, lj = _iota((LANES, LANES), 0), _iota((LANES, LANES), 1)
    same_head = (li % HEAD_GROUP) == (lj % HEAD_GROUP)
    prefix = (same_head & (li // HEAD_GROUP <= lj // HEAD_GROUP)).astype(F32)
    last = (same_head & (li // HEAD_GROUP == keys_per_row - 1)).astype(F32)
    strict = (_iota((rows, rows), 0) > _iota((rows, rows), 1)).astype(F32)
    for a in range(n_grp):
        x = jnp.concatenate([lf_ref[pt_ref[b, g * n_pg + o], a] for o in range(n_pg)], axis=0)
        y = _dot(x, prefix, HIGHEST)
        tot = _dot(y, last, HIGHEST)
        o_ref[0, a] = y + _dot(strict, tot, HIGHEST) + car_ref[a]
        car_ref[a] += jnp.sum(tot, axis=0, keepdims=True)


def _fpast(page_table, cache_logf_grp):
    b, n_pages = page_table.shape
    n_phys, n_grp, prow, _ = cache_logf_grp.shape
    n_pg = _tile(n_pages, FPAST_PAGES_PER_STEP)
    return pl.pallas_call(
        functools.partial(_fpast_kernel, n_pg=n_pg),
        grid_spec=pltpu.PrefetchScalarGridSpec(
            num_scalar_prefetch=1,
            grid=(b, n_pages // n_pg),
            in_specs=[pl.BlockSpec((n_phys, n_grp, prow, LANES), lambda bi, g, pt: (0, 0, 0, 0))],
            out_specs=pl.BlockSpec((1, n_grp, n_pg * prow, LANES), lambda bi, g, pt: (bi, 0, g, 0)),
            scratch_shapes=[pltpu.VMEM((n_grp, 1, LANES), F32)],
        ),
        out_shape=jax.ShapeDtypeStruct((b, n_grp, n_pages * prow, LANES), F32),
        compiler_params=_params(2, 40),
    )(page_table, cache_logf_grp)


def _fox_sample_kernel(pt_ref, q_ref, *refs, heads, n_pg):
    kc_refs, vc_refs = refs[:n_pg], refs[n_pg:2 * n_pg]
    fp_ref, kn_ref, vn_ref, fq_ref, fn_ref, o_ref, qa_ref, m_ref, l_ref, acc_ref = refs[2 * n_pg:]
    step = pl.program_id(1)
    t, d = q_ref.shape
    hd = d // heads
    grp = HEAD_GROUP
    n_grp = heads // grp
    r = grp * t
    page = kc_refs[0].shape[1]
    prow = page * grp // LANES

    @pl.when(step == 0)
    def _():
        q = q_ref[...] * (hd ** -0.5 * LOG2E)
        for a in range(n_grp):
            qa_ref[a] = jnp.concatenate([q[:, (a * grp + j) * hd:(a * grp + j + 1) * hd] for j in range(grp)], axis=0)
        m_ref[...] = jnp.full_like(m_ref, MASK_VALUE)
        l_ref[...] = jnp.zeros_like(l_ref)
        acc_ref[...] = jnp.zeros_like(acc_ref)

    head_match = (_iota((r, LANES), 0) // t) == (_iota((r, LANES), 1) % grp)

    def update(a, k_list, v_list, fk_list, mask):
        fq = jnp.broadcast_to(fq_ref[0, a] * LOG2E, (r, LANES))
        qa = qa_ref[a]
        us = []
        for k_flat, fk in zip(k_list, fk_list):
            s = _dot_nt(qa, k_flat)
            for c in range(k_flat.shape[0] // LANES):
                u = s[:, c * LANES:(c + 1) * LANES] - fk[c:c + 1] * LOG2E
                us.append(jnp.where(mask, u, MASK_VALUE))
        row_max = jnp.max(functools.reduce(jnp.maximum, us), axis=-1, keepdims=True)
        m_old = m_ref[a]
        m_new = jnp.maximum(m_old, jnp.broadcast_to(row_max, (r, LANES)) + fq)
        shift = fq - m_new
        ps = [jnp.exp2(u + shift) for u in us]
        alpha = jnp.exp2(m_old - m_new)
        m_ref[a] = m_new
        l_ref[a] = alpha * l_ref[a] + functools.reduce(jnp.add, ps)
        pv, at = None, 0
        for v_flat in v_list:
            n_t = v_flat.shape[0] // LANES
            p_pg = jnp.concatenate(ps[at:at + n_t], axis=1) if n_t > 1 else ps[at]
            at += n_t
            part = _dot(p_pg, v_flat)
            pv = part if pv is None else pv + part
        acc_ref[a] = alpha * acc_ref[a] + pv

    def group_rows(ref, a):
        blk = ref[0, :, a * grp:(a + 1) * grp, :]
        return blk.reshape(blk.shape[0] * grp, hd)

    for a in range(n_grp):
        update(a, [group_rows(kc, a) for kc in kc_refs], [group_rows(vc, a) for vc in vc_refs],
               [fp_ref[0, a, o * prow:(o + 1) * prow] for o in range(n_pg)], head_match)

    @pl.when(step == pl.num_programs(1) - 1)
    def _():
        lane = _iota((r, LANES), 1)
        new_mask = head_match & (lane // grp <= _iota((r, LANES), 0) % t) & (lane < t * grp)
        zeros = jnp.zeros((LANES - t * grp, hd), F32)
        for a in range(n_grp):
            kn = jnp.concatenate([group_rows(kn_ref, a), zeros], axis=0)
            vn = jnp.concatenate([group_rows(vn_ref, a), zeros], axis=0)
            update(a, [kn], [vn], [fn_ref[0, a]], new_mask)
            res = acc_ref[a] / jnp.sum(l_ref[a], axis=-1, keepdims=True)
            for j in range(grp):
                h = a * grp + j
                o_ref[:, h * hd:(h + 1) * hd] = res[j * t:(j + 1) * t]


def _fox_sample(q, cache_k, cache_v, page_table, fpast, k_new, v_new, fq_col, fnew_flat):
    b, n_pages = page_table.shape
    m, d = q.shape
    t = m // b
    _, page, heads, hd = cache_k.shape
    grp = HEAD_GROUP
    n_grp = heads // grp
    assert hd == LANES and heads % grp == 0 and t * grp <= LANES and (page * grp) % LANES == 0
    r = grp * t
    n_pg = _tile(n_pages, PAGES_PER_STEP)
    prow = page * grp // LANES
    page_spec = lambda o: pl.BlockSpec((1, page, heads, hd), lambda bi, s, pt: (pt[bi, s * n_pg + o], 0, 0, 0))
    new_spec = pl.BlockSpec((1, t, heads, hd), lambda bi, s, pt: (bi, 0, 0, 0))
    in_specs = ([pl.BlockSpec((t, d), lambda bi, s, pt: (bi, 0))]
                + [page_spec(o) for o in range(n_pg)] * 2
                + [pl.BlockSpec((1, n_grp, n_pg * prow, LANES), lambda bi, s, pt: (bi, 0, s, 0)),
                   new_spec, new_spec,
                   pl.BlockSpec((1, n_grp, r, 1), lambda bi, s, pt: (bi, 0, 0, 0)),
                   pl.BlockSpec((1, n_grp, 1, LANES), lambda bi, s, pt: (bi, 0, 0, 0))])
    return pl.pallas_call(
        functools.partial(_fox_sample_kernel, heads=heads, n_pg=n_pg),
        grid_spec=pltpu.PrefetchScalarGridSpec(
            num_scalar_prefetch=1,
            grid=(b, n_pages // n_pg),
            in_specs=in_specs,
            out_specs=pl.BlockSpec((t, d), lambda bi, s, pt: (bi, 0)),
            scratch_shapes=[pltpu.VMEM((n_grp, r, hd), F32), pltpu.VMEM((n_grp, r, LANES), F32),
                            pltpu.VMEM((n_grp, r, LANES), F32), pltpu.VMEM((n_grp, r, hd), F32)],
        ),
        out_shape=jax.ShapeDtypeStruct((m, d), F32),
        compiler_params=_params(2, 48),
    )(page_table, q, *([cache_k] * n_pg), *([cache_v] * n_pg), fpast, k_new, v_new, fq_col, fnew_flat)


def _trunk(x3, mods, s0, attend_ctx, wts):
    (norm_pre, norm_post, a_w_in, a_w_out, a_lb_logits, a_out_norm, kv_norm, w_kv, wft, b_forget,
     b_w_q, b_w_o, mlp_w_in, mlp_w_out) = wts
    b, l, d = x3.shape
    m = b * l
    depth = norm_pre.shape[0]
    n_a = a_w_in.shape[0]
    a_heads = s0.shape[2]
    b_heads = b_forget.shape[0]
    x = x3.reshape(m, d)
    if m <= 512:
        rows_per_mod = m
        expand = lambda t: jnp.repeat(t, l, axis=0)[None]
    else:
        rows_per_mod = l
        expand = lambda t: t[:, None, :]
    row = lambda t: t.reshape(1, -1)
    states = []
    shared = None
    for layer in range(depth):
        mod = expand(mods[layer, 0])
        if layer < n_a:
            proj = _norm_mm(x, row(norm_pre[layer, 0]), mod, a_w_in, layer, rows_per_mod=rows_per_mod)
            o, s_new = _hgrn(proj.reshape(b, l, -1), a_lb_logits, row(a_out_norm[layer]), s0[layer],
                             layer=layer, heads=a_heads)
            states.append(s_new)
            y_in, w_o, w_idx = o.reshape(m, d), a_w_out, layer
        else:
            j = layer - n_a
            if shared is None:
                kvg = row(kv_norm)
                k, k16 = _norm_mm(x, kvg, None, w_kv, 0, rows_per_mod=rows_per_mod, n_cols=d, col_off=0,
                                  out_dtypes=(F32, BF16))
                v, v16 = _norm_mm(x, kvg, None, w_kv, 0, rows_per_mod=rows_per_mod, n_cols=d, col_off=d,
                                  out_dtypes=(F32, BF16))
                shared = attend_ctx["make"](x, kvg, wft, b_forget.reshape(b_heads, 1), k, v, k16, v16)
            q = _norm_mm(x, row(norm_pre[layer, 0]), mod, b_w_q, j, rows_per_mod=rows_per_mod,
                         out_dtypes=(attend_ctx["q_dtype"],))
            y_in, w_o, w_idx = attend_ctx["attend"](q, shared), b_w_o, j
        x = _mm_post(y_in, w_o, w_idx, x, mod, row(norm_post[layer, 0]), rows_per_mod=rows_per_mod)
        mod = expand(mods[layer, 1])
        x = _mlp(x, row(norm_pre[layer, 1]), mod, mlp_w_in, mlp_w_out, layer, row(norm_post[layer, 1]),
                 rows_per_mod=rows_per_mod)
    k, v, lft = shared["rows"]
    kv_shape = (b, l, b_heads, d // b_heads)
    return (x.reshape(b, l, d), jnp.stack(states), k.reshape(kv_shape), v.reshape(kv_shape),
            jnp.swapaxes(lft, 1, 2))


def kernel(x_prompt, x_sample, c_prompt, c_sample, state_hgrn, cache_k, cache_v, cache_logf, page_table,
           norm_pre, norm_post, w_ada, b_ada, a_w_in, a_w_out, a_lb_logits, a_out_norm,
           kv_norm, w_kv, b_forget, b_w_q, b_w_o, mlp_w_in, mlp_w_out):
    bp, _, d = x_prompt.shape
    bs, ts, _ = x_sample.shape
    depth = norm_pre.shape[0]
    n_phys, page, heads, hd = cache_k.shape
    wts = (norm_pre, norm_post, a_w_in.astype(BF16), a_w_out.astype(BF16), a_lb_logits, a_out_norm, kv_norm,
           w_kv.astype(BF16)[None], w_kv[:, 2 * d:].T, b_forget,
           b_w_q.astype(BF16), b_w_o.astype(BF16), mlp_w_in.astype(BF16), mlp_w_out.astype(BF16))

    n_c = bp + bs
    c_rows = -(-n_c // 8) * 8
    c_all = jnp.concatenate([c_prompt, c_sample, jnp.zeros((c_rows - n_c, d), F32)], axis=0)
    mods = _ada_all(c_all, w_ada.reshape(depth * 2, d, 3 * d), b_ada.reshape(depth * 2, 1, 3 * d))
    mods = mods.reshape(depth, 2, c_rows, 3 * d)
    mods_p, mods_s = mods[:, :, :bp], mods[:, :, bp:n_c]

    def make_prompt(x, kvg, wft, bft, k, v, k16, v16):
        lft, ft = _logf(x, kvg, wft, bft, jnp.zeros((bp, heads, 1), F32), batch=bp)
        return {"rows": (k, v, lft), "k": k16, "v": v16, "ft": ft}

    def attend_prompt(q, sh):
        return _fox_prompt(q, sh["k"], sh["v"], sh["ft"], batch=bp, heads=heads)

    s0_prompt = jnp.zeros((a_w_in.shape[0], bp) + state_hgrn.shape[2:], F32)
    y_p, st_p, k_p, v_p, lf_p = _trunk(
        x_prompt, mods_p, s0_prompt, {"make": make_prompt, "attend": attend_prompt, "q_dtype": BF16}, wts)

    grp = HEAD_GROUP
    n_grp = heads // grp
    prow = page * grp // LANES
    lf_grp = cache_logf.astype(F32).reshape(n_phys, page, n_grp, grp).transpose(0, 2, 1, 3)
    fpast = _fpast(page_table, lf_grp.reshape(n_phys, n_grp, prow, LANES))
    f_total = fpast[:, :, -1, LANES - grp:].reshape(bs, heads, 1)

    def make_sample(x, kvg, wft, bft, k, v, k16, v16):
        lft, fnew_t = _logf(x, kvg, wft, bft, f_total, batch=bs)
        fq_col = fnew_t.reshape(bs, n_grp, grp * ts, 1)
        fnew_flat = fnew_t.reshape(bs, n_grp, grp, ts).transpose(0, 1, 3, 2).reshape(bs, n_grp, 1, ts * grp)
        fnew_flat = jnp.pad(fnew_flat, ((0, 0), (0, 0), (0, 0), (0, LANES - ts * grp)))
        shape4 = (bs, ts, heads, hd)
        return {"rows": (k, v, lft), "k": k.reshape(shape4), "v": v.reshape(shape4), "fq": fq_col, "fn": fnew_flat}

    def attend_sample(q, sh):
        return _fox_sample(q, cache_k, cache_v, page_table, fpast, sh["k"], sh["v"], sh["fq"], sh["fn"])

    y_s, st_s, k_s, v_s, lf_s = _trunk(
        x_sample, mods_s, state_hgrn, {"make": make_sample, "attend": attend_sample, "q_dtype": F32}, wts)

    return (y_p, y_s, st_p, st_s, k_p, v_p, lf_p, k_s, v_s, lf_s)
```

```python
import functools
import math

import jax
import jax.numpy as jnp
from jax import lax
from jax.experimental import pallas as pl
from jax.experimental.pallas import tpu as pltpu

EPS = 1e-6
MASK_VALUE = -1e30
TINY = 1e-30
LOG2E = math.log2(math.e)
F32 = jnp.float32
BF16 = jnp.bfloat16
HIGHEST = lax.Precision.HIGHEST

LANES = 128
HGRN_CHUNK = 128
HGRN_SUB = 8
SUBLANES = 8
HEAD_GROUP = SUBLANES
PAGES_PER_STEP = 4
FPAST_PAGES_PER_STEP = 8
HGRN_HEADS_PER_STEP = 4
ATTN_HEADS_PER_STEP = 4
MIB = 2 ** 20


def _params(n_axes, vmem_mib):
    return pltpu.CompilerParams(
        dimension_semantics=("arbitrary",) * n_axes, vmem_limit_bytes=vmem_mib * MIB)


def _tile(n, pref):
    t = min(n, pref)
    while n % t:
        t -= 1
    return t


def _sigmoid(x):
    return 1.0 / (1.0 + jnp.exp(-x))


def _log_sigmoid(x):
    return jnp.minimum(x, 0.0) - jnp.log1p(jnp.exp(-jnp.abs(x)))


def _rms(x, g):
    return x * lax.rsqrt(jnp.mean(x * x, axis=-1, keepdims=True) + EPS) * g


def _dot(a, b, precision=None):
    return jnp.dot(a, b, preferred_element_type=F32, precision=precision)


def _dot_nt(a, b, precision=None):
    return lax.dot_general(a, b, (((1,), (1,)), ((), ())), preferred_element_type=F32, precision=precision)


def _iota(shape, dim):
    return lax.broadcasted_iota(jnp.int32, shape, dim)


def _ada_kernel(c_ref, w_ref, b_ref, o_ref):
    c = c_ref[...]
    s = (c * _sigmoid(c)).astype(BF16)
    o_ref[0] = _dot(s, w_ref[0].astype(BF16)) + b_ref[0]


def _ada_all(c_all, w_ada, b_ada):
    n, d, n3 = w_ada.shape
    r = c_all.shape[0]
    tn = _tile(n3, 1024)
    return pl.pallas_call(
        _ada_kernel,
        grid=(n, n3 // tn),
        in_specs=[
            pl.BlockSpec((r, d), lambda i, j: (0, 0)),
            pl.BlockSpec((1, d, tn), lambda i, j: (i, 0, j)),
            pl.BlockSpec((1, 1, tn), lambda i, j: (i, 0, j)),
        ],
        out_specs=pl.BlockSpec((1, r, tn), lambda i, j: (i, 0, j)),
        out_shape=jax.ShapeDtypeStruct((n, r, n3), F32),
        compiler_params=_params(2, 40),
    )(c_all, w_ada, b_ada)


def _norm_mm_kernel(*refs, has_mod, n_out):
    if has_mod:
        x_ref, g_ref, sh_ref, sc_ref, w_ref = refs[:5]
        rest = refs[5:]
    else:
        x_ref, g_ref, w_ref = refs[:3]
        rest = refs[3:]
    o_refs, h_ref = rest[:n_out], rest[n_out]

    @pl.when(pl.program_id(1) == 0)
    def _():
        h = _rms(x_ref[...], g_ref[...])
        if has_mod:
            h = h * (1.0 + sc_ref[0]) + sh_ref[0]
        h_ref[...] = h.astype(BF16)

    y = _dot(h_ref[...], w_ref[...])
    for o_ref in o_refs:
        o_ref[...] = y.astype(o_ref.dtype)


def _norm_mm(x, g, mod, w, layer, *, rows_per_mod, n_cols=None, col_off=0, out_dtypes=(F32,),
             tm_pref=1024, tn_pref=1024):
    m, d = x.shape
    n_cols = w.shape[2] if n_cols is None else n_cols
    tm = _tile(m, tm_pref)
    tn = _tile(n_cols, tn_pref)
    assert col_off % tn == 0
    joff = col_off // tn
    in_specs = [pl.BlockSpec((tm, d), lambda i, j: (i, 0)), pl.BlockSpec((1, d), lambda i, j: (0, 0))]
    args = [x, g]
    if mod is not None:
        r = mod.shape[1]
        assert rows_per_mod % tm == 0 and r in (1, tm)
        bidx = lambda i: (i * tm) // rows_per_mod
        in_specs += [pl.BlockSpec((1, r, d), lambda i, j: (bidx(i), 0, 0)),
                     pl.BlockSpec((1, r, d), lambda i, j: (bidx(i), 0, 1))]
        args += [mod, mod]
    in_specs.append(pl.BlockSpec((None, d, tn), lambda i, j: (layer, 0, j + joff)))
    args.append(w)
    outs = pl.pallas_call(
        functools.partial(_norm_mm_kernel, has_mod=mod is not None, n_out=len(out_dtypes)),
        grid=(m // tm, n_cols // tn),
        in_specs=in_specs,
        out_specs=[pl.BlockSpec((tm, tn), lambda i, j: (i, j)) for _ in out_dtypes],
        out_shape=[jax.ShapeDtypeStruct((m, n_cols), dt) for dt in out_dtypes],
        scratch_shapes=[pltpu.VMEM((tm, d), BF16)],
        compiler_params=_params(2, 56),
    )(*args)
    return outs if len(outs) > 1 else outs[0]


def _mm_post_kernel(a_ref, w_ref, x_ref, gate_ref, g_ref, o_ref):
    y = _dot(a_ref[...].astype(BF16), w_ref[...])
    o_ref[...] = x_ref[...] + gate_ref[0] * _rms(y, g_ref[...])


def _mm_post(a, w, layer, x, mod, g, *, rows_per_mod, tm_pref=512):
    m, kdim = a.shape
    d = w.shape[2]
    tm = _tile(m, tm_pref)
    r = mod.shape[1]
    assert rows_per_mod % tm == 0 and r in (1, tm)
    bidx = lambda i: (i * tm) // rows_per_mod
    return pl.pallas_call(
        _mm_post_kernel,
        grid=(m // tm,),
        in_specs=[
            pl.BlockSpec((tm, kdim), lambda i: (i, 0)),
            pl.BlockSpec((None, kdim, d), lambda i: (layer, 0, 0)),
            pl.BlockSpec((tm, d), lambda i: (i, 0)),
            pl.BlockSpec((1, r, d), lambda i: (bidx(i), 0, 2)),
            pl.BlockSpec((1, d), lambda i: (0, 0)),
        ],
        out_specs=pl.BlockSpec((tm, d), lambda i: (i, 0)),
        out_shape=jax.ShapeDtypeStruct((m, d), F32),
        compiler_params=_params(1, 56),
    )(a, w, x, mod, g)


def _mlp_phases(f, nu, nd, x_ref, gpre_ref, sh_ref, sc_ref, gate_ref, w1_ref, w2_ref, gpost_ref, o_ref,
                h_ref, u_ref):
    tf = w1_ref.shape[1]
    tn = w2_ref.shape[1]

    def pre():
        @pl.when(f == 0)
        def _():
            h = _rms(x_ref[...], gpre_ref[...]) * (1.0 + sc_ref[0]) + sh_ref[0]
            h_ref[...] = h.astype(BF16)

    def up():
        u = jnp.maximum(_dot(h_ref[...], w1_ref[...]), 0.0)
        u_ref[f] = (u * u).astype(BF16)

    def down():
        @pl.when(f >= nu)
        def _():
            y = _dot(u_ref[0], w2_ref[0:tf, :])
            for c in range(1, nu):
                y = y + _dot(u_ref[c], w2_ref[c * tf:(c + 1) * tf, :])
            o_ref[:, pl.ds(pl.multiple_of((f - nu) * tn, tn), tn)] = y

        @pl.when(f == nu + nd - 1)
        def _():
            o_ref[...] = x_ref[...] + gate_ref[0] * _rms(o_ref[...], gpost_ref[...])

    return pre, up, down


def _mlp_plan(x, mod, w1, layer, rows_per_mod, ij, tm_pref=512, tf_pref=512, tn_pref=256):
    m, d = x.shape
    ff = w1.shape[2]
    tm = _tile(m, tm_pref)
    tf = _tile(ff, tf_pref)
    tn = _tile(d, tn_pref)
    nu, nd = ff // tf, d // tn
    r = mod.shape[1]
    assert rows_per_mod % tm == 0 and r in (1, tm)
    bidx = lambda i: (i * tm) // rows_per_mod
    row = lambda g: ij(*g)[0]
    col = lambda g: ij(*g)[1]
    in_specs = [
        pl.BlockSpec((tm, d), lambda *g: (row(g), 0)),
        pl.BlockSpec((1, d), lambda *g: (0, 0)),
        pl.BlockSpec((1, r, d), lambda *g: (bidx(row(g)), 0, 0)),
        pl.BlockSpec((1, r, d), lambda *g: (bidx(row(g)), 0, 1)),
        pl.BlockSpec((1, r, d), lambda *g: (bidx(row(g)), 0, 2)),
        pl.BlockSpec((None, d, tf), lambda *g: (layer, 0, jnp.minimum(col(g), nu - 1))),
        pl.BlockSpec((None, ff, tn), lambda *g: (layer, 0, jnp.maximum(col(g) - nu, 0))),
        pl.BlockSpec((1, d), lambda *g: (0, 0)),
    ]
    out_spec = pl.BlockSpec((tm, d), lambda *g: (row(g), 0))
    scratch = [pltpu.VMEM((tm, d), BF16), pltpu.VMEM((nu, tm, tf), BF16)]
    return (m // tm, nu, nd), in_specs, out_spec, jax.ShapeDtypeStruct((m, d), F32), scratch


def _mlp(x, gpre, mod, w1, w2, layer, gpost, *, rows_per_mod):
    (mi, nu, nd), in_specs, out_spec, out_shape, scratch = _mlp_plan(
        x, mod, w1, layer, rows_per_mod, lambda i, f: (i, f))

    def body(*refs):
        f = pl.program_id(1)
        pre, up, down = _mlp_phases(f, nu, nd, *refs)
        pre()
        pl.when(f < nu)(up)
        down()

    return pl.pallas_call(
        body,
        grid=(mi, nu + nd),
        in_specs=in_specs,
        out_specs=out_spec,
        out_shape=out_shape,
        scratch_shapes=scratch,
        compiler_params=_params(2, 56),
    )(x, gpre, mod, mod, mod, w1, w2, gpost)


def _hgrn_kernel(q_ref, f_ref, i_ref, g_ref, lbl_ref, og_ref, s0_ref, o_ref, s_ref, st_ref,
                 *, layer, rows, n_inner, hpb):
    c = HGRN_CHUNK
    sub = HGRN_SUB
    dk = q_ref.shape[-1] // hpb
    step = pl.program_id(2)

    @pl.when(step == 0)
    def _():
        for h in range(hpb):
            st_ref[h] = s0_ref[0, h].T

    z = lbl_ref[...]
    e = jnp.exp(z - jnp.max(z, axis=0, keepdims=True))
    p = e / jnp.sum(e, axis=0, keepdims=True)
    cs = p[0:1]
    for r in range(1, layer + 1):
        cs = cs + p[r:r + 1]
    lb_all = cs - p[0:1]
    og_all = og_ref[...]

    tril = (_iota((c, c), 0) >= _iota((c, c), 1)).astype(BF16)
    lane = _iota((sub, c), 1)
    sub_row = _iota((sub, c), 0)
    scale = dk ** -0.5

    def pad(t, fill=0.0):
        if rows == c:
            return t
        return jnp.concatenate([t, jnp.full((c - rows, t.shape[1]), fill, t.dtype)], axis=0)

    def cumsum_rows(x):
        hi = x.astype(BF16)
        r1 = x - hi.astype(F32)
        mid = r1.astype(BF16)
        lo = (r1 - mid.astype(F32)).astype(BF16)
        return _dot(tril, hi) + _dot(tril, mid) + _dot(tril, lo)

    def head_chunk(h, sl):
        hs = slice(h * dk, (h + 1) * dk)
        lb = lb_all[:, hs]
        lb_floor = jnp.maximum(lb, TINY)
        one_m_lb = 1.0 - lb
        log2_one_m_lb = jnp.log(one_m_lb) * LOG2E
        qz = q_ref[0, sl, hs]
        fz = f_ref[0, sl, hs]
        q = pad(qz * _sigmoid(qz))
        e = jnp.exp(-jnp.abs(fz))
        t = 1.0 + e
        r = 1.0 / t
        sig = jnp.where(fz >= 0.0, r, e * r)
        logf = pad(jnp.log(lb_floor + one_m_lb * sig))
        lk = pad(log2_one_m_lb + (jnp.minimum(-fz, 0.0) - jnp.log(t)) * LOG2E, -jnp.inf)
        v = pad(i_ref[0, sl, hs])
        b2 = cumsum_rows(logf) * LOG2E
        c2 = b2 - lk
        st = st_ref[h]
        inter = _dot_nt((q * jnp.exp2(b2)).astype(BF16), st.astype(BF16))
        blocks = []
        for i in range(c // sub):
            lo = i * sub
            qi, bi, ci = q[lo:lo + sub], b2[lo:lo + sub], c2[lo:lo + sub]
            if i > 0:
                ref_b = bi[0:1]
                qs = (qi * jnp.exp2(bi - ref_b)).astype(BF16)
                kt = jnp.exp2(ref_b - c2[:lo]).astype(BF16)
                kt = jnp.concatenate([kt, jnp.zeros((c - lo, dk), BF16)], axis=0)
                s_i = _dot_nt(qs, kt)
            else:
                s_i = jnp.zeros((sub, c), F32)
            for s in range(sub):
                w = jnp.sum(qi * jnp.exp2(bi - ci[s:s + 1]), axis=-1, keepdims=True)
                s_i = jnp.where(lane == lo + s, w, s_i)
            blocks.append(jnp.where(lane <= lo + sub_row, s_i, 0.0))
        scores = jnp.concatenate(blocks, axis=0).astype(BF16)
        o = (inter + _dot(scores, v.astype(BF16))) * scale
        blast = b2[c - 1:c]
        kd = jnp.exp2(blast - c2).astype(BF16)
        st_ref[h] = st * jnp.exp2(blast) + _dot(v.T.astype(BF16), kd)
        gz = g_ref[0, sl, hs]
        o_ref[0, sl, hs] = _rms(o[:rows], og_all[:, hs]) * (gz * _sigmoid(gz))

    def chunk(j, carry):
        if n_inner == 1:
            sl = pl.ds(0, rows)
        else:
            sl = pl.ds(pl.multiple_of(j * c, c), c)
        for h in range(hpb):
            head_chunk(h, sl)
        return carry

    lax.fori_loop(0, n_inner, chunk, 0)

    @pl.when(step == pl.num_programs(2) - 1)
    def _():
        for h in range(hpb):
            s_ref[0, h] = st_ref[h].T


def _hgrn(proj, lb_logits, out_g, s0, *, layer, heads):
    b, l, _ = proj.shape
    dk = s0.shape[2]
    dv = s0.shape[3]
    assert dk == LANES and dv == LANES
    c = HGRN_CHUNK
    if l >= c:
        assert l % c == 0
        rows = c
        n_inner = _tile(l // c, 4)
        blk = rows * n_inner
    else:
        assert l % 8 == 0
        rows, n_inner, blk = l, 1, l
    nl = lb_logits.shape[0]
    hpb = _tile(heads, HGRN_HEADS_PER_STEP)
    ng = heads // hpb
    col = lambda part: pl.BlockSpec((1, blk, hpb * dk), lambda bi, h, s: (bi, s, part * ng + h))
    o, s_new = pl.pallas_call(
        functools.partial(_hgrn_kernel, layer=layer, rows=rows, n_inner=n_inner, hpb=hpb),
        grid=(b, ng, l // blk),
        in_specs=[
            col(0), col(1), col(2), col(3),
            pl.BlockSpec((nl, hpb * dk), lambda bi, h, s: (0, h)),
            pl.BlockSpec((1, hpb * dv), lambda bi, h, s: (0, h)),
            pl.BlockSpec((1, hpb, dk, dv), lambda bi, h, s: (bi, h, 0, 0)),
        ],
        out_specs=[
            pl.BlockSpec((1, blk, hpb * dv), lambda bi, h, s: (bi, s, h)),
            pl.BlockSpec((1, hpb, dk, dv), lambda bi, h, s: (bi, h, 0, 0)),
        ],
        out_shape=[jax.ShapeDtypeStruct((b, l, heads * dv), F32),
                   jax.ShapeDtypeStruct((b, heads, dk, dv), F32)],
        scratch_shapes=[pltpu.VMEM((hpb, dv, dk), F32)],
        compiler_params=_params(3, 32),
    )(proj, proj, proj, proj, lb_logits, out_g, s0)
    return o, s_new


def _logf_kernel(x_ref, g_ref, wft_ref, bft_ref, c0_ref, lf_ref, ft_ref, car_ref):
    tm = x_ref.shape[0]

    @pl.when(pl.program_id(1) == 0)
    def _():
        car_ref[...] = c0_ref[0]

    h = _rms(x_ref[...], g_ref[...])
    zt = _dot_nt(wft_ref[...], h, HIGHEST) + bft_ref[...]
    lft = _log_sigmoid(zt)
    lf_ref[0] = lft
    triu = (_iota((tm, tm), 0) <= _iota((tm, tm), 1)).astype(F32)
    ft = _dot(lft, triu, HIGHEST) + car_ref[...]
    ft_ref[0] = ft
    car_ref[...] = ft[:, tm - 1:tm]


def _logf(x, g, wft, bft, c0t, *, batch):
    m, d = x.shape
    l = m // batch
    hh = wft.shape[0]
    tm = _tile(l, 512)
    nt = l // tm
    out = jax.ShapeDtypeStruct((batch, hh, l), F32)
    return pl.pallas_call(
        _logf_kernel,
        grid=(batch, nt),
        in_specs=[
            pl.BlockSpec((tm, d), lambda b, i: (b * nt + i, 0)),
            pl.BlockSpec((1, d), lambda b, i: (0, 0)),
            pl.BlockSpec((hh, d), lambda b, i: (0, 0)),
            pl.BlockSpec((hh, 1), lambda b, i: (0, 0)),
            pl.BlockSpec((1, hh, 1), lambda b, i: (b, 0, 0)),
        ],
        out_specs=[pl.BlockSpec((1, hh, tm), lambda b, i: (b, 0, i)),
                   pl.BlockSpec((1, hh, tm), lambda b, i: (b, 0, i))],
        out_shape=[out, out],
        scratch_shapes=[pltpu.VMEM((hh, 1), F32)],
        compiler_params=_params(2, 32),
    )(x, g, wft, bft, c0t)


def _fox_prompt_kernel(q_ref, k_ref, v_ref, fq_ref, fk_ref, o_ref, m_ref, l_ref, acc_ref, *, tq, hpb):
    i = pl.program_id(2)
    hd = q_ref.shape[-1] // hpb
    n_lt = tq // LANES
    m_ref[...] = jnp.full_like(m_ref, MASK_VALUE)
    l_ref[...] = jnp.zeros_like(l_ref)
    acc_ref[...] = jnp.zeros_like(acc_ref)
    qs, fqs = [], []
    for h in range(hpb):
        qs.append((q_ref[:, h * hd:(h + 1) * hd].astype(F32) * (hd ** -0.5 * LOG2E)).astype(BF16))
        fqs.append(jnp.broadcast_to(fq_ref[0, h] * LOG2E, (tq, LANES)))

    def tile(j, masked):
        ks = pl.ds(pl.multiple_of(j * tq, tq), tq)
        for h in range(hpb):
            cs = slice(h * hd, (h + 1) * hd)
            s = _dot_nt(qs[h], k_ref[ks, cs])
            fk = fk_ref[0, h, :, ks] * LOG2E
            us = []
            for c in range(n_lt):
                ls = slice(c * LANES, (c + 1) * LANES)
                u = s[:, ls] - fk[:, ls]
                if masked:
                    u = jnp.where(_iota((tq, LANES), 0) >= c * LANES + _iota((tq, LANES), 1), u, MASK_VALUE)
                us.append(u)
            row_max = jnp.max(functools.reduce(jnp.maximum, us), axis=-1, keepdims=True)
            m_old = m_ref[h]
            m_new = jnp.maximum(m_old, jnp.broadcast_to(row_max, (tq, LANES)) + fqs[h])
            shift = fqs[h] - m_new
            ps = [jnp.exp2(u + shift) for u in us]
            alpha = jnp.exp2(m_old - m_new)
            m_ref[h] = m_new
            l_ref[h] = alpha * l_ref[h] + functools.reduce(jnp.add, ps)
            p_all = jnp.concatenate([p.astype(BF16) for p in ps], axis=1)
            acc_ref[h] = alpha * acc_ref[h] + _dot(p_all, v_ref[ks, cs])

    def body(j, carry):
        tile(j, False)
        return carry

    lax.fori_loop(0, i, body, 0)
    tile(i, True)
    for h in range(hpb):
        o_ref[:, h * hd:(h + 1) * hd] = acc_ref[h] / jnp.sum(l_ref[h], axis=-1, keepdims=True)


def _fox_prompt(q, k, v, ft, *, batch, heads):
    m, d = q.shape
    l = m // batch
    hd = d // heads
    assert hd == LANES
    hpb = ATTN_HEADS_PER_STEP
    tq = _tile(l, 512)
    nq = l // tq
    fcol = ft[..., None]
    frow = ft[:, :, None, :]
    return pl.pallas_call(
        functools.partial(_fox_prompt_kernel, tq=tq, hpb=hpb),
        grid=(batch, heads // hpb, nq),
        in_specs=[
            pl.BlockSpec((tq, hpb * hd), lambda b, h, i: (b * nq + i, h)),
            pl.BlockSpec((l, hpb * hd), lambda b, h, i: (b, h)),
            pl.BlockSpec((l, hpb * hd), lambda b, h, i: (b, h)),
            pl.BlockSpec((1, hpb, tq, 1), lambda b, h, i: (b, h, i, 0)),
            pl.BlockSpec((1, hpb, 1, l), lambda b, h, i: (b, h, 0, 0)),
        ],
        out_specs=pl.BlockSpec((tq, hpb * hd), lambda b, h, i: (b * nq + i, h)),
        out_shape=jax.ShapeDtypeStruct((m, d), F32),
        scratch_shapes=[pltpu.VMEM((hpb, tq, LANES), F32), pltpu.VMEM((hpb, tq, LANES), F32),
                        pltpu.VMEM((hpb, tq, hd), F32)],
        compiler_params=_params(3, 40),
    )(q, k, v, fcol, frow)


def _fpast_kernel(pt_ref, lf_ref, o_ref, car_ref, *, n_pg):
    b, g = pl.program_id(0), pl.program_id(1)
    n_grp, prow = lf_ref.shape[1], lf_ref.shape[2]
    rows = n_pg * prow
    keys_per_row = LANES // HEAD_GROUP

    @pl.when(g == 0)
    def _():
        car_ref[...] = jnp.zeros_like(car_ref)

    li, lj = _iota((LANES, LANES), 0), _iota((LANES, LANES), 1)
    same_head = (li % HEAD_GROUP) == (lj % HEAD_GROUP)
    prefix = (same_head & (li // HEAD_GROUP <= lj // HEAD_GROUP)).astype(F32)
    last = (same_head & (li // HEAD_GROUP == keys_per_row - 1)).astype(F32)
    strict = (_iota((rows, rows), 0) > _iota((rows, rows), 1)).astype(F32)
    for a in range(n_grp):
        x = jnp.concatenate([lf_ref[pt_ref[b, g * n_pg + o], a] for o in range(n_pg)], axis=0)
        y = _dot(x, prefix, HIGHEST)
        tot = _dot(y, last, HIGHEST)
        o_ref[0, a] = y + _dot(strict, tot, HIGHEST) + car_ref[a]
        car_ref[a] += jnp.sum(tot, axis=0, keepdims=True)


def _fpast(page_table, cache_logf_grp):
    b, n_pages = page_table.shape
    n_phys, n_grp, prow, _ = cache_logf_grp.shape
    n_pg = _tile(n_pages, FPAST_PAGES_PER_STEP)
    return pl.pallas_call(
        functools.partial(_fpast_kernel, n_pg=n_pg),
        grid_spec=pltpu.PrefetchScalarGridSpec(
            num_scalar_prefetch=1,
            grid=(b, n_pages // n_pg),
            in_specs=[pl.BlockSpec((n_phys, n_grp, prow, LANES), lambda bi, g, pt: (0, 0, 0, 0))],
            out_specs=pl.BlockSpec((1, n_grp, n_pg * prow, LANES), lambda bi, g, pt: (bi, 0, g, 0)),
            scratch_shapes=[pltpu.VMEM((n_grp, 1, LANES), F32)],
        ),
        out_shape=jax.ShapeDtypeStruct((b, n_grp, n_pages * prow, LANES), F32),
        compiler_params=_params(2, 40),
    )(page_table, cache_logf_grp)


def _fox_sample_phases(step, nsteps, q_ref, *refs, heads, n_pg):
    kc_refs, vc_refs = refs[:n_pg], refs[n_pg:2 * n_pg]
    fp_ref, kn_ref, vn_ref, fq_ref, fn_ref, o_ref, qa_ref, m_ref, l_ref, acc_ref = refs[2 * n_pg:]
    t, d = q_ref.shape
    hd = d // heads
    grp = HEAD_GROUP
    n_grp = heads // grp
    r = grp * t
    page = kc_refs[0].shape[1]
    prow = page * grp // LANES

    def pre():
        @pl.when(step == 0)
        def _():
            q = q_ref[...] * (hd ** -0.5 * LOG2E)
            for a in range(n_grp):
                qa_ref[a] = jnp.concatenate(
                    [q[:, (a * grp + j) * hd:(a * grp + j + 1) * hd] for j in range(grp)], axis=0)
            m_ref[...] = jnp.full_like(m_ref, MASK_VALUE)
            l_ref[...] = jnp.zeros_like(l_ref)
            acc_ref[...] = jnp.zeros_like(acc_ref)

    head_match = (_iota((r, LANES), 0) // t) == (_iota((r, LANES), 1) % grp)

    def update(a, k_list, v_list, fk_list, mask):
        fq = jnp.broadcast_to(fq_ref[0, a] * LOG2E, (r, LANES))
        qa = qa_ref[a]
        us = []
        for k_flat, fk in zip(k_list, fk_list):
            s = _dot_nt(qa, k_flat)
            for c in range(k_flat.shape[0] // LANES):
                u = s[:, c * LANES:(c + 1) * LANES] - fk[c:c + 1] * LOG2E
                us.append(jnp.where(mask, u, MASK_VALUE))
        row_max = jnp.max(functools.reduce(jnp.maximum, us), axis=-1, keepdims=True)
        m_old = m_ref[a]
        m_new = jnp.maximum(m_old, jnp.broadcast_to(row_max, (r, LANES)) + fq)
        shift = fq - m_new
        ps = [jnp.exp2(u + shift) for u in us]
        alpha = jnp.exp2(m_old - m_new)
        m_ref[a] = m_new
        l_ref[a] = alpha * l_ref[a] + functools.reduce(jnp.add, ps)
        pv, at = None, 0
        for v_flat in v_list:
            n_t = v_flat.shape[0] // LANES
            p_pg = jnp.concatenate(ps[at:at + n_t], axis=1) if n_t > 1 else ps[at]
            at += n_t
            part = _dot(p_pg, v_flat)
            pv = part if pv is None else pv + part
        acc_ref[a] = alpha * acc_ref[a] + pv

    def group_rows(ref, a):
        blk = ref[0, :, a * grp:(a + 1) * grp, :]
        return blk.reshape(blk.shape[0] * grp, hd)

    def main():
        for a in range(n_grp):
            update(a, [group_rows(kc, a) for kc in kc_refs], [group_rows(vc, a) for vc in vc_refs],
                   [fp_ref[0, a, o * prow:(o + 1) * prow] for o in range(n_pg)], head_match)

    def post():
        @pl.when(step == nsteps - 1)
        def _():
            lane = _iota((r, LANES), 1)
            new_mask = head_match & (lane // grp <= _iota((r, LANES), 0) % t) & (lane < t * grp)
            zeros = jnp.zeros((LANES - t * grp, hd), F32)
            for a in range(n_grp):
                kn = jnp.concatenate([group_rows(kn_ref, a), zeros], axis=0)
                vn = jnp.concatenate([group_rows(vn_ref, a), zeros], axis=0)
                update(a, [kn], [vn], [fn_ref[0, a]], new_mask)
                res = acc_ref[a] / jnp.sum(l_ref[a], axis=-1, keepdims=True)
                for j in range(grp):
                    h = a * grp + j
                    o_ref[:, h * hd:(h + 1) * hd] = res[j * t:(j + 1) * t]

    return pre, main, post


def _fox_sample(q, cache_k, cache_v, page_table, fpast, k_new, v_new, fq_col, fnew_flat):
    plan = _fox_sample_plan(q, cache_k, page_table, lambda bi, s: (bi, s))
    grid, n_pg, in_specs, out_spec, out_shape, scratch = plan
    heads = cache_k.shape[2]

    def body(pt_ref, *refs):
        for phase in _fox_sample_phases(pl.program_id(1), pl.num_programs(1), *refs, heads=heads, n_pg=n_pg):
            phase()

    return pl.pallas_call(
        body,
        grid_spec=pltpu.PrefetchScalarGridSpec(
            num_scalar_prefetch=1, grid=grid, in_specs=in_specs, out_specs=out_spec, scratch_shapes=scratch),
        out_shape=out_shape,
        compiler_params=_params(2, 48),
    )(page_table, q, *([cache_k] * n_pg), *([cache_v] * n_pg), fpast, k_new, v_new, fq_col, fnew_flat)


def _fox_sample_plan(q, cache_k, page_table, bs):
    b, n_pages = page_table.shape
    m, d = q.shape
    t = m // b
    _, page, heads, hd = cache_k.shape
    grp = HEAD_GROUP
    n_grp = heads // grp
    assert hd == LANES and heads % grp == 0 and t * grp <= LANES and (page * grp) % LANES == 0
    r = grp * t
    n_pg = _tile(n_pages, PAGES_PER_STEP)
    prow = page * grp // LANES
    seq = lambda g: bs(*g)[0]
    pgrp = lambda g: bs(*g)[1]
    page_spec = lambda o: pl.BlockSpec(
        (1, page, heads, hd), lambda *g: (g[-1][seq(g[:-1]), pgrp(g[:-1]) * n_pg + o], 0, 0, 0))
    new_spec = pl.BlockSpec((1, t, heads, hd), lambda *g: (seq(g[:-1]), 0, 0, 0))
    in_specs = ([pl.BlockSpec((t, d), lambda *g: (seq(g[:-1]), 0))]
                + [page_spec(o) for o in range(n_pg)] * 2
                + [pl.BlockSpec((1, n_grp, n_pg * prow, LANES), lambda *g: (seq(g[:-1]), 0, pgrp(g[:-1]), 0)),
                   new_spec, new_spec,
                   pl.BlockSpec((1, n_grp, r, 1), lambda *g: (seq(g[:-1]), 0, 0, 0)),
                   pl.BlockSpec((1, n_grp, 1, LANES), lambda *g: (seq(g[:-1]), 0, 0, 0))])
    out_spec = pl.BlockSpec((t, d), lambda *g: (seq(g[:-1]), 0))
    scratch = [pltpu.VMEM((n_grp, r, hd), F32), pltpu.VMEM((n_grp, r, LANES), F32),
               pltpu.VMEM((n_grp, r, LANES), F32), pltpu.VMEM((n_grp, r, hd), F32)]
    return (b, n_pages // n_pg), n_pg, in_specs, out_spec, jax.ShapeDtypeStruct((m, d), F32), scratch


def _pairable(mlp_args, attn_args):
    x, mod, w1, layer, rows_per_mod = mlp_args[0], mlp_args[2], mlp_args[3], mlp_args[5], mlp_args[7]
    mi, nu, _ = _mlp_plan(x, mod, w1, layer, rows_per_mod, lambda i, f: (i, f))[0]
    nb, ns = _fox_sample_plan(attn_args[0], attn_args[1], attn_args[3], lambda bi, s: (bi, s))[0]
    return mi * nu == nb * ns


def _mlp_attn(mlp_args, attn_args):
    x, gpre, mod, w1, w2, layer, gpost, rows_per_mod = mlp_args
    q, cache_k, cache_v, page_table, fpast, k_new, v_new, fq_col, fnew_flat = attn_args
    heads = cache_k.shape[2]
    ext = {}
    (mi, nu, nd), mlp_in, mlp_out, mlp_shape, mlp_scratch = _mlp_plan(
        x, mod, w1, layer, rows_per_mod, lambda s, *pt: (s // ext["nf"], s % ext["nf"]))
    nf = nu + nd
    attn_step = lambda s: (s // ext["nf"]) * ext["nu"] + jnp.minimum(s % ext["nf"], ext["nu"] - 1)
    (nb, ns), n_pg, attn_in, attn_out, attn_shape, attn_scratch = _fox_sample_plan(
        q, cache_k, page_table, lambda s: (attn_step(s) // ext["ns"], attn_step(s) % ext["ns"]))
    ext.update(nf=nf, nu=nu, ns=ns)
    assert mi * nu == nb * ns
    n_mlp_in, n_attn_in = len(mlp_in), len(attn_in)

    def body(pt_ref, *refs):
        s = pl.program_id(0)
        i, f = s // nf, s % nf
        mlp_refs = refs[:n_mlp_in]
        attn_refs = refs[n_mlp_in:n_mlp_in + n_attn_in]
        o_mlp, o_attn = refs[n_mlp_in + n_attn_in:n_mlp_in + n_attn_in + 2]
        scr = refs[n_mlp_in + n_attn_in + 2:]
        a_step = i * nu + f
        mlp_pre, mlp_up, mlp_down = _mlp_phases(f, nu, nd, *mlp_refs, o_mlp, *scr[:len(mlp_scratch)])
        attn_pre, attn_main, attn_post = _fox_sample_phases(
            a_step % ns, ns, *attn_refs, o_attn, *scr[len(mlp_scratch):], heads=heads, n_pg=n_pg)
        mlp_pre()

        @pl.when(f < nu)
        def _():
            attn_pre()
            mlp_up()
            attn_main()
            attn_post()

        mlp_down()

    return pl.pallas_call(
        body,
        grid_spec=pltpu.PrefetchScalarGridSpec(
            num_scalar_prefetch=1, grid=(mi * nf,), in_specs=mlp_in + attn_in,
            out_specs=[mlp_out, attn_out], scratch_shapes=mlp_scratch + attn_scratch),
        out_shape=[mlp_shape, attn_shape],
        compiler_params=_params(1, 60),
    )(page_table, x, gpre, mod, mod, mod, w1, w2, gpost,
      q, *([cache_k] * n_pg), *([cache_v] * n_pg), fpast, k_new, v_new, fq_col, fnew_flat)


def _trunk(x3, mods, s0, attend_ctx, wts):
    (norm_pre, norm_post, a_w_in, a_w_out, a_lb_logits, a_out_norm, kv_norm, w_kv, wft, b_forget,
     b_w_q, b_w_o, mlp_w_in, mlp_w_out) = wts
    b, l, d = x3.shape
    m = b * l
    depth = norm_pre.shape[0]
    n_a = a_w_in.shape[0]
    a_heads = s0.shape[2]
    b_heads = b_forget.shape[0]
    x = x3.reshape(m, d)
    if m <= 512:
        rows_per_mod = m
        expand = lambda t: jnp.repeat(t, l, axis=0)[None]
    else:
        rows_per_mod = l
        expand = lambda t: t[:, None, :]
    row = lambda t: t.reshape(1, -1)
    states = []
    shared = None
    for layer in range(depth):
        mod = expand(mods[layer, 0])
        if layer < n_a:
            proj = _norm_mm(x, row(norm_pre[layer, 0]), mod, a_w_in, layer, rows_per_mod=rows_per_mod)
            o, s_new = _hgrn(proj.reshape(b, l, -1), a_lb_logits, row(a_out_norm[layer]), s0[layer],
                             layer=layer, heads=a_heads)
            states.append(s_new)
            y_in, w_o, w_idx = o.reshape(m, d), a_w_out, layer
        else:
            j = layer - n_a
            if shared is None:
                kvg = row(kv_norm)
                k, k16 = _norm_mm(x, kvg, None, w_kv, 0, rows_per_mod=rows_per_mod, n_cols=d, col_off=0,
                                  out_dtypes=(F32, BF16))
                v, v16 = _norm_mm(x, kvg, None, w_kv, 0, rows_per_mod=rows_per_mod, n_cols=d, col_off=d,
                                  out_dtypes=(F32, BF16))
                shared = attend_ctx["make"](x, kvg, wft, b_forget.reshape(b_heads, 1), k, v, k16, v16)
            q = _norm_mm(x, row(norm_pre[layer, 0]), mod, b_w_q, j, rows_per_mod=rows_per_mod,
                         out_dtypes=(attend_ctx["q_dtype"],))
            y_in, w_o, w_idx = (yield "attend", (q, shared)), b_w_o, j
        x = _mm_post(y_in, w_o, w_idx, x, mod, row(norm_post[layer, 0]), rows_per_mod=rows_per_mod)
        mod = expand(mods[layer, 1])
        x = yield "mlp", (x, row(norm_pre[layer, 1]), mod, mlp_w_in, mlp_w_out, layer, row(norm_post[layer, 1]),
                          rows_per_mod)
    k, v, lft = shared["rows"]
    kv_shape = (b, l, b_heads, d // b_heads)
    return (x.reshape(b, l, d), jnp.stack(states), k.reshape(kv_shape), v.reshape(kv_shape),
            jnp.swapaxes(lft, 1, 2))


def kernel(x_prompt, x_sample, c_prompt, c_sample, state_hgrn, cache_k, cache_v, cache_logf, page_table,
           norm_pre, norm_post, w_ada, b_ada, a_w_in, a_w_out, a_lb_logits, a_out_norm,
           kv_norm, w_kv, b_forget, b_w_q, b_w_o, mlp_w_in, mlp_w_out):
    bp, _, d = x_prompt.shape
    bs, ts, _ = x_sample.shape
    depth = norm_pre.shape[0]
    n_phys, page, heads, hd = cache_k.shape
    wts = (norm_pre, norm_post, a_w_in.astype(BF16), a_w_out.astype(BF16), a_lb_logits, a_out_norm, kv_norm,
           w_kv.astype(BF16)[None], w_kv[:, 2 * d:].T, b_forget,
           b_w_q.astype(BF16), b_w_o.astype(BF16), mlp_w_in.astype(BF16), mlp_w_out.astype(BF16))

    n_c = bp + bs
    c_rows = -(-n_c // 8) * 8
    c_all = jnp.concatenate([c_prompt, c_sample, jnp.zeros((c_rows - n_c, d), F32)], axis=0)
    mods = _ada_all(c_all, w_ada.reshape(depth * 2, d, 3 * d), b_ada.reshape(depth * 2, 1, 3 * d))
    mods = mods.reshape(depth, 2, c_rows, 3 * d)
    mods_p, mods_s = mods[:, :, :bp], mods[:, :, bp:n_c]

    def make_prompt(x, kvg, wft, bft, k, v, k16, v16):
        lft, ft = _logf(x, kvg, wft, bft, jnp.zeros((bp, heads, 1), F32), batch=bp)
        return {"rows": (k, v, lft), "k": k16, "v": v16, "ft": ft}

    def attend_prompt(q, sh):
        return _fox_prompt(q, sh["k"], sh["v"], sh["ft"], batch=bp, heads=heads)

    s0_prompt = jnp.zeros((a_w_in.shape[0], bp) + state_hgrn.shape[2:], F32)
    gen_p = _trunk(x_prompt, mods_p, s0_prompt, {"make": make_prompt, "q_dtype": BF16}, wts)

    grp = HEAD_GROUP
    n_grp = heads // grp
    prow = page * grp // LANES
    lf_grp = cache_logf.astype(F32).reshape(n_phys, page, n_grp, grp).transpose(0, 2, 1, 3)
    fpast = _fpast(page_table, lf_grp.reshape(n_phys, n_grp, prow, LANES))
    f_total = fpast[:, :, -1, LANES - grp:].reshape(bs, heads, 1)

    def make_sample(x, kvg, wft, bft, k, v, k16, v16):
        lft, fnew_t = _logf(x, kvg, wft, bft, f_total, batch=bs)
        fq_col = fnew_t.reshape(bs, n_grp, grp * ts, 1)
        fnew_flat = fnew_t.reshape(bs, n_grp, grp, ts).transpose(0, 1, 3, 2).reshape(bs, n_grp, 1, ts * grp)
        fnew_flat = jnp.pad(fnew_flat, ((0, 0), (0, 0), (0, 0), (0, LANES - ts * grp)))
        shape4 = (bs, ts, heads, hd)
        return {"rows": (k, v, lft), "k": k.reshape(shape4), "v": v.reshape(shape4), "fq": fq_col, "fn": fnew_flat}

    def sample_attn_args(q, sh):
        return (q, cache_k, cache_v, page_table, fpast, sh["k"], sh["v"], sh["fq"], sh["fn"])

    gen_s = _trunk(x_sample, mods_s, state_hgrn, {"make": make_sample, "q_dtype": F32}, wts)

    def advance(gen, value):
        try:
            return gen.send(value), None
        except StopIteration as done:
            return None, done.value

    def run_mlp(a):
        return _mlp(*a[:7], rows_per_mod=a[7])

    req_p, out_p = advance(gen_p, None)
    req_s, out_s = advance(gen_s, None)
    while req_p is not None or req_s is not None:
        if req_s is not None and req_s[0] == "mlp":
            req_s, out_s = advance(gen_s, run_mlp(req_s[1]))
        elif req_p is not None and req_p[0] == "attend":
            req_p, out_p = advance(gen_p, attend_prompt(*req_p[1]))
        elif req_p is not None and req_s is not None and _pairable(req_p[1], sample_attn_args(*req_s[1])):
            x_new, y_att = _mlp_attn(req_p[1], sample_attn_args(*req_s[1]))
            req_p, out_p = advance(gen_p, x_new)
            req_s, out_s = advance(gen_s, y_att)
        elif req_s is not None:
            req_s, out_s = advance(gen_s, _fox_sample(*sample_attn_args(*req_s[1])))
        else:
            req_p, out_p = advance(gen_p, run_mlp(req_p[1]))
    y_p, st_p, k_p, v_p, lf_p = out_p
    y_s, st_s, k_s, v_s, lf_s = out_s

    return (y_p, y_s, st_p, st_s, k_p, v_p, lf_p, k_s, v_s, lf_s)
```

```python
import functools
import math

import jax
import jax.numpy as jnp
from jax import lax
from jax.experimental import pallas as pl
from jax.experimental.pallas import tpu as pltpu

EPS = 1e-6
MASK_VALUE = -1e30
TINY = 1e-30
LOG2E = math.log2(math.e)
F32 = jnp.float32
BF16 = jnp.bfloat16
HIGHEST = lax.Precision.HIGHEST

LANES = 128
HGRN_CHUNK = 128
HGRN_SUB = 8
SUBLANES = 8
HEAD_GROUP = SUBLANES
PAGES_PER_STEP = 8
FPAST_PAGES_PER_STEP = 8
HGRN_HEADS_PER_STEP = 4
ATTN_HEADS_PER_STEP = 4
MIB = 2 ** 20


def _params(n_axes, vmem_mib):
    return pltpu.CompilerParams(
        dimension_semantics=("arbitrary",) * n_axes, vmem_limit_bytes=vmem_mib * MIB)


def _tile(n, pref):
    t = min(n, pref)
    while n % t:
        t -= 1
    return t


def _sigmoid(x):
    return 1.0 / (1.0 + jnp.exp(-x))


def _log_sigmoid(x):
    return jnp.minimum(x, 0.0) - jnp.log1p(jnp.exp(-jnp.abs(x)))


def _rms(x, g):
    return x * lax.rsqrt(jnp.mean(x * x, axis=-1, keepdims=True) + EPS) * g


def _dot(a, b, precision=None):
    return jnp.dot(a, b, preferred_element_type=F32, precision=precision)


def _dot_nt(a, b, precision=None):
    return lax.dot_general(a, b, (((1,), (1,)), ((), ())), preferred_element_type=F32, precision=precision)


def _iota(shape, dim):
    return lax.broadcasted_iota(jnp.int32, shape, dim)


def _ada_kernel(c_ref, w_ref, b_ref, o_ref):
    c = c_ref[...]
    s = (c * _sigmoid(c)).astype(BF16)
    o_ref[0] = _dot(s, w_ref[0].astype(BF16)) + b_ref[0]


def _ada_all(c_all, w_ada, b_ada):
    n, d, n3 = w_ada.shape
    r = c_all.shape[0]
    tn = _tile(n3, 1024)
    return pl.pallas_call(
        _ada_kernel,
        grid=(n, n3 // tn),
        in_specs=[
            pl.BlockSpec((r, d), lambda i, j: (0, 0)),
            pl.BlockSpec((1, d, tn), lambda i, j: (i, 0, j)),
            pl.BlockSpec((1, 1, tn), lambda i, j: (i, 0, j)),
        ],
        out_specs=pl.BlockSpec((1, r, tn), lambda i, j: (i, 0, j)),
        out_shape=jax.ShapeDtypeStruct((n, r, n3), F32),
        compiler_params=_params(2, 40),
    )(c_all, w_ada, b_ada)


def _norm_mm_kernel(*refs, has_mod, n_out, emit):
    if has_mod:
        x_ref, g_ref, sh_ref, sc_ref, w_ref = refs[:5]
        rest = refs[5:]
    else:
        x_ref, g_ref, w_ref = refs[:3]
        rest = refs[3:]
    o_refs, h_ref = rest[:n_out], rest[-1]

    @pl.when(pl.program_id(1) == 0)
    def _():
        h = _rms(x_ref[...], g_ref[...])
        if has_mod:
            h = h * (1.0 + sc_ref[0]) + sh_ref[0]
        h_ref[...] = h.astype(BF16)

    w = w_ref[0].astype(BF16)
    if emit:
        rest[n_out][0] = w
    y = _dot(h_ref[...], w)
    for o_ref in o_refs:
        o_ref[...] = y.astype(o_ref.dtype)


def _norm_mm(x, g, mod, w, layer, *, rows_per_mod, n_cols=None, col_off=0, out_dtypes=(F32,),
             tm_pref=1024, tn_pref=1024):
    m, d = x.shape
    n_cols = w.shape[2] if n_cols is None else n_cols
    tm = _tile(m, tm_pref)
    tn = _tile(n_cols, tn_pref)
    assert col_off % tn == 0
    joff = col_off // tn
    emit = w.dtype != BF16
    assert not emit or (col_off == 0 and n_cols == w.shape[2])
    in_specs = [pl.BlockSpec((tm, d), lambda i, j: (i, 0)), pl.BlockSpec((1, d), lambda i, j: (0, 0))]
    args = [x, g]
    if mod is not None:
        r = mod.shape[1]
        assert rows_per_mod % tm == 0 and r in (1, tm)
        bidx = lambda i: (i * tm) // rows_per_mod
        in_specs += [pl.BlockSpec((1, r, d), lambda i, j: (bidx(i), 0, 0)),
                     pl.BlockSpec((1, r, d), lambda i, j: (bidx(i), 0, 1))]
        args += [mod, mod]
    in_specs.append(pl.BlockSpec((1, d, tn), lambda i, j: (layer, 0, j + joff)))
    args.append(w)
    out_specs = [pl.BlockSpec((tm, tn), lambda i, j: (i, j)) for _ in out_dtypes]
    out_shape = [jax.ShapeDtypeStruct((m, n_cols), dt) for dt in out_dtypes]
    if emit:
        out_specs.append(pl.BlockSpec((1, d, tn), lambda i, j: (0, 0, j)))
        out_shape.append(jax.ShapeDtypeStruct((1, d, n_cols), BF16))
    outs = pl.pallas_call(
        functools.partial(_norm_mm_kernel, has_mod=mod is not None, n_out=len(out_dtypes), emit=emit),
        grid=(m // tm, n_cols // tn),
        in_specs=in_specs,
        out_specs=out_specs,
        out_shape=out_shape,
        scratch_shapes=[pltpu.VMEM((tm, d), BF16)],
        compiler_params=_params(2, 56),
    )(*args)
    return outs if len(outs) > 1 else outs[0]


def _mm_post_kernel(a_ref, w_ref, x_ref, gate_ref, g_ref, o_ref):
    y = _dot(a_ref[...].astype(BF16), w_ref[...])
    o_ref[...] = x_ref[...] + gate_ref[0] * _rms(y, g_ref[...])


def _mm_post(a, w, layer, x, mod, g, *, rows_per_mod, tm_pref=512):
    m, kdim = a.shape
    d = w.shape[2]
    tm = _tile(m, tm_pref)
    r = mod.shape[1]
    assert rows_per_mod % tm == 0 and r in (1, tm)
    bidx = lambda i: (i * tm) // rows_per_mod
    return pl.pallas_call(
        _mm_post_kernel,
        grid=(m // tm,),
        in_specs=[
            pl.BlockSpec((tm, kdim), lambda i: (i, 0)),
            pl.BlockSpec((None, kdim, d), lambda i: (layer, 0, 0)),
            pl.BlockSpec((tm, d), lambda i: (i, 0)),
            pl.BlockSpec((1, r, d), lambda i: (bidx(i), 0, 2)),
            pl.BlockSpec((1, d), lambda i: (0, 0)),
        ],
        out_specs=pl.BlockSpec((tm, d), lambda i: (i, 0)),
        out_shape=jax.ShapeDtypeStruct((m, d), F32),
        compiler_params=_params(1, 56),
    )(a, w, x, mod, g)


def _mlp_kernel(*refs, emit):
    x_ref, gpre_ref, sh_ref, sc_ref, gate_ref, w1_ref, w2_ref, gpost_ref, o_ref = refs[:9]
    h_ref, acc_ref = refs[-2:]
    f = pl.program_id(1)

    @pl.when(f == 0)
    def _():
        h = _rms(x_ref[...], gpre_ref[...]) * (1.0 + sc_ref[0]) + sh_ref[0]
        h_ref[...] = h.astype(BF16)
        acc_ref[...] = jnp.zeros_like(acc_ref)

    w1 = w1_ref[0].astype(BF16)
    w2 = w2_ref[0].astype(BF16)
    if emit:
        w1b_ref, w2b_ref = refs[9:11]
        w1b_ref[0] = w1
        w2b_ref[0] = w2
    u = jnp.maximum(_dot(h_ref[...], w1), 0.0)
    acc_ref[...] += _dot((u * u).astype(BF16), w2)

    @pl.when(f == pl.num_programs(1) - 1)
    def _():
        o_ref[...] = x_ref[...] + gate_ref[0] * _rms(acc_ref[...], gpost_ref[...])


def _mlp(x, gpre, mod, w1, w2, layer, gpost, *, rows_per_mod, tm_pref=512, tf_pref=512):
    m, d = x.shape
    ff = w1.shape[2]
    tm = _tile(m, tm_pref)
    tf = _tile(ff, tf_pref)
    r = mod.shape[1]
    assert rows_per_mod % tm == 0 and r in (1, tm)
    emit = w1.dtype != BF16
    bidx = lambda i: (i * tm) // rows_per_mod
    out_specs = [pl.BlockSpec((tm, d), lambda i, f: (i, 0))]
    out_shape = [jax.ShapeDtypeStruct((m, d), F32)]
    if emit:
        out_specs += [pl.BlockSpec((1, d, tf), lambda i, f: (0, 0, f)),
                      pl.BlockSpec((1, tf, d), lambda i, f: (0, f, 0))]
        out_shape += [jax.ShapeDtypeStruct((1, d, ff), BF16), jax.ShapeDtypeStruct((1, ff, d), BF16)]
    outs = pl.pallas_call(
        functools.partial(_mlp_kernel, emit=emit),
        grid=(m // tm, ff // tf),
        in_specs=[
            pl.BlockSpec((tm, d), lambda i, f: (i, 0)),
            pl.BlockSpec((1, d), lambda i, f: (0, 0)),
            pl.BlockSpec((1, r, d), lambda i, f: (bidx(i), 0, 0)),
            pl.BlockSpec((1, r, d), lambda i, f: (bidx(i), 0, 1)),
            pl.BlockSpec((1, r, d), lambda i, f: (bidx(i), 0, 2)),
            pl.BlockSpec((1, d, tf), lambda i, f: (layer, 0, f)),
            pl.BlockSpec((1, tf, d), lambda i, f: (layer, f, 0)),
            pl.BlockSpec((1, d), lambda i, f: (0, 0)),
        ],
        out_specs=out_specs,
        out_shape=out_shape,
        scratch_shapes=[pltpu.VMEM((tm, d), BF16), pltpu.VMEM((tm, d), F32)],
        compiler_params=_params(2, 56),
    )(x, gpre, mod, mod, mod, w1, w2, gpost)
    return outs if emit else outs[0]


def _hgrn_kernel(q_ref, f_ref, i_ref, g_ref, lbl_ref, og_ref, s0_ref, o_ref, s_ref, st_ref,
                 *, layer, rows, n_inner, hpb):
    c = HGRN_CHUNK
    sub = HGRN_SUB
    dk = q_ref.shape[-1] // hpb
    step = pl.program_id(2)

    @pl.when(step == 0)
    def _():
        for h in range(hpb):
            st_ref[h] = s0_ref[0, h].T

    z = lbl_ref[...]
    e = jnp.exp(z - jnp.max(z, axis=0, keepdims=True))
    p = e / jnp.sum(e, axis=0, keepdims=True)
    cs = p[0:1]
    for r in range(1, layer + 1):
        cs = cs + p[r:r + 1]
    lb_all = cs - p[0:1]
    og_all = og_ref[...]

    tril = (_iota((c, c), 0) >= _iota((c, c), 1)).astype(BF16)
    lane = _iota((sub, c), 1)
    sub_row = _iota((sub, c), 0)
    scale = dk ** -0.5

    def pad(t, fill=0.0):
        if rows == c:
            return t
        return jnp.concatenate([t, jnp.full((c - rows, t.shape[1]), fill, t.dtype)], axis=0)

    def cumsum_rows(x):
        hi = x.astype(BF16)
        r1 = x - hi.astype(F32)
        mid = r1.astype(BF16)
        lo = (r1 - mid.astype(F32)).astype(BF16)
        return _dot(tril, hi) + _dot(tril, mid) + _dot(tril, lo)

    def head_chunk(h, sl):
        hs = slice(h * dk, (h + 1) * dk)
        lb = lb_all[:, hs]
        lb_floor = jnp.maximum(lb, TINY)
        one_m_lb = 1.0 - lb
        log2_one_m_lb = jnp.log(one_m_lb) * LOG2E
        qz = q_ref[0, sl, hs]
        fz = f_ref[0, sl, hs]
        q = pad(qz * _sigmoid(qz))
        e = jnp.exp(-jnp.abs(fz))
        t = 1.0 + e
        r = 1.0 / t
        sig = jnp.where(fz >= 0.0, r, e * r)
        logf = pad(jnp.log(lb_floor + one_m_lb * sig))
        lk = pad(log2_one_m_lb + (jnp.minimum(-fz, 0.0) - jnp.log(t)) * LOG2E, -jnp.inf)
        v = pad(i_ref[0, sl, hs])
        b2 = cumsum_rows(logf) * LOG2E
        c2 = b2 - lk
        st = st_ref[h]
        inter = _dot_nt((q * jnp.exp2(b2)).astype(BF16), st.astype(BF16))
        blocks = []
        for i in range(c // sub):
            lo = i * sub
            qi, bi, ci = q[lo:lo + sub], b2[lo:lo + sub], c2[lo:lo + sub]
            if i > 0:
                ref_b = bi[0:1]
                qs = (qi * jnp.exp2(bi - ref_b)).astype(BF16)
                kt = jnp.exp2(ref_b - c2[:lo]).astype(BF16)
                kt = jnp.concatenate([kt, jnp.zeros((c - lo, dk), BF16)], axis=0)
                s_i = _dot_nt(qs, kt)
            else:
                s_i = jnp.zeros((sub, c), F32)
            for s in range(sub):
                w = jnp.sum(qi * jnp.exp2(bi - ci[s:s + 1]), axis=-1, keepdims=True)
                s_i = jnp.where(lane == lo + s, w, s_i)
            blocks.append(jnp.where(lane <= lo + sub_row, s_i, 0.0))
        scores = jnp.concatenate(blocks, axis=0).astype(BF16)
        o = (inter + _dot(scores, v.astype(BF16))) * scale
        blast = b2[c - 1:c]
        kd = jnp.exp2(blast - c2).astype(BF16)
        st_ref[h] = st * jnp.exp2(blast) + _dot(v.T.astype(BF16), kd)
        gz = g_ref[0, sl, hs]
        o_ref[0, sl, hs] = _rms(o[:rows], og_all[:, hs]) * (gz * _sigmoid(gz))

    def chunk(j, carry):
        if n_inner == 1:
            sl = pl.ds(0, rows)
        else:
            sl = pl.ds(pl.multiple_of(j * c, c), c)
        for h in range(hpb):
            head_chunk(h, sl)
        return carry

    lax.fori_loop(0, n_inner, chunk, 0)

    @pl.when(step == pl.num_programs(2) - 1)
    def _():
        for h in range(hpb):
            s_ref[0, h] = st_ref[h].T


def _hgrn(proj, lb_logits, out_g, s0, *, layer, heads):
    b, l, _ = proj.shape
    dk = s0.shape[2]
    dv = s0.shape[3]
    assert dk == LANES and dv == LANES
    c = HGRN_CHUNK
    if l >= c:
        assert l % c == 0
        rows = c
        n_inner = _tile(l // c, 4)
        blk = rows * n_inner
    else:
        assert l % 8 == 0
        rows, n_inner, blk = l, 1, l
    nl = lb_logits.shape[0]
    hpb = _tile(heads, HGRN_HEADS_PER_STEP)
    ng = heads // hpb
    col = lambda part: pl.BlockSpec((1, blk, hpb * dk), lambda bi, h, s: (bi, s, part * ng + h))
    o, s_new = pl.pallas_call(
        functools.partial(_hgrn_kernel, layer=layer, rows=rows, n_inner=n_inner, hpb=hpb),
        grid=(b, ng, l // blk),
        in_specs=[
            col(0), col(1), col(2), col(3),
            pl.BlockSpec((nl, hpb * dk), lambda bi, h, s: (0, h)),
            pl.BlockSpec((1, hpb * dv), lambda bi, h, s: (0, h)),
            pl.BlockSpec((1, hpb, dk, dv), lambda bi, h, s: (bi, h, 0, 0)),
        ],
        out_specs=[
            pl.BlockSpec((1, blk, hpb * dv), lambda bi, h, s: (bi, s, h)),
            pl.BlockSpec((1, hpb, dk, dv), lambda bi, h, s: (bi, h, 0, 0)),
        ],
        out_shape=[jax.ShapeDtypeStruct((b, l, heads * dv), F32),
                   jax.ShapeDtypeStruct((b, heads, dk, dv), F32)],
        scratch_shapes=[pltpu.VMEM((hpb, dv, dk), F32)],
        compiler_params=_params(3, 32),
    )(proj, proj, proj, proj, lb_logits, out_g, s0)
    return o, s_new


def _logf_kernel(x_ref, g_ref, wft_ref, bft_ref, c0_ref, lf_ref, ft_ref, car_ref):
    tm = x_ref.shape[0]

    @pl.when(pl.program_id(1) == 0)
    def _():
        car_ref[...] = c0_ref[0]

    h = _rms(x_ref[...], g_ref[...])
    zt = _dot_nt(wft_ref[...], h, HIGHEST) + bft_ref[...]
    lft = _log_sigmoid(zt)
    lf_ref[0] = lft
    triu = (_iota((tm, tm), 0) <= _iota((tm, tm), 1)).astype(F32)
    ft = _dot(lft, triu, HIGHEST) + car_ref[...]
    ft_ref[0] = ft
    car_ref[...] = ft[:, tm - 1:tm]


def _logf(x, g, wft, bft, c0t, *, batch):
    m, d = x.shape
    l = m // batch
    hh = wft.shape[0]
    tm = _tile(l, 512)
    nt = l // tm
    out = jax.ShapeDtypeStruct((batch, hh, l), F32)
    return pl.pallas_call(
        _logf_kernel,
        grid=(batch, nt),
        in_specs=[
            pl.BlockSpec((tm, d), lambda b, i: (b * nt + i, 0)),
            pl.BlockSpec((1, d), lambda b, i: (0, 0)),
            pl.BlockSpec((hh, d), lambda b, i: (0, 0)),
            pl.BlockSpec((hh, 1), lambda b, i: (0, 0)),
            pl.BlockSpec((1, hh, 1), lambda b, i: (b, 0, 0)),
        ],
        out_specs=[pl.BlockSpec((1, hh, tm), lambda b, i: (b, 0, i)),
                   pl.BlockSpec((1, hh, tm), lambda b, i: (b, 0, i))],
        out_shape=[out, out],
        scratch_shapes=[pltpu.VMEM((hh, 1), F32)],
        compiler_params=_params(2, 32),
    )(x, g, wft, bft, c0t)


def _fox_prompt_kernel(q_ref, k_ref, v_ref, fq_ref, fk_ref, o_ref, m_ref, l_ref, acc_ref, *, tq, hpb):
    i = pl.program_id(2)
    hd = q_ref.shape[-1] // hpb
    n_lt = tq // LANES
    m_ref[...] = jnp.full_like(m_ref, MASK_VALUE)
    l_ref[...] = jnp.zeros_like(l_ref)
    acc_ref[...] = jnp.zeros_like(acc_ref)
    qs, fqs = [], []
    for h in range(hpb):
        qs.append((q_ref[:, h * hd:(h + 1) * hd].astype(F32) * (hd ** -0.5 * LOG2E)).astype(BF16))
        fqs.append(jnp.broadcast_to(fq_ref[0, h] * LOG2E, (tq, LANES)))

    def tile(j, masked):
        ks = pl.ds(pl.multiple_of(j * tq, tq), tq)
        for h in range(hpb):
            cs = slice(h * hd, (h + 1) * hd)
            s = _dot_nt(qs[h], k_ref[ks, cs])
            fk = fk_ref[0, h, :, ks] * LOG2E
            us = []
            for c in range(n_lt):
                ls = slice(c * LANES, (c + 1) * LANES)
                u = s[:, ls] - fk[:, ls]
                if masked:
                    u = jnp.where(_iota((tq, LANES), 0) >= c * LANES + _iota((tq, LANES), 1), u, MASK_VALUE)
                us.append(u)
            row_max = jnp.max(functools.reduce(jnp.maximum, us), axis=-1, keepdims=True)
            m_old = m_ref[h]
            m_new = jnp.maximum(m_old, jnp.broadcast_to(row_max, (tq, LANES)) + fqs[h])
            shift = fqs[h] - m_new
            ps = [jnp.exp2(u + shift) for u in us]
            alpha = jnp.exp2(m_old - m_new)
            m_ref[h] = m_new
            l_ref[h] = alpha * l_ref[h] + functools.reduce(jnp.add, ps)
            p_all = jnp.concatenate([p.astype(BF16) for p in ps], axis=1)
            acc_ref[h] = alpha * acc_ref[h] + _dot(p_all, v_ref[ks, cs])

    def body(j, carry):
        tile(j, False)
        return carry

    lax.fori_loop(0, i, body, 0)
    tile(i, True)
    for h in range(hpb):
        o_ref[:, h * hd:(h + 1) * hd] = acc_ref[h] / jnp.sum(l_ref[h], axis=-1, keepdims=True)


def _fox_prompt(q, k, v, ft, *, batch, heads):
    m, d = q.shape
    l = m // batch
    hd = d // heads
    assert hd == LANES
    hpb = ATTN_HEADS_PER_STEP
    tq = _tile(l, 512)
    nq = l // tq
    fcol = ft[..., None]
    frow = ft[:, :, None, :]
    return pl.pallas_call(
        functools.partial(_fox_prompt_kernel, tq=tq, hpb=hpb),
        grid=(batch, heads // hpb, nq),
        in_specs=[
            pl.BlockSpec((tq, hpb * hd), lambda b, h, i: (b * nq + i, h)),
            pl.BlockSpec((l, hpb * hd), lambda b, h, i: (b, h)),
            pl.BlockSpec((l, hpb * hd), lambda b, h, i: (b, h)),
            pl.BlockSpec((1, hpb, tq, 1), lambda b, h, i: (b, h, i, 0)),
            pl.BlockSpec((1, hpb, 1, l), lambda b, h, i: (b, h, 0, 0)),
        ],
        out_specs=pl.BlockSpec((tq, hpb * hd), lambda b, h, i: (b * nq + i, h)),
        out_shape=jax.ShapeDtypeStruct((m, d), F32),
        scratch_shapes=[pltpu.VMEM((hpb, tq, LANES), F32), pltpu.VMEM((hpb, tq, LANES), F32),
                        pltpu.VMEM((hpb, tq, hd), F32)],
        compiler_params=_params(3, 40),
    )(q, k, v, fcol, frow)


def _fpast_kernel(pt_ref, lf_ref, o_ref, car_ref, *, n_pg):
    b, g = pl.program_id(0), pl.program_id(1)
    n_grp, prow = lf_ref.shape[1], lf_ref.shape[2]
    rows = n_pg * prow
    keys_per_row = LANES // HEAD_GROUP

    @pl.when(g == 0)
    def _():
        car_ref[...] = jnp.zeros_like(car_ref)

    li, lj = _iota((LANES, LANES), 0), _iota((LANES, LANES), 1)
    same_head = (li % HEAD_GROUP) == (lj % HEAD_GROUP)
    prefix = (same_head & (li // HEAD_GROUP <= lj // HEAD_GROUP)).astype(F32)
    last = (same_head & (li // HEAD_GROUP == keys_per_row - 1)).astype(F32)
    strict = (_iota((rows, rows), 0) > _iota((rows, rows), 1)).astype(F32)
    for a in range(n_grp):
        x = jnp.concatenate([lf_ref[pt_ref[b, g * n_pg + o], a] for o in range(n_pg)], axis=0)
        y = _dot(x, prefix, HIGHEST)
        tot = _dot(y, last, HIGHEST)
        o_ref[0, a] = y + _dot(strict, tot, HIGHEST) + car_ref[a]
        car_ref[a] += jnp.sum(tot, axis=0, keepdims=True)


def _fpast(page_table, cache_logf_grp):
    b, n_pages = page_table.shape
    n_phys, n_grp, prow, _ = cache_logf_grp.shape
    n_pg = _tile(n_pages, FPAST_PAGES_PER_STEP)
    return pl.pallas_call(
        functools.partial(_fpast_kernel, n_pg=n_pg),
        grid_spec=pltpu.PrefetchScalarGridSpec(
            num_scalar_prefetch=1,
            grid=(b, n_pages // n_pg),
            in_specs=[pl.BlockSpec((n_phys, n_grp, prow, LANES), lambda bi, g, pt: (0, 0, 0, 0))],
            out_specs=pl.BlockSpec((1, n_grp, n_pg * prow, LANES), lambda bi, g, pt: (bi, 0, g, 0)),
            scratch_shapes=[pltpu.VMEM((n_grp, 1, LANES), F32)],
        ),
        out_shape=jax.ShapeDtypeStruct((b, n_grp, n_pages * prow, LANES), F32),
        compiler_params=_params(2, 40),
    )(page_table, cache_logf_grp)


def _fox_sample_phases(step, nsteps, q_ref, *refs, heads, n_pg):
    kc_refs, vc_refs = refs[:n_pg], refs[n_pg:2 * n_pg]
    fp_ref, kn_ref, vn_ref, fq_ref, fn_ref, o_ref, qa_ref, m_ref, l_ref, acc_ref = refs[2 * n_pg:]
    t, d = q_ref.shape
    hd = d // heads
    grp = HEAD_GROUP
    n_grp = heads // grp
    r = grp * t
    page = kc_refs[0].shape[1]
    prow = page * grp // LANES

    def pre():
        @pl.when(step == 0)
        def _():
            q = q_ref[...] * (hd ** -0.5 * LOG2E)
            for a in range(n_grp):
                qa_ref[a] = jnp.concatenate(
                    [q[:, (a * grp + j) * hd:(a * grp + j + 1) * hd] for j in range(grp)], axis=0)
            m_ref[...] = jnp.full_like(m_ref, MASK_VALUE)
            l_ref[...] = jnp.zeros_like(l_ref)
            acc_ref[...] = jnp.zeros_like(acc_ref)

    head_match = (_iota((r, LANES), 0) // t) == (_iota((r, LANES), 1) % grp)

    def update(a, k_list, v_list, fk_list, mask):
        fq = jnp.broadcast_to(fq_ref[0, a] * LOG2E, (r, LANES))
        qa = qa_ref[a]
        us = []
        for k_flat, fk in zip(k_list, fk_list):
            s = _dot_nt(qa, k_flat)
            for c in range(k_flat.shape[0] // LANES):
                u = s[:, c * LANES:(c + 1) * LANES] - fk[c:c + 1] * LOG2E
                us.append(jnp.where(mask, u, MASK_VALUE))
        row_max = jnp.max(functools.reduce(jnp.maximum, us), axis=-1, keepdims=True)
        m_old = m_ref[a]
        m_new = jnp.maximum(m_old, jnp.broadcast_to(row_max, (r, LANES)) + fq)
        shift = fq - m_new
        ps = [jnp.exp2(u + shift) for u in us]
        alpha = jnp.exp2(m_old - m_new)
        m_ref[a] = m_new
        l_ref[a] = alpha * l_ref[a] + functools.reduce(jnp.add, ps)
        pv, at = None, 0
        for v_flat in v_list:
            n_t = v_flat.shape[0] // LANES
            p_pg = jnp.concatenate(ps[at:at + n_t], axis=1) if n_t > 1 else ps[at]
            at += n_t
            part = _dot(p_pg, v_flat)
            pv = part if pv is None else pv + part
        acc_ref[a] = alpha * acc_ref[a] + pv

    def group_rows(ref, a):
        blk = ref[0, :, a * grp:(a + 1) * grp, :]
        return blk.reshape(blk.shape[0] * grp, hd)

    def main():
        for a in range(n_grp):
            update(a, [group_rows(kc, a) for kc in kc_refs], [group_rows(vc, a) for vc in vc_refs],
                   [fp_ref[0, a, o * prow:(o + 1) * prow] for o in range(n_pg)], head_match)

    def post():
        @pl.when(step == nsteps - 1)
        def _():
            lane = _iota((r, LANES), 1)
            new_mask = head_match & (lane // grp <= _iota((r, LANES), 0) % t) & (lane < t * grp)
            zeros = jnp.zeros((LANES - t * grp, hd), F32)
            for a in range(n_grp):
                kn = jnp.concatenate([group_rows(kn_ref, a), zeros], axis=0)
                vn = jnp.concatenate([group_rows(vn_ref, a), zeros], axis=0)
                update(a, [kn], [vn], [fn_ref[0, a]], new_mask)
                res = acc_ref[a] / jnp.sum(l_ref[a], axis=-1, keepdims=True)
                for j in range(grp):
                    h = a * grp + j
                    o_ref[:, h * hd:(h + 1) * hd] = res[j * t:(j + 1) * t]

    return pre, main, post


def _fox_sample(q, cache_k, cache_v, page_table, fpast, k_new, v_new, fq_col, fnew_flat):
    plan = _fox_sample_plan(q, cache_k, page_table, lambda bi, s: (bi, s))
    grid, n_pg, in_specs, out_spec, out_shape, scratch = plan
    heads = cache_k.shape[2]

    def body(pt_ref, *refs):
        for phase in _fox_sample_phases(pl.program_id(1), pl.num_programs(1), *refs, heads=heads, n_pg=n_pg):
            phase()

    return pl.pallas_call(
        body,
        grid_spec=pltpu.PrefetchScalarGridSpec(
            num_scalar_prefetch=1, grid=grid, in_specs=in_specs, out_specs=out_spec, scratch_shapes=scratch),
        out_shape=out_shape,
        compiler_params=_params(2, 56),
    )(page_table, q, *([cache_k] * n_pg), *([cache_v] * n_pg), fpast, k_new, v_new, fq_col, fnew_flat)


def _fox_sample_plan(q, cache_k, page_table, bs):
    b, n_pages = page_table.shape
    m, d = q.shape
    t = m // b
    _, page, heads, hd = cache_k.shape
    grp = HEAD_GROUP
    n_grp = heads // grp
    assert hd == LANES and heads % grp == 0 and t * grp <= LANES and (page * grp) % LANES == 0
    r = grp * t
    n_pg = _tile(n_pages, PAGES_PER_STEP)
    prow = page * grp // LANES
    seq = lambda g: bs(*g)[0]
    pgrp = lambda g: bs(*g)[1]
    page_spec = lambda o: pl.BlockSpec(
        (1, page, heads, hd), lambda *g: (g[-1][seq(g[:-1]), pgrp(g[:-1]) * n_pg + o], 0, 0, 0))
    new_spec = pl.BlockSpec((1, t, heads, hd), lambda *g: (seq(g[:-1]), 0, 0, 0))
    in_specs = ([pl.BlockSpec((t, d), lambda *g: (seq(g[:-1]), 0))]
                + [page_spec(o) for o in range(n_pg)] * 2
                + [pl.BlockSpec((1, n_grp, n_pg * prow, LANES), lambda *g: (seq(g[:-1]), 0, pgrp(g[:-1]), 0)),
                   new_spec, new_spec,
                   pl.BlockSpec((1, n_grp, r, 1), lambda *g: (seq(g[:-1]), 0, 0, 0)),
                   pl.BlockSpec((1, n_grp, 1, LANES), lambda *g: (seq(g[:-1]), 0, 0, 0))])
    out_spec = pl.BlockSpec((t, d), lambda *g: (seq(g[:-1]), 0))
    scratch = [pltpu.VMEM((n_grp, r, hd), F32), pltpu.VMEM((n_grp, r, LANES), F32),
               pltpu.VMEM((n_grp, r, LANES), F32), pltpu.VMEM((n_grp, r, hd), F32)]
    return (b, n_pages // n_pg), n_pg, in_specs, out_spec, jax.ShapeDtypeStruct((m, d), F32), scratch


def _trunk(x3, mods, s0, attend_ctx, wts):
    (norm_pre, norm_post, a_w_in, a_w_out, a_lb_logits, a_out_norm, kv_norm, w_kv, wft, b_forget,
     b_w_q, b_w_o, mlp_w_in, mlp_w_out) = wts
    b, l, d = x3.shape
    m = b * l
    depth = norm_pre.shape[0]
    n_a = a_w_out.shape[0]
    a_heads = s0.shape[2]
    pick = lambda w, layer: (w[layer], 0) if isinstance(w, list) else (w, layer)
    casts = {"a_w_in": [], "mlp_w_in": [], "mlp_w_out": []}
    b_heads = b_forget.shape[0]
    x = x3.reshape(m, d)
    if m <= 512:
        rows_per_mod = m
        expand = lambda t: jnp.repeat(t, l, axis=0)[None]
    else:
        rows_per_mod = l
        expand = lambda t: t[:, None, :]
    row = lambda t: t.reshape(1, -1)
    states = []
    shared = None
    for layer in range(depth):
        mod = expand(mods[layer, 0])
        if layer < n_a:
            w_in, w_idx = pick(a_w_in, layer)
            proj = _norm_mm(x, row(norm_pre[layer, 0]), mod, w_in, w_idx, rows_per_mod=rows_per_mod)
            if w_in.dtype != BF16:
                proj, w_in_b = proj
                casts["a_w_in"].append(w_in_b)
            o, s_new = _hgrn(proj.reshape(b, l, -1), a_lb_logits, row(a_out_norm[layer]), s0[layer],
                             layer=layer, heads=a_heads)
            states.append(s_new)
            y_in, w_o, w_idx = o.reshape(m, d), a_w_out, layer
        else:
            j = layer - n_a
            if shared is None:
                kvg = row(kv_norm)
                k, k16 = _norm_mm(x, kvg, None, w_kv, 0, rows_per_mod=rows_per_mod, n_cols=d, col_off=0,
                                  out_dtypes=(F32, BF16))
                v, v16 = _norm_mm(x, kvg, None, w_kv, 0, rows_per_mod=rows_per_mod, n_cols=d, col_off=d,
                                  out_dtypes=(F32, BF16))
                shared = attend_ctx["make"](x, kvg, wft, b_forget.reshape(b_heads, 1), k, v, k16, v16)
            q = _norm_mm(x, row(norm_pre[layer, 0]), mod, b_w_q, j, rows_per_mod=rows_per_mod,
                         out_dtypes=(attend_ctx["q_dtype"],))
            y_in, w_o, w_idx = attend_ctx["attend"](q, shared), b_w_o, j
        x = _mm_post(y_in, w_o, w_idx, x, mod, row(norm_post[layer, 0]), rows_per_mod=rows_per_mod)
        mod = expand(mods[layer, 1])
        w1, w_idx = pick(mlp_w_in, layer)
        w2, _ = pick(mlp_w_out, layer)
        x = _mlp(x, row(norm_pre[layer, 1]), mod, w1, w2, w_idx, row(norm_post[layer, 1]),
                 rows_per_mod=rows_per_mod)
        if w1.dtype != BF16:
            x, w1_b, w2_b = x
            casts["mlp_w_in"].append(w1_b)
            casts["mlp_w_out"].append(w2_b)
    k, v, lft = shared["rows"]
    kv_shape = (b, l, b_heads, d // b_heads)
    return (x.reshape(b, l, d), jnp.stack(states), k.reshape(kv_shape), v.reshape(kv_shape),
            jnp.swapaxes(lft, 1, 2), casts)


def kernel(x_prompt, x_sample, c_prompt, c_sample, state_hgrn, cache_k, cache_v, cache_logf, page_table,
           norm_pre, norm_post, w_ada, b_ada, a_w_in, a_w_out, a_lb_logits, a_out_norm,
           kv_norm, w_kv, b_forget, b_w_q, b_w_o, mlp_w_in, mlp_w_out):
    bp, _, d = x_prompt.shape
    bs, ts, _ = x_sample.shape
    depth = norm_pre.shape[0]
    n_phys, page, heads, hd = cache_k.shape
    wts_s = (norm_pre, norm_post, a_w_in, a_w_out.astype(BF16), a_lb_logits, a_out_norm, kv_norm,
             w_kv.astype(BF16)[None], w_kv[:, 2 * d:].T, b_forget,
             b_w_q.astype(BF16), b_w_o.astype(BF16), mlp_w_in, mlp_w_out)

    n_c = bp + bs
    c_rows = -(-n_c // 8) * 8
    c_all = jnp.concatenate([c_prompt, c_sample, jnp.zeros((c_rows - n_c, d), F32)], axis=0)
    mods = _ada_all(c_all, w_ada.reshape(depth * 2, d, 3 * d), b_ada.reshape(depth * 2, 1, 3 * d))
    mods = mods.reshape(depth, 2, c_rows, 3 * d)
    mods_p, mods_s = mods[:, :, :bp], mods[:, :, bp:n_c]

    grp = HEAD_GROUP
    n_grp = heads // grp
    prow = page * grp // LANES
    lf_grp = cache_logf.astype(F32).reshape(n_phys, page, n_grp, grp).transpose(0, 2, 1, 3)
    fpast = _fpast(page_table, lf_grp.reshape(n_phys, n_grp, prow, LANES))
    f_total = fpast[:, :, -1, LANES - grp:].reshape(bs, heads, 1)

    def make_sample(x, kvg, wft, bft, k, v, k16, v16):
        lft, fnew_t = _logf(x, kvg, wft, bft, f_total, batch=bs)
        fq_col = fnew_t.reshape(bs, n_grp, grp * ts, 1)
        fnew_flat = fnew_t.reshape(bs, n_grp, grp, ts).transpose(0, 1, 3, 2).reshape(bs, n_grp, 1, ts * grp)
        fnew_flat = jnp.pad(fnew_flat, ((0, 0), (0, 0), (0, 0), (0, LANES - ts * grp)))
        shape4 = (bs, ts, heads, hd)
        return {"rows": (k, v, lft), "k": k.reshape(shape4), "v": v.reshape(shape4), "fq": fq_col, "fn": fnew_flat}

    def attend_sample(q, sh):
        return _fox_sample(q, cache_k, cache_v, page_table, fpast, sh["k"], sh["v"], sh["fq"], sh["fn"])

    y_s, st_s, k_s, v_s, lf_s, casts = _trunk(
        x_sample, mods_s, state_hgrn, {"make": make_sample, "attend": attend_sample, "q_dtype": F32}, wts_s)

    def make_prompt(x, kvg, wft, bft, k, v, k16, v16):
        lft, ft = _logf(x, kvg, wft, bft, jnp.zeros((bp, heads, 1), F32), batch=bp)
        return {"rows": (k, v, lft), "k": k16, "v": v16, "ft": ft}

    def attend_prompt(q, sh):
        return _fox_prompt(q, sh["k"], sh["v"], sh["ft"], batch=bp, heads=heads)

    wts_p = wts_s[:2] + (casts["a_w_in"],) + wts_s[3:12] + (casts["mlp_w_in"], casts["mlp_w_out"])
    s0_prompt = jnp.zeros((a_w_out.shape[0], bp) + state_hgrn.shape[2:], F32)
    y_p, st_p, k_p, v_p, lf_p, _ = _trunk(
        x_prompt, mods_p, s0_prompt, {"make": make_prompt, "attend": attend_prompt, "q_dtype": BF16}, wts_p)

    return (y_p, y_s, st_p, st_s, k_p, v_p, lf_p, k_s, v_s, lf_s)
```

```python
import functools
import math

import jax
import jax.numpy as jnp
from jax import lax
from jax.experimental import pallas as pl
from jax.experimental.pallas import tpu as pltpu

EPS = 1e-6
MASK_VALUE = -1e30
TINY = 1e-30
LOG2E = math.log2(math.e)
F32 = jnp.float32
BF16 = jnp.bfloat16
HIGHEST = lax.Precision.HIGHEST

LANES = 128
HGRN_CHUNK = 128
HGRN_SUB = 8
SUBLANES = 8
HEAD_GROUP = SUBLANES
PAGES_PER_STEP = 8
FPAST_PAGES_PER_STEP = 16
HGRN_HEADS_PER_STEP = 8
ATTN_HEADS_PER_STEP = 8
MIB = 2 ** 20


def _params(n_axes, vmem_mib):
    return pltpu.CompilerParams(
        dimension_semantics=("arbitrary",) * n_axes, vmem_limit_bytes=vmem_mib * MIB)


def _tile(n, pref):
    t = min(n, pref)
    while n % t:
        t -= 1
    return t


def _sigmoid(x):
    return 1.0 / (1.0 + jnp.exp(-x))


def _log_sigmoid(x):
    return jnp.minimum(x, 0.0) - jnp.log1p(jnp.exp(-jnp.abs(x)))


def _rms(x, g):
    return x * lax.rsqrt(jnp.mean(x * x, axis=-1, keepdims=True) + EPS) * g


def _dot(a, b, precision=None):
    return jnp.dot(a, b, preferred_element_type=F32, precision=precision)


def _dot_nt(a, b, precision=None):
    return lax.dot_general(a, b, (((1,), (1,)), ((), ())), preferred_element_type=F32, precision=precision)


def _iota(shape, dim):
    return lax.broadcasted_iota(jnp.int32, shape, dim)


def _ada_kernel(c_ref, w_ref, b_ref, o_ref):
    c = c_ref[...]
    s = (c * _sigmoid(c)).astype(BF16)
    o_ref[0] = _dot(s, w_ref[0].astype(BF16)) + b_ref[0]


def _ada_all(c_all, w_ada, b_ada):
    n, d, n3 = w_ada.shape
    r = c_all.shape[0]
    tn = _tile(n3, 1024)
    return pl.pallas_call(
        _ada_kernel,
        grid=(n, n3 // tn),
        in_specs=[
            pl.BlockSpec((r, d), lambda i, j: (0, 0)),
            pl.BlockSpec((1, d, tn), lambda i, j: (i, 0, j)),
            pl.BlockSpec((1, 1, tn), lambda i, j: (i, 0, j)),
        ],
        out_specs=pl.BlockSpec((1, r, tn), lambda i, j: (i, 0, j)),
        out_shape=jax.ShapeDtypeStruct((n, r, n3), F32),
        compiler_params=_params(2, 40),
    )(c_all, w_ada, b_ada)


def _norm_mm_kernel(*refs, has_mod, n_out, emit):
    if has_mod:
        x_ref, g_ref, sh_ref, sc_ref, w_ref = refs[:5]
        rest = refs[5:]
    else:
        x_ref, g_ref, w_ref = refs[:3]
        rest = refs[3:]
    o_refs, h_ref = rest[:n_out], rest[-1]

    @pl.when(pl.program_id(1) == 0)
    def _():
        h = _rms(x_ref[...], g_ref[...])
        if has_mod:
            h = h * (1.0 + sc_ref[0]) + sh_ref[0]
        h_ref[...] = h.astype(BF16)

    w = w_ref[0].astype(BF16)
    if emit:
        rest[n_out][0] = w
    y = _dot(h_ref[...], w)
    for o_ref in o_refs:
        o_ref[...] = y.astype(o_ref.dtype)


def _norm_mm(x, g, mod, w, layer, *, rows_per_mod, n_cols=None, col_off=0, out_dtypes=(F32,),
             tm_pref=1024, tn_pref=1024):
    m, d = x.shape
    n_cols = w.shape[2] if n_cols is None else n_cols
    tm = _tile(m, tm_pref)
    tn = _tile(n_cols, tn_pref)
    assert col_off % tn == 0
    joff = col_off // tn
    emit = w.dtype != BF16
    assert not emit or (col_off == 0 and n_cols == w.shape[2])
    in_specs = [pl.BlockSpec((tm, d), lambda i, j: (i, 0)), pl.BlockSpec((1, d), lambda i, j: (0, 0))]
    args = [x, g]
    if mod is not None:
        r = mod.shape[1]
        assert rows_per_mod % tm == 0 and r in (1, tm)
        bidx = lambda i: (i * tm) // rows_per_mod
        in_specs += [pl.BlockSpec((1, r, d), lambda i, j: (bidx(i), 0, 0)),
                     pl.BlockSpec((1, r, d), lambda i, j: (bidx(i), 0, 1))]
        args += [mod, mod]
    in_specs.append(pl.BlockSpec((1, d, tn), lambda i, j: (layer, 0, j + joff)))
    args.append(w)
    out_specs = [pl.BlockSpec((tm, tn), lambda i, j: (i, j)) for _ in out_dtypes]
    out_shape = [jax.ShapeDtypeStruct((m, n_cols), dt) for dt in out_dtypes]
    if emit:
        out_specs.append(pl.BlockSpec((1, d, tn), lambda i, j: (0, 0, j)))
        out_shape.append(jax.ShapeDtypeStruct((1, d, n_cols), BF16))
    outs = pl.pallas_call(
        functools.partial(_norm_mm_kernel, has_mod=mod is not None, n_out=len(out_dtypes), emit=emit),
        grid=(m // tm, n_cols // tn),
        in_specs=in_specs,
        out_specs=out_specs,
        out_shape=out_shape,
        scratch_shapes=[pltpu.VMEM((tm, d), BF16)],
        compiler_params=_params(2, 56),
    )(*args)
    return outs if len(outs) > 1 else outs[0]


def _mm_post_kernel(a_ref, w_ref, x_ref, gate_ref, g_ref, o_ref):
    y = _dot(a_ref[...].astype(BF16), w_ref[...])
    o_ref[...] = x_ref[...] + gate_ref[0] * _rms(y, g_ref[...])


def _mm_post(a, w, layer, x, mod, g, *, rows_per_mod, tm_pref=512):
    m, kdim = a.shape
    d = w.shape[2]
    tm = _tile(m, tm_pref)
    r = mod.shape[1]
    assert rows_per_mod % tm == 0 and r in (1, tm)
    bidx = lambda i: (i * tm) // rows_per_mod
    return pl.pallas_call(
        _mm_post_kernel,
        grid=(m // tm,),
        in_specs=[
            pl.BlockSpec((tm, kdim), lambda i: (i, 0)),
            pl.BlockSpec((None, kdim, d), lambda i: (layer, 0, 0)),
            pl.BlockSpec((tm, d), lambda i: (i, 0)),
            pl.BlockSpec((1, r, d), lambda i: (bidx(i), 0, 2)),
            pl.BlockSpec((1, d), lambda i: (0, 0)),
        ],
        out_specs=pl.BlockSpec((tm, d), lambda i: (i, 0)),
        out_shape=jax.ShapeDtypeStruct((m, d), F32),
        compiler_params=_params(1, 56),
    )(a, w, x, mod, g)


def _mlp_kernel(*refs, emit):
    x_ref, gpre_ref, sh_ref, sc_ref, gate_ref, w1_ref, w2_ref, gpost_ref, o_ref = refs[:9]
    h_ref, acc_ref = refs[-2:]
    f = pl.program_id(1)

    @pl.when(f == 0)
    def _():
        h = _rms(x_ref[...], gpre_ref[...]) * (1.0 + sc_ref[0]) + sh_ref[0]
        h_ref[...] = h.astype(BF16)
        acc_ref[...] = jnp.zeros_like(acc_ref)

    w1 = w1_ref[0].astype(BF16)
    w2 = w2_ref[0].astype(BF16)
    if emit:
        w1b_ref, w2b_ref = refs[9:11]
        w1b_ref[0] = w1
        w2b_ref[0] = w2
    u = jnp.maximum(_dot(h_ref[...], w1), 0.0)
    acc_ref[...] += _dot((u * u).astype(BF16), w2)

    @pl.when(f == pl.num_programs(1) - 1)
    def _():
        o_ref[...] = x_ref[...] + gate_ref[0] * _rms(acc_ref[...], gpost_ref[...])


def _mlp(x, gpre, mod, w1, w2, layer, gpost, *, rows_per_mod, tm_pref=512, tf_pref=1024):
    m, d = x.shape
    ff = w1.shape[2]
    tm = _tile(m, tm_pref)
    tf = _tile(ff, tf_pref)
    r = mod.shape[1]
    assert rows_per_mod % tm == 0 and r in (1, tm)
    emit = w1.dtype != BF16
    bidx = lambda i: (i * tm) // rows_per_mod
    out_specs = [pl.BlockSpec((tm, d), lambda i, f: (i, 0))]
    out_shape = [jax.ShapeDtypeStruct((m, d), F32)]
    if emit:
        out_specs += [pl.BlockSpec((1, d, tf), lambda i, f: (0, 0, f)),
                      pl.BlockSpec((1, tf, d), lambda i, f: (0, f, 0))]
        out_shape += [jax.ShapeDtypeStruct((1, d, ff), BF16), jax.ShapeDtypeStruct((1, ff, d), BF16)]
    outs = pl.pallas_call(
        functools.partial(_mlp_kernel, emit=emit),
        grid=(m // tm, ff // tf),
        in_specs=[
            pl.BlockSpec((tm, d), lambda i, f: (i, 0)),
            pl.BlockSpec((1, d), lambda i, f: (0, 0)),
            pl.BlockSpec((1, r, d), lambda i, f: (bidx(i), 0, 0)),
            pl.BlockSpec((1, r, d), lambda i, f: (bidx(i), 0, 1)),
            pl.BlockSpec((1, r, d), lambda i, f: (bidx(i), 0, 2)),
            pl.BlockSpec((1, d, tf), lambda i, f: (layer, 0, f)),
            pl.BlockSpec((1, tf, d), lambda i, f: (layer, f, 0)),
            pl.BlockSpec((1, d), lambda i, f: (0, 0)),
        ],
        out_specs=out_specs,
        out_shape=out_shape,
        scratch_shapes=[pltpu.VMEM((tm, d), BF16), pltpu.VMEM((tm, d), F32)],
        compiler_params=_params(2, 56),
    )(x, gpre, mod, mod, mod, w1, w2, gpost)
    return outs if emit else outs[0]


def _hgrn_kernel(q_ref, f_ref, i_ref, g_ref, lbl_ref, og_ref, s0_ref, o_ref, s_ref, st_ref,
                 *, layer, rows, n_inner, hpb):
    c = HGRN_CHUNK
    sub = HGRN_SUB
    dk = q_ref.shape[-1] // hpb
    step = pl.program_id(2)

    @pl.when(step == 0)
    def _():
        for h in range(hpb):
            st_ref[h] = s0_ref[0, h].T

    z = lbl_ref[...]
    e = jnp.exp(z - jnp.max(z, axis=0, keepdims=True))
    p = e / jnp.sum(e, axis=0, keepdims=True)
    cs = p[0:1]
    for r in range(1, layer + 1):
        cs = cs + p[r:r + 1]
    lb_all = cs - p[0:1]
    og_all = og_ref[...]

    tril = (_iota((c, c), 0) >= _iota((c, c), 1)).astype(BF16)
    lane = _iota((sub, c), 1)
    sub_row = _iota((sub, c), 0)
    scale = dk ** -0.5

    def pad(t, fill=0.0):
        if rows == c:
            return t
        return jnp.concatenate([t, jnp.full((c - rows, t.shape[1]), fill, t.dtype)], axis=0)

    def cumsum_rows(x):
        hi = x.astype(BF16)
        r1 = x - hi.astype(F32)
        mid = r1.astype(BF16)
        lo = (r1 - mid.astype(F32)).astype(BF16)
        return _dot(tril, hi) + _dot(tril, mid) + _dot(tril, lo)

    def head_chunk(h, sl):
        hs = slice(h * dk, (h + 1) * dk)
        lb = lb_all[:, hs]
        lb_floor = jnp.maximum(lb, TINY)
        one_m_lb = 1.0 - lb
        log2_one_m_lb = jnp.log(one_m_lb) * LOG2E
        qz = q_ref[0, sl, hs]
        fz = f_ref[0, sl, hs]
        q = pad(qz * _sigmoid(qz))
        e = jnp.exp(-jnp.abs(fz))
        t = 1.0 + e
        r = 1.0 / t
        sig = jnp.where(fz >= 0.0, r, e * r)
        logf = pad(jnp.log(lb_floor + one_m_lb * sig))
        lk = pad(log2_one_m_lb + (jnp.minimum(-fz, 0.0) - jnp.log(t)) * LOG2E, -jnp.inf)
        v = pad(i_ref[0, sl, hs])
        b2 = cumsum_rows(logf) * LOG2E
        c2 = b2 - lk
        st = st_ref[h]
        inter = _dot_nt((q * jnp.exp2(b2)).astype(BF16), st.astype(BF16))
        blocks = []
        for i in range(c // sub):
            lo = i * sub
            qi, bi, ci = q[lo:lo + sub], b2[lo:lo + sub], c2[lo:lo + sub]
            if i > 0:
                ref_b = bi[0:1]
                qs = (qi * jnp.exp2(bi - ref_b)).astype(BF16)
                kt = jnp.exp2(ref_b - c2[:lo]).astype(BF16)
                kt = jnp.concatenate([kt, jnp.zeros((c - lo, dk), BF16)], axis=0)
                s_i = _dot_nt(qs, kt)
            else:
                s_i = jnp.zeros((sub, c), F32)
            for s in range(sub):
                w = jnp.sum(qi * jnp.exp2(bi - ci[s:s + 1]), axis=-1, keepdims=True)
                s_i = jnp.where(lane == lo + s, w, s_i)
            blocks.append(jnp.where(lane <= lo + sub_row, s_i, 0.0))
        scores = jnp.concatenate(blocks, axis=0).astype(BF16)
        o = (inter + _dot(scores, v.astype(BF16))) * scale
        blast = b2[c - 1:c]
        kd = jnp.exp2(blast - c2).astype(BF16)
        st_ref[h] = st * jnp.exp2(blast) + _dot(v.T.astype(BF16), kd)
        gz = g_ref[0, sl, hs]
        o_ref[0, sl, hs] = _rms(o[:rows], og_all[:, hs]) * (gz * _sigmoid(gz))

    def chunk(j, carry):
        if n_inner == 1:
            sl = pl.ds(0, rows)
        else:
            sl = pl.ds(pl.multiple_of(j * c, c), c)
        for h in range(hpb):
            head_chunk(h, sl)
        return carry

    lax.fori_loop(0, n_inner, chunk, 0)

    @pl.when(step == pl.num_programs(2) - 1)
    def _():
        for h in range(hpb):
            s_ref[0, h] = st_ref[h].T


def _hgrn(proj, lb_logits, out_g, s0, *, layer, heads):
    b, l, _ = proj.shape
    dk = s0.shape[2]
    dv = s0.shape[3]
    assert dk == LANES and dv == LANES
    c = HGRN_CHUNK
    if l >= c:
        assert l % c == 0
        rows = c
        n_inner = _tile(l // c, 4)
        blk = rows * n_inner
    else:
        assert l % 8 == 0
        rows, n_inner, blk = l, 1, l
    nl = lb_logits.shape[0]
    hpb = _tile(heads, HGRN_HEADS_PER_STEP)
    ng = heads // hpb
    col = lambda part: pl.BlockSpec((1, blk, hpb * dk), lambda bi, h, s: (bi, s, part * ng + h))
    o, s_new = pl.pallas_call(
        functools.partial(_hgrn_kernel, layer=layer, rows=rows, n_inner=n_inner, hpb=hpb),
        grid=(b, ng, l // blk),
        in_specs=[
            col(0), col(1), col(2), col(3),
            pl.BlockSpec((nl, hpb * dk), lambda bi, h, s: (0, h)),
            pl.BlockSpec((1, hpb * dv), lambda bi, h, s: (0, h)),
            pl.BlockSpec((1, hpb, dk, dv), lambda bi, h, s: (bi, h, 0, 0)),
        ],
        out_specs=[
            pl.BlockSpec((1, blk, hpb * dv), lambda bi, h, s: (bi, s, h)),
            pl.BlockSpec((1, hpb, dk, dv), lambda bi, h, s: (bi, h, 0, 0)),
        ],
        out_shape=[jax.ShapeDtypeStruct((b, l, heads * dv), F32),
                   jax.ShapeDtypeStruct((b, heads, dk, dv), F32)],
        scratch_shapes=[pltpu.VMEM((hpb, dv, dk), F32)],
        compiler_params=_params(3, 32),
    )(proj, proj, proj, proj, lb_logits, out_g, s0)
    return o, s_new


def _logf_kernel(x_ref, g_ref, wft_ref, bft_ref, c0_ref, lf_ref, ft_ref, car_ref):
    tm = x_ref.shape[0]

    @pl.when(pl.program_id(1) == 0)
    def _():
        car_ref[...] = c0_ref[0]

    h = _rms(x_ref[...], g_ref[...])
    zt = _dot_nt(wft_ref[...], h, HIGHEST) + bft_ref[...]
    lft = _log_sigmoid(zt)
    lf_ref[0] = lft
    triu = (_iota((tm, tm), 0) <= _iota((tm, tm), 1)).astype(F32)
    ft = _dot(lft, triu, HIGHEST) + car_ref[...]
    ft_ref[0] = ft
    car_ref[...] = ft[:, tm - 1:tm]


def _logf(x, g, wft, bft, c0t, *, batch):
    m, d = x.shape
    l = m // batch
    hh = wft.shape[0]
    tm = _tile(l, 512)
    nt = l // tm
    out = jax.ShapeDtypeStruct((batch, hh, l), F32)
    return pl.pallas_call(
        _logf_kernel,
        grid=(batch, nt),
        in_specs=[
            pl.BlockSpec((tm, d), lambda b, i: (b * nt + i, 0)),
            pl.BlockSpec((1, d), lambda b, i: (0, 0)),
            pl.BlockSpec((hh, d), lambda b, i: (0, 0)),
            pl.BlockSpec((hh, 1), lambda b, i: (0, 0)),
            pl.BlockSpec((1, hh, 1), lambda b, i: (b, 0, 0)),
        ],
        out_specs=[pl.BlockSpec((1, hh, tm), lambda b, i: (b, 0, i)),
                   pl.BlockSpec((1, hh, tm), lambda b, i: (b, 0, i))],
        out_shape=[out, out],
        scratch_shapes=[pltpu.VMEM((hh, 1), F32)],
        compiler_params=_params(2, 32),
    )(x, g, wft, bft, c0t)


def _fox_prompt_kernel(q_ref, k_ref, v_ref, fq_ref, fk_ref, o_ref, m_ref, l_ref, acc_ref, *, tq, hpb):
    i = pl.program_id(2)
    hd = q_ref.shape[-1] // hpb
    n_lt = tq // LANES
    m_ref[...] = jnp.full_like(m_ref, MASK_VALUE)
    l_ref[...] = jnp.zeros_like(l_ref)
    acc_ref[...] = jnp.zeros_like(acc_ref)
    qs, fqs = [], []
    for h in range(hpb):
        qs.append((q_ref[:, h * hd:(h + 1) * hd].astype(F32) * (hd ** -0.5 * LOG2E)).astype(BF16))
        fqs.append(jnp.broadcast_to(fq_ref[0, h] * LOG2E, (tq, LANES)))

    def tile(j, masked):
        ks = pl.ds(pl.multiple_of(j * tq, tq), tq)
        for h in range(hpb):
            cs = slice(h * hd, (h + 1) * hd)
            s = _dot_nt(qs[h], k_ref[ks, cs])
            fk = fk_ref[0, h, :, ks] * LOG2E
            us = []
            for c in range(n_lt):
                ls = slice(c * LANES, (c + 1) * LANES)
                u = s[:, ls] - fk[:, ls]
                if masked:
                    u = jnp.where(_iota((tq, LANES), 0) >= c * LANES + _iota((tq, LANES), 1), u, MASK_VALUE)
                us.append(u)
            row_max = jnp.max(functools.reduce(jnp.maximum, us), axis=-1, keepdims=True)
            m_old = m_ref[h]
            m_new = jnp.maximum(m_old, jnp.broadcast_to(row_max, (tq, LANES)) + fqs[h])
            shift = fqs[h] - m_new
            ps = [jnp.exp2(u + shift) for u in us]
            alpha = jnp.exp2(m_old - m_new)
            m_ref[h] = m_new
            l_ref[h] = alpha * l_ref[h] + functools.reduce(jnp.add, ps)
            p_all = jnp.concatenate([p.astype(BF16) for p in ps], axis=1)
            acc_ref[h] = alpha * acc_ref[h] + _dot(p_all, v_ref[ks, cs])

    def body(j, carry):
        tile(j, False)
        return carry

    lax.fori_loop(0, i, body, 0)
    tile(i, True)
    for h in range(hpb):
        o_ref[:, h * hd:(h + 1) * hd] = acc_ref[h] / jnp.sum(l_ref[h], axis=-1, keepdims=True)


def _fox_prompt(q, k, v, ft, *, batch, heads):
    m, d = q.shape
    l = m // batch
    hd = d // heads
    assert hd == LANES
    hpb = ATTN_HEADS_PER_STEP
    tq = _tile(l, 512)
    nq = l // tq
    fcol = ft[..., None]
    frow = ft[:, :, None, :]
    return pl.pallas_call(
        functools.partial(_fox_prompt_kernel, tq=tq, hpb=hpb),
        grid=(batch, heads // hpb, nq),
        in_specs=[
            pl.BlockSpec((tq, hpb * hd), lambda b, h, i: (b * nq + i, h)),
            pl.BlockSpec((l, hpb * hd), lambda b, h, i: (b, h)),
            pl.BlockSpec((l, hpb * hd), lambda b, h, i: (b, h)),
            pl.BlockSpec((1, hpb, tq, 1), lambda b, h, i: (b, h, i, 0)),
            pl.BlockSpec((1, hpb, 1, l), lambda b, h, i: (b, h, 0, 0)),
        ],
        out_specs=pl.BlockSpec((tq, hpb * hd), lambda b, h, i: (b * nq + i, h)),
        out_shape=jax.ShapeDtypeStruct((m, d), F32),
        scratch_shapes=[pltpu.VMEM((hpb, tq, LANES), F32), pltpu.VMEM((hpb, tq, LANES), F32),
                        pltpu.VMEM((hpb, tq, hd), F32)],
        compiler_params=_params(3, 56),
    )(q, k, v, fcol, frow)


def _fpast_kernel(pt_ref, lf_ref, o_ref, car_ref, *, n_pg):
    b, g = pl.program_id(0), pl.program_id(1)
    n_grp, prow = lf_ref.shape[1], lf_ref.shape[2]
    rows = n_pg * prow
    keys_per_row = LANES // HEAD_GROUP

    @pl.when(g == 0)
    def _():
        car_ref[...] = jnp.zeros_like(car_ref)

    li, lj = _iota((LANES, LANES), 0), _iota((LANES, LANES), 1)
    same_head = (li % HEAD_GROUP) == (lj % HEAD_GROUP)
    prefix = (same_head & (li // HEAD_GROUP <= lj // HEAD_GROUP)).astype(F32)
    last = (same_head & (li // HEAD_GROUP == keys_per_row - 1)).astype(F32)
    strict = (_iota((rows, rows), 0) > _iota((rows, rows), 1)).astype(F32)
    for a in range(n_grp):
        x = jnp.concatenate([lf_ref[pt_ref[b, g * n_pg + o], a] for o in range(n_pg)], axis=0)
        y = _dot(x, prefix, HIGHEST)
        tot = _dot(y, last, HIGHEST)
        o_ref[0, a] = y + _dot(strict, tot, HIGHEST) + car_ref[a]
        car_ref[a] += jnp.sum(tot, axis=0, keepdims=True)


def _fpast(page_table, cache_logf_grp):
    b, n_pages = page_table.shape
    n_phys, n_grp, prow, _ = cache_logf_grp.shape
    n_pg = _tile(n_pages, FPAST_PAGES_PER_STEP)
    return pl.pallas_call(
        functools.partial(_fpast_kernel, n_pg=n_pg),
        grid_spec=pltpu.PrefetchScalarGridSpec(
            num_scalar_prefetch=1,
            grid=(b, n_pages // n_pg),
            in_specs=[pl.BlockSpec((n_phys, n_grp, prow, LANES), lambda bi, g, pt: (0, 0, 0, 0))],
            out_specs=pl.BlockSpec((1, n_grp, n_pg * prow, LANES), lambda bi, g, pt: (bi, 0, g, 0)),
            scratch_shapes=[pltpu.VMEM((n_grp, 1, LANES), F32)],
        ),
        out_shape=jax.ShapeDtypeStruct((b, n_grp, n_pages * prow, LANES), F32),
        compiler_params=_params(2, 40),
    )(page_table, cache_logf_grp)


def _fox_sample_phases(step, nsteps, q_ref, *refs, heads, n_pg):
    kc_refs, vc_refs = refs[:n_pg], refs[n_pg:2 * n_pg]
    fp_ref, kn_ref, vn_ref, fq_ref, fn_ref, o_ref, qa_ref, m_ref, l_ref, acc_ref = refs[2 * n_pg:]
    t, d = q_ref.shape
    hd = d // heads
    grp = HEAD_GROUP
    n_grp = heads // grp
    r = grp * t
    page = kc_refs[0].shape[1]
    prow = page * grp // LANES

    def pre():
        @pl.when(step == 0)
        def _():
            q = q_ref[...] * (hd ** -0.5 * LOG2E)
            for a in range(n_grp):
                qa_ref[a] = jnp.concatenate(
                    [q[:, (a * grp + j) * hd:(a * grp + j + 1) * hd] for j in range(grp)], axis=0)
            m_ref[...] = jnp.full_like(m_ref, MASK_VALUE)
            l_ref[...] = jnp.zeros_like(l_ref)
            acc_ref[...] = jnp.zeros_like(acc_ref)

    head_match = (_iota((r, LANES), 0) // t) == (_iota((r, LANES), 1) % grp)

    def update(a, k_list, v_list, fk_list, mask):
        fq = jnp.broadcast_to(fq_ref[0, a] * LOG2E, (r, LANES))
        qa = qa_ref[a]
        us = []
        for k_flat, fk in zip(k_list, fk_list):
            s = _dot_nt(qa, k_flat)
            for c in range(k_flat.shape[0] // LANES):
                u = s[:, c * LANES:(c + 1) * LANES] - fk[c:c + 1] * LOG2E
                us.append(jnp.where(mask, u, MASK_VALUE))
        row_max = jnp.max(functools.reduce(jnp.maximum, us), axis=-1, keepdims=True)
        m_old = m_ref[a]
        m_new = jnp.maximum(m_old, jnp.broadcast_to(row_max, (r, LANES)) + fq)
        shift = fq - m_new
        ps = [jnp.exp2(u + shift) for u in us]
        alpha = jnp.exp2(m_old - m_new)
        m_ref[a] = m_new
        l_ref[a] = alpha * l_ref[a] + functools.reduce(jnp.add, ps)
        pv, at = None, 0
        for v_flat in v_list:
            n_t = v_flat.shape[0] // LANES
            p_pg = jnp.concatenate(ps[at:at + n_t], axis=1) if n_t > 1 else ps[at]
            at += n_t
            part = _dot(p_pg, v_flat)
            pv = part if pv is None else pv + part
        acc_ref[a] = alpha * acc_ref[a] + pv

    def group_rows(ref, a):
        blk = ref[0, :, a * grp:(a + 1) * grp, :]
        return blk.reshape(blk.shape[0] * grp, hd)

    def main():
        for a in range(n_grp):
            update(a, [group_rows(kc, a) for kc in kc_refs], [group_rows(vc, a) for vc in vc_refs],
                   [fp_ref[0, a, o * prow:(o + 1) * prow] for o in range(n_pg)], head_match)

    def post():
        @pl.when(step == nsteps - 1)
        def _():
            lane = _iota((r, LANES), 1)
            new_mask = head_match & (lane // grp <= _iota((r, LANES), 0) % t) & (lane < t * grp)
            zeros = jnp.zeros((LANES - t * grp, hd), F32)
            for a in range(n_grp):
                kn = jnp.concatenate([group_rows(kn_ref, a), zeros], axis=0)
                vn = jnp.concatenate([group_rows(vn_ref, a), zeros], axis=0)
                update(a, [kn], [vn], [fn_ref[0, a]], new_mask)
                res = acc_ref[a] / jnp.sum(l_ref[a], axis=-1, keepdims=True)
                for j in range(grp):
                    h = a * grp + j
                    o_ref[:, h * hd:(h + 1) * hd] = res[j * t:(j + 1) * t]

    return pre, main, post


def _fox_sample(q, cache_k, cache_v, page_table, fpast, k_new, v_new, fq_col, fnew_flat):
    plan = _fox_sample_plan(q, cache_k, page_table, lambda bi, s: (bi, s))
    grid, n_pg, in_specs, out_spec, out_shape, scratch = plan
    heads = cache_k.shape[2]

    def body(pt_ref, *refs):
        for phase in _fox_sample_phases(pl.program_id(1), pl.num_programs(1), *refs, heads=heads, n_pg=n_pg):
            phase()

    return pl.pallas_call(
        body,
        grid_spec=pltpu.PrefetchScalarGridSpec(
            num_scalar_prefetch=1, grid=grid, in_specs=in_specs, out_specs=out_spec, scratch_shapes=scratch),
        out_shape=out_shape,
        compiler_params=_params(2, 56),
    )(page_table, q, *([cache_k] * n_pg), *([cache_v] * n_pg), fpast, k_new, v_new, fq_col, fnew_flat)


def _fox_sample_plan(q, cache_k, page_table, bs):
    b, n_pages = page_table.shape
    m, d = q.shape
    t = m // b
    _, page, heads, hd = cache_k.shape
    grp = HEAD_GROUP
    n_grp = heads // grp
    assert hd == LANES and heads % grp == 0 and t * grp <= LANES and (page * grp) % LANES == 0
    r = grp * t
    n_pg = _tile(n_pages, PAGES_PER_STEP)
    prow = page * grp // LANES
    seq = lambda g: bs(*g)[0]
    pgrp = lambda g: bs(*g)[1]
    page_spec = lambda o: pl.BlockSpec(
        (1, page, heads, hd), lambda *g: (g[-1][seq(g[:-1]), pgrp(g[:-1]) * n_pg + o], 0, 0, 0))
    new_spec = pl.BlockSpec((1, t, heads, hd), lambda *g: (seq(g[:-1]), 0, 0, 0))
    in_specs = ([pl.BlockSpec((t, d), lambda *g: (seq(g[:-1]), 0))]
                + [page_spec(o) for o in range(n_pg)] * 2
                + [pl.BlockSpec((1, n_grp, n_pg * prow, LANES), lambda *g: (seq(g[:-1]), 0, pgrp(g[:-1]), 0)),
                   new_spec, new_spec,
                   pl.BlockSpec((1, n_grp, r, 1), lambda *g: (seq(g[:-1]), 0, 0, 0)),
                   pl.BlockSpec((1, n_grp, 1, LANES), lambda *g: (seq(g[:-1]), 0, 0, 0))])
    out_spec = pl.BlockSpec((t, d), lambda *g: (seq(g[:-1]), 0))
    scratch = [pltpu.VMEM((n_grp, r, hd), F32), pltpu.VMEM((n_grp, r, LANES), F32),
               pltpu.VMEM((n_grp, r, LANES), F32), pltpu.VMEM((n_grp, r, hd), F32)]
    return (b, n_pages // n_pg), n_pg, in_specs, out_spec, jax.ShapeDtypeStruct((m, d), F32), scratch


def _trunk(x3, mods, s0, attend_ctx, wts):
    (norm_pre, norm_post, a_w_in, a_w_out, a_lb_logits, a_out_norm, kv_norm, w_kv, wft, b_forget,
     b_w_q, b_w_o, mlp_w_in, mlp_w_out) = wts
    b, l, d = x3.shape
    m = b * l
    depth = norm_pre.shape[0]
    n_a = a_w_out.shape[0]
    a_heads = s0.shape[2]
    pick = lambda w, layer: (w[layer], 0) if isinstance(w, list) else (w, layer)
    casts = {"a_w_in": [], "mlp_w_in": [], "mlp_w_out": []}
    b_heads = b_forget.shape[0]
    x = x3.reshape(m, d)
    if m <= 512:
        rows_per_mod = m
        expand = lambda t: jnp.repeat(t, l, axis=0)[None]
    else:
        rows_per_mod = l
        expand = lambda t: t[:, None, :]
    row = lambda t: t.reshape(1, -1)
    states = []
    shared = None
    for layer in range(depth):
        mod = expand(mods[layer, 0])
        if layer < n_a:
            w_in, w_idx = pick(a_w_in, layer)
            proj = _norm_mm(x, row(norm_pre[layer, 0]), mod, w_in, w_idx, rows_per_mod=rows_per_mod)
            if w_in.dtype != BF16:
                proj, w_in_b = proj
                casts["a_w_in"].append(w_in_b)
            o, s_new = _hgrn(proj.reshape(b, l, -1), a_lb_logits, row(a_out_norm[layer]), s0[layer],
                             layer=layer, heads=a_heads)
            states.append(s_new)
            y_in, w_o, w_idx = o.reshape(m, d), a_w_out, layer
        else:
            j = layer - n_a
            if shared is None:
                kvg = row(kv_norm)
                k, k16 = _norm_mm(x, kvg, None, w_kv, 0, rows_per_mod=rows_per_mod, n_cols=d, col_off=0,
                                  out_dtypes=(F32, BF16))
                v, v16 = _norm_mm(x, kvg, None, w_kv, 0, rows_per_mod=rows_per_mod, n_cols=d, col_off=d,
                                  out_dtypes=(F32, BF16))
                shared = attend_ctx["make"](x, kvg, wft, b_forget.reshape(b_heads, 1), k, v, k16, v16)
            q = _norm_mm(x, row(norm_pre[layer, 0]), mod, b_w_q, j, rows_per_mod=rows_per_mod,
                         out_dtypes=(attend_ctx["q_dtype"],))
            y_in, w_o, w_idx = attend_ctx["attend"](q, shared), b_w_o, j
        x = _mm_post(y_in, w_o, w_idx, x, mod, row(norm_post[layer, 0]), rows_per_mod=rows_per_mod)
        mod = expand(mods[layer, 1])
        w1, w_idx = pick(mlp_w_in, layer)
        w2, _ = pick(mlp_w_out, layer)
        x = _mlp(x, row(norm_pre[layer, 1]), mod, w1, w2, w_idx, row(norm_post[layer, 1]),
                 rows_per_mod=rows_per_mod)
        if w1.dtype != BF16:
            x, w1_b, w2_b = x
            casts["mlp_w_in"].append(w1_b)
            casts["mlp_w_out"].append(w2_b)
    k, v, lft = shared["rows"]
    kv_shape = (b, l, b_heads, d // b_heads)
    return (x.reshape(b, l, d), jnp.stack(states), k.reshape(kv_shape), v.reshape(kv_shape),
            jnp.swapaxes(lft, 1, 2), casts)


def kernel(x_prompt, x_sample, c_prompt, c_sample, state_hgrn, cache_k, cache_v, cache_logf, page_table,
           norm_pre, norm_post, w_ada, b_ada, a_w_in, a_w_out, a_lb_logits, a_out_norm,
           kv_norm, w_kv, b_forget, b_w_q, b_w_o, mlp_w_in, mlp_w_out):
    bp, _, d = x_prompt.shape
    bs, ts, _ = x_sample.shape
    depth = norm_pre.shape[0]
    n_phys, page, heads, hd = cache_k.shape
    wts_s = (norm_pre, norm_post, a_w_in, a_w_out.astype(BF16), a_lb_logits, a_out_norm, kv_norm,
             w_kv.astype(BF16)[None], w_kv[:, 2 * d:].T, b_forget,
             b_w_q.astype(BF16), b_w_o.astype(BF16), mlp_w_in, mlp_w_out)

    n_c = bp + bs
    c_rows = -(-n_c // 8) * 8
    c_all = jnp.concatenate([c_prompt, c_sample, jnp.zeros((c_rows - n_c, d), F32)], axis=0)
    mods = _ada_all(c_all, w_ada.reshape(depth * 2, d, 3 * d), b_ada.reshape(depth * 2, 1, 3 * d))
    mods = mods.reshape(depth, 2, c_rows, 3 * d)
    mods_p, mods_s = mods[:, :, :bp], mods[:, :, bp:n_c]

    grp = HEAD_GROUP
    n_grp = heads // grp
    prow = page * grp // LANES
    lf_grp = cache_logf.astype(F32).reshape(n_phys, page, n_grp, grp).transpose(0, 2, 1, 3)
    fpast = _fpast(page_table, lf_grp.reshape(n_phys, n_grp, prow, LANES))
    f_total = fpast[:, :, -1, LANES - grp:].reshape(bs, heads, 1)

    def make_sample(x, kvg, wft, bft, k, v, k16, v16):
        lft, fnew_t = _logf(x, kvg, wft, bft, f_total, batch=bs)
        fq_col = fnew_t.reshape(bs, n_grp, grp * ts, 1)
        fnew_flat = fnew_t.reshape(bs, n_grp, grp, ts).transpose(0, 1, 3, 2).reshape(bs, n_grp, 1, ts * grp)
        fnew_flat = jnp.pad(fnew_flat, ((0, 0), (0, 0), (0, 0), (0, LANES - ts * grp)))
        shape4 = (bs, ts, heads, hd)
        return {"rows": (k, v, lft), "k": k.reshape(shape4), "v": v.reshape(shape4), "fq": fq_col, "fn": fnew_flat}

    def attend_sample(q, sh):
        return _fox_sample(q, cache_k, cache_v, page_table, fpast, sh["k"], sh["v"], sh["fq"], sh["fn"])

    y_s, st_s, k_s, v_s, lf_s, casts = _trunk(
        x_sample, mods_s, state_hgrn, {"make": make_sample, "attend": attend_sample, "q_dtype": F32}, wts_s)

    def make_prompt(x, kvg, wft, bft, k, v, k16, v16):
        lft, ft = _logf(x, kvg, wft, bft, jnp.zeros((bp, heads, 1), F32), batch=bp)
        return {"rows": (k, v, lft), "k": k16, "v": v16, "ft": ft}

    def attend_prompt(q, sh):
        return _fox_prompt(q, sh["k"], sh["v"], sh["ft"], batch=bp, heads=heads)

    wts_p = wts_s[:2] + (casts["a_w_in"],) + wts_s[3:12] + (casts["mlp_w_in"], casts["mlp_w_out"])
    s0_prompt = jnp.zeros((a_w_out.shape[0], bp) + state_hgrn.shape[2:], F32)
    y_p, st_p, k_p, v_p, lf_p, _ = _trunk(
        x_prompt, mods_p, s0_prompt, {"make": make_prompt, "attend": attend_prompt, "q_dtype": BF16}, wts_p)

    return (y_p, y_s, st_p, st_s, k_p, v_p, lf_p, k_s, v_s, lf_s)
```

```python
import functools
import math

import jax
import jax.numpy as jnp
from jax import lax
from jax.experimental import pallas as pl
from jax.experimental.pallas import tpu as pltpu

EPS = 1e-6
MASK_VALUE = -1e30
TINY = 1e-30
LOG2E = math.log2(math.e)
F32 = jnp.float32
BF16 = jnp.bfloat16
HIGHEST = lax.Precision.HIGHEST

LANES = 128
HGRN_CHUNK = 128
HGRN_SUB = 8
SUBLANES = 8
BF16_ROWS = 16
HEAD_GROUP = SUBLANES
PAGES_PER_STEP = 8
FPAST_PAGES_PER_STEP = 16
HGRN_HEADS_PER_STEP = 8
ATTN_HEADS_PER_STEP = 4
MIB = 2 ** 20


def _params(n_axes, vmem_mib):
    return pltpu.CompilerParams(
        dimension_semantics=("arbitrary",) * n_axes, vmem_limit_bytes=vmem_mib * MIB)


def _tile(n, pref):
    t = min(n, pref)
    while n % t:
        t -= 1
    return t


def _sigmoid(x):
    return 1.0 / (1.0 + jnp.exp(-x))


def _log_sigmoid(x):
    return jnp.minimum(x, 0.0) - jnp.log1p(jnp.exp(-jnp.abs(x)))


def _rms(x, g):
    return x * lax.rsqrt(jnp.mean(x * x, axis=-1, keepdims=True) + EPS) * g


def _dot(a, b, precision=None):
    return jnp.dot(a, b, preferred_element_type=F32, precision=precision)


def _dot_nt(a, b, precision=None):
    return lax.dot_general(a, b, (((1,), (1,)), ((), ())), preferred_element_type=F32, precision=precision)


def _iota(shape, dim):
    return lax.broadcasted_iota(jnp.int32, shape, dim)


def _ada_kernel(c_ref, w_ref, b_ref, o_ref):
    c = c_ref[...]
    s = (c * _sigmoid(c)).astype(BF16)
    o_ref[0] = _dot(s, w_ref[0].astype(BF16)) + b_ref[0]


def _ada_all(c_all, w_ada, b_ada):
    n, d, n3 = w_ada.shape
    r = c_all.shape[0]
    tn = _tile(n3, 1024)
    return pl.pallas_call(
        _ada_kernel,
        grid=(n, n3 // tn),
        in_specs=[
            pl.BlockSpec((r, d), lambda i, j: (0, 0)),
            pl.BlockSpec((1, d, tn), lambda i, j: (i, 0, j)),
            pl.BlockSpec((1, 1, tn), lambda i, j: (i, 0, j)),
        ],
        out_specs=pl.BlockSpec((1, r, tn), lambda i, j: (i, 0, j)),
        out_shape=jax.ShapeDtypeStruct((n, r, n3), F32),
        compiler_params=_params(2, 40),
    )(c_all, w_ada, b_ada)


def _norm_mm_kernel(*refs, has_mod, n_out, emit):
    if has_mod:
        x_ref, g_ref, sh_ref, sc_ref, w_ref = refs[:5]
        rest = refs[5:]
    else:
        x_ref, g_ref, w_ref = refs[:3]
        rest = refs[3:]
    o_refs, h_ref = rest[:n_out], rest[-1]

    @pl.when(pl.program_id(1) == 0)
    def _():
        h = _rms(x_ref[...], g_ref[...])
        if has_mod:
            h = h * (1.0 + sc_ref[0]) + sh_ref[0]
        h_ref[...] = h.astype(BF16)

    w = w_ref[0].astype(BF16)
    if emit:
        rest[n_out][0] = w
    y = _dot(h_ref[...], w)
    for o_ref in o_refs:
        o_ref[...] = y.astype(o_ref.dtype)


def _norm_mm(x, g, mod, w, layer, *, rows_per_mod, n_cols=None, col_off=0, out_dtypes=(F32,),
             tm_pref=1024, tn_pref=1024):
    m, d = x.shape
    n_cols = w.shape[2] if n_cols is None else n_cols
    tm = _tile(m, tm_pref)
    tn = _tile(n_cols, tn_pref)
    assert col_off % tn == 0
    joff = col_off // tn
    emit = w.dtype != BF16
    assert not emit or (col_off == 0 and n_cols == w.shape[2])
    in_specs = [pl.BlockSpec((tm, d), lambda i, j: (i, 0)), pl.BlockSpec((1, d), lambda i, j: (0, 0))]
    args = [x, g]
    if mod is not None:
        r = mod.shape[1]
        assert rows_per_mod % tm == 0 and r in (1, tm)
        bidx = lambda i: (i * tm) // rows_per_mod
        in_specs += [pl.BlockSpec((1, r, d), lambda i, j: (bidx(i), 0, 0)),
                     pl.BlockSpec((1, r, d), lambda i, j: (bidx(i), 0, 1))]
        args += [mod, mod]
    in_specs.append(pl.BlockSpec((1, d, tn), lambda i, j: (layer, 0, j + joff)))
    args.append(w)
    out_specs = [pl.BlockSpec((tm, tn), lambda i, j: (i, j)) for _ in out_dtypes]
    out_shape = [jax.ShapeDtypeStruct((m, n_cols), dt) for dt in out_dtypes]
    if emit:
        out_specs.append(pl.BlockSpec((1, d, tn), lambda i, j: (0, 0, j)))
        out_shape.append(jax.ShapeDtypeStruct((1, d, n_cols), BF16))
    outs = pl.pallas_call(
        functools.partial(_norm_mm_kernel, has_mod=mod is not None, n_out=len(out_dtypes), emit=emit),
        grid=(m // tm, n_cols // tn),
        in_specs=in_specs,
        out_specs=out_specs,
        out_shape=out_shape,
        scratch_shapes=[pltpu.VMEM((tm, d), BF16)],
        compiler_params=_params(2, 56),
    )(*args)
    return outs if len(outs) > 1 else outs[0]


def _mm_post_kernel(a_ref, w_ref, x_ref, gate_ref, g_ref, o_ref):
    y = _dot(a_ref[...].astype(BF16), w_ref[...])
    o_ref[...] = x_ref[...] + gate_ref[0] * _rms(y, g_ref[...])


def _mm_post(a, w, layer, x, mod, g, *, rows_per_mod, tm_pref=512):
    m, kdim = a.shape
    d = w.shape[2]
    tm = _tile(m, tm_pref)
    r = mod.shape[1]
    assert rows_per_mod % tm == 0 and r in (1, tm)
    bidx = lambda i: (i * tm) // rows_per_mod
    return pl.pallas_call(
        _mm_post_kernel,
        grid=(m // tm,),
        in_specs=[
            pl.BlockSpec((tm, kdim), lambda i: (i, 0)),
            pl.BlockSpec((None, kdim, d), lambda i: (layer, 0, 0)),
            pl.BlockSpec((tm, d), lambda i: (i, 0)),
            pl.BlockSpec((1, r, d), lambda i: (bidx(i), 0, 2)),
            pl.BlockSpec((1, d), lambda i: (0, 0)),
        ],
        out_specs=pl.BlockSpec((tm, d), lambda i: (i, 0)),
        out_shape=jax.ShapeDtypeStruct((m, d), F32),
        compiler_params=_params(1, 56),
    )(a, w, x, mod, g)


def _mlp_kernel(*refs, emit):
    x_ref, gpre_ref, sh_ref, sc_ref, gate_ref, w1_ref, w2_ref, gpost_ref, o_ref = refs[:9]
    h_ref, acc_ref = refs[-2:]
    f = pl.program_id(1)

    @pl.when(f == 0)
    def _():
        h = _rms(x_ref[...], gpre_ref[...]) * (1.0 + sc_ref[0]) + sh_ref[0]
        h_ref[...] = h.astype(BF16)
        acc_ref[...] = jnp.zeros_like(acc_ref)

    w1 = w1_ref[0].astype(BF16)
    w2 = w2_ref[0].astype(BF16)
    if emit:
        w1b_ref, w2b_ref = refs[9:11]
        w1b_ref[0] = w1
        w2b_ref[0] = w2
    u = jnp.maximum(_dot(h_ref[...], w1), 0.0)
    acc_ref[...] += _dot((u * u).astype(BF16), w2)

    @pl.when(f == pl.num_programs(1) - 1)
    def _():
        o_ref[...] = x_ref[...] + gate_ref[0] * _rms(acc_ref[...], gpost_ref[...])


def _mlp(x, gpre, mod, w1, w2, layer, gpost, *, rows_per_mod, tm_pref=512, tf_pref=1024):
    m, d = x.shape
    ff = w1.shape[2]
    tm = _tile(m, tm_pref)
    tf = _tile(ff, tf_pref)
    r = mod.shape[1]
    assert rows_per_mod % tm == 0 and r in (1, tm)
    emit = w1.dtype != BF16
    bidx = lambda i: (i * tm) // rows_per_mod
    out_specs = [pl.BlockSpec((tm, d), lambda i, f: (i, 0))]
    out_shape = [jax.ShapeDtypeStruct((m, d), F32)]
    if emit:
        out_specs += [pl.BlockSpec((1, d, tf), lambda i, f: (0, 0, f)),
                      pl.BlockSpec((1, tf, d), lambda i, f: (0, f, 0))]
        out_shape += [jax.ShapeDtypeStruct((1, d, ff), BF16), jax.ShapeDtypeStruct((1, ff, d), BF16)]
    outs = pl.pallas_call(
        functools.partial(_mlp_kernel, emit=emit),
        grid=(m // tm, ff // tf),
        in_specs=[
            pl.BlockSpec((tm, d), lambda i, f: (i, 0)),
            pl.BlockSpec((1, d), lambda i, f: (0, 0)),
            pl.BlockSpec((1, r, d), lambda i, f: (bidx(i), 0, 0)),
            pl.BlockSpec((1, r, d), lambda i, f: (bidx(i), 0, 1)),
            pl.BlockSpec((1, r, d), lambda i, f: (bidx(i), 0, 2)),
            pl.BlockSpec((1, d, tf), lambda i, f: (layer, 0, f)),
            pl.BlockSpec((1, tf, d), lambda i, f: (layer, f, 0)),
            pl.BlockSpec((1, d), lambda i, f: (0, 0)),
        ],
        out_specs=out_specs,
        out_shape=out_shape,
        scratch_shapes=[pltpu.VMEM((tm, d), BF16), pltpu.VMEM((tm, d), F32)],
        compiler_params=_params(2, 56),
    )(x, gpre, mod, mod, mod, w1, w2, gpost)
    return outs if emit else outs[0]


def _hgrn_kernel(q_ref, f_ref, i_ref, g_ref, lbl_ref, og_ref, s0_ref, o_ref, s_ref, st_ref,
                 *, layer, c, rows, n_inner, hpb):
    sub = HGRN_SUB
    dk = q_ref.shape[-1] // hpb
    step = pl.program_id(2)

    @pl.when(step == 0)
    def _():
        for h in range(hpb):
            st_ref[h] = s0_ref[0, h].T

    z = lbl_ref[...]
    e = jnp.exp(z - jnp.max(z, axis=0, keepdims=True))
    p = e / jnp.sum(e, axis=0, keepdims=True)
    cs = p[0:1]
    for r in range(1, layer + 1):
        cs = cs + p[r:r + 1]
    lb_all = cs - p[0:1]
    og_all = og_ref[...]

    tril = (_iota((c, c), 0) >= _iota((c, c), 1)).astype(BF16)
    lane = _iota((sub, c), 1)
    sub_row = _iota((sub, c), 0)
    scale = dk ** -0.5

    def pad(t, fill=0.0):
        if rows == c:
            return t
        return jnp.concatenate([t, jnp.full((c - rows, t.shape[1]), fill, t.dtype)], axis=0)

    def cumsum_rows(x):
        hi = x.astype(BF16)
        r1 = x - hi.astype(F32)
        mid = r1.astype(BF16)
        lo = (r1 - mid.astype(F32)).astype(BF16)
        return _dot(tril, hi) + _dot(tril, mid) + _dot(tril, lo)

    def head_chunk(h, sl):
        hs = slice(h * dk, (h + 1) * dk)
        lb = lb_all[:, hs]
        lb_floor = jnp.maximum(lb, TINY)
        one_m_lb = 1.0 - lb
        log2_one_m_lb = jnp.log(one_m_lb) * LOG2E
        qz = q_ref[0, sl, hs]
        fz = f_ref[0, sl, hs]
        q = pad(qz * _sigmoid(qz))
        e = jnp.exp(-jnp.abs(fz))
        t = 1.0 + e
        r = 1.0 / t
        sig = jnp.where(fz >= 0.0, r, e * r)
        logf = pad(jnp.log(lb_floor + one_m_lb * sig))
        lk = pad(log2_one_m_lb + (jnp.minimum(-fz, 0.0) - jnp.log(t)) * LOG2E, -jnp.inf)
        v = pad(i_ref[0, sl, hs])
        b2 = cumsum_rows(logf) * LOG2E
        c2 = b2 - lk
        st = st_ref[h]
        inter = _dot_nt((q * jnp.exp2(b2)).astype(BF16), st.astype(BF16))
        blocks = []
        for i in range(c // sub):
            lo = i * sub
            qi, bi, ci = q[lo:lo + sub], b2[lo:lo + sub], c2[lo:lo + sub]
            if i > 0:
                ref_b = bi[0:1]
                qs = (qi * jnp.exp2(bi - ref_b)).astype(BF16)
                kt = jnp.exp2(ref_b - c2[:lo]).astype(BF16)
                kt = jnp.concatenate([kt, jnp.zeros((c - lo, dk), BF16)], axis=0)
                s_i = _dot_nt(qs, kt)
            else:
                s_i = jnp.zeros((sub, c), F32)
            for s in range(sub):
                w = jnp.sum(qi * jnp.exp2(bi - ci[s:s + 1]), axis=-1, keepdims=True)
                s_i = jnp.where(lane == lo + s, w, s_i)
            blocks.append(jnp.where(lane <= lo + sub_row, s_i, 0.0))
        scores = jnp.concatenate(blocks, axis=0).astype(BF16)
        o = (inter + _dot(scores, v.astype(BF16))) * scale
        blast = b2[c - 1:c]
        kd = jnp.exp2(blast - c2).astype(BF16)
        v_sq = v if c == dk else jnp.concatenate([v, jnp.zeros((dk - c, dk), F32)], axis=0)
        kd_sq = kd if c == dk else jnp.concatenate([kd, jnp.zeros((dk - c, dk), BF16)], axis=0)
        st_ref[h] = st * jnp.exp2(blast) + _dot(v_sq.T.astype(BF16), kd_sq)
        gz = g_ref[0, sl, hs]
        o_ref[0, sl, hs] = _rms(o[:rows], og_all[:, hs]) * (gz * _sigmoid(gz))

    def chunk(j, carry):
        if n_inner == 1:
            sl = pl.ds(0, rows)
        else:
            sl = pl.ds(pl.multiple_of(j * c, c), c)
        for h in range(hpb):
            head_chunk(h, sl)
        return carry

    lax.fori_loop(0, n_inner, chunk, 0)

    @pl.when(step == pl.num_programs(2) - 1)
    def _():
        for h in range(hpb):
            s_ref[0, h] = st_ref[h].T


def _hgrn(proj, lb_logits, out_g, s0, *, layer, heads):
    b, l, _ = proj.shape
    dk = s0.shape[2]
    dv = s0.shape[3]
    assert dk == LANES and dv == LANES
    if l >= HGRN_CHUNK:
        assert l % HGRN_CHUNK == 0
        c = rows = HGRN_CHUNK
        n_inner = _tile(l // c, 4)
        blk = rows * n_inner
    else:
        assert l % SUBLANES == 0
        c = -(-l // BF16_ROWS) * BF16_ROWS
        rows, n_inner, blk = l, 1, l
    nl = lb_logits.shape[0]
    hpb = _tile(heads, HGRN_HEADS_PER_STEP)
    ng = heads // hpb
    col = lambda part: pl.BlockSpec((1, blk, hpb * dk), lambda bi, h, s: (bi, s, part * ng + h))
    o, s_new = pl.pallas_call(
        functools.partial(_hgrn_kernel, layer=layer, c=c, rows=rows, n_inner=n_inner, hpb=hpb),
        grid=(b, ng, l // blk),
        in_specs=[
            col(0), col(1), col(2), col(3),
            pl.BlockSpec((nl, hpb * dk), lambda bi, h, s: (0, h)),
            pl.BlockSpec((1, hpb * dv), lambda bi, h, s: (0, h)),
            pl.BlockSpec((1, hpb, dk, dv), lambda bi, h, s: (bi, h, 0, 0)),
        ],
        out_specs=[
            pl.BlockSpec((1, blk, hpb * dv), lambda bi, h, s: (bi, s, h)),
            pl.BlockSpec((1, hpb, dk, dv), lambda bi, h, s: (bi, h, 0, 0)),
        ],
        out_shape=[jax.ShapeDtypeStruct((b, l, heads * dv), F32),
                   jax.ShapeDtypeStruct((b, heads, dk, dv), F32)],
        scratch_shapes=[pltpu.VMEM((hpb, dv, dk), F32)],
        compiler_params=_params(3, 32),
    )(proj, proj, proj, proj, lb_logits, out_g, s0)
    return o, s_new


def _logf_kernel(x_ref, g_ref, wft_ref, bft_ref, c0_ref, lf_ref, ft_ref, car_ref):
    tm = x_ref.shape[0]

    @pl.when(pl.program_id(1) == 0)
    def _():
        car_ref[...] = c0_ref[0]

    h = _rms(x_ref[...], g_ref[...])
    zt = _dot_nt(wft_ref[...], h, HIGHEST) + bft_ref[...]
    lft = _log_sigmoid(zt)
    lf_ref[0] = lft
    triu = (_iota((tm, tm), 0) <= _iota((tm, tm), 1)).astype(F32)
    ft = _dot(lft, triu, HIGHEST) + car_ref[...]
    ft_ref[0] = ft
    car_ref[...] = ft[:, tm - 1:tm]


def _logf(x, g, wft, bft, c0t, *, batch):
    m, d = x.shape
    l = m // batch
    hh = wft.shape[0]
    tm = _tile(l, 512)
    nt = l // tm
    out = jax.ShapeDtypeStruct((batch, hh, l), F32)
    return pl.pallas_call(
        _logf_kernel,
        grid=(batch, nt),
        in_specs=[
            pl.BlockSpec((tm, d), lambda b, i: (b * nt + i, 0)),
            pl.BlockSpec((1, d), lambda b, i: (0, 0)),
            pl.BlockSpec((hh, d), lambda b, i: (0, 0)),
            pl.BlockSpec((hh, 1), lambda b, i: (0, 0)),
            pl.BlockSpec((1, hh, 1), lambda b, i: (b, 0, 0)),
        ],
        out_specs=[pl.BlockSpec((1, hh, tm), lambda b, i: (b, 0, i)),
                   pl.BlockSpec((1, hh, tm), lambda b, i: (b, 0, i))],
        out_shape=[out, out],
        scratch_shapes=[pltpu.VMEM((hh, 1), F32)],
        compiler_params=_params(2, 32),
    )(x, g, wft, bft, c0t)


def _fox_prompt_kernel(q_ref, k_ref, v_ref, fq_ref, fk_ref, o_ref, m_ref, l_ref, acc_ref, *, tq, hpb):
    i = pl.program_id(2)
    hd = q_ref.shape[-1] // hpb
    n_lt = tq // LANES
    m_ref[...] = jnp.full_like(m_ref, MASK_VALUE)
    l_ref[...] = jnp.zeros_like(l_ref)
    acc_ref[...] = jnp.zeros_like(acc_ref)
    qs, fqs = [], []
    for h in range(hpb):
        qs.append((q_ref[:, h * hd:(h + 1) * hd].astype(F32) * (hd ** -0.5 * LOG2E)).astype(BF16))
        fqs.append(jnp.broadcast_to(fq_ref[0, h] * LOG2E, (tq, LANES)))

    def tile(j, masked):
        ks = pl.ds(pl.multiple_of(j * tq, tq), tq)
        for h in range(hpb):
            cs = slice(h * hd, (h + 1) * hd)
            s = _dot_nt(qs[h], k_ref[ks, cs])
            fk = fk_ref[0, h, :, ks] * LOG2E
            us = []
            for c in range(n_lt):
                ls = slice(c * LANES, (c + 1) * LANES)
                u = s[:, ls] - fk[:, ls]
                if masked:
                    u = jnp.where(_iota((tq, LANES), 0) >= c * LANES + _iota((tq, LANES), 1), u, MASK_VALUE)
                us.append(u)
            row_max = jnp.max(functools.reduce(jnp.maximum, us), axis=-1, keepdims=True)
            m_old = m_ref[h]
            m_new = jnp.maximum(m_old, jnp.broadcast_to(row_max, (tq, LANES)) + fqs[h])
            shift = fqs[h] - m_new
            ps = [jnp.exp2(u + shift) for u in us]
            alpha = jnp.exp2(m_old - m_new)
            m_ref[h] = m_new
            l_ref[h] = alpha * l_ref[h] + functools.reduce(jnp.add, ps)
            p_all = jnp.concatenate([p.astype(BF16) for p in ps], axis=1)
            acc_ref[h] = alpha * acc_ref[h] + _dot(p_all, v_ref[ks, cs])

    def body(j, carry):
        tile(j, False)
        return carry

    lax.fori_loop(0, i, body, 0)
    tile(i, True)
    for h in range(hpb):
        o_ref[:, h * hd:(h + 1) * hd] = acc_ref[h] / jnp.sum(l_ref[h], axis=-1, keepdims=True)


def _fox_prompt(q, k, v, ft, *, batch, heads):
    m, d = q.shape
    l = m // batch
    hd = d // heads
    assert hd == LANES
    hpb = ATTN_HEADS_PER_STEP
    tq = _tile(l, 1024)
    nq = l // tq
    fcol = ft[..., None]
    frow = ft[:, :, None, :]
    return pl.pallas_call(
        functools.partial(_fox_prompt_kernel, tq=tq, hpb=hpb),
        grid=(batch, heads // hpb, nq),
        in_specs=[
            pl.BlockSpec((tq, hpb * hd), lambda b, h, i: (b * nq + i, h)),
            pl.BlockSpec((l, hpb * hd), lambda b, h, i: (b, h)),
            pl.BlockSpec((l, hpb * hd), lambda b, h, i: (b, h)),
            pl.BlockSpec((1, hpb, tq, 1), lambda b, h, i: (b, h, i, 0)),
            pl.BlockSpec((1, hpb, 1, l), lambda b, h, i: (b, h, 0, 0)),
        ],
        out_specs=pl.BlockSpec((tq, hpb * hd), lambda b, h, i: (b * nq + i, h)),
        out_shape=jax.ShapeDtypeStruct((m, d), F32),
        scratch_shapes=[pltpu.VMEM((hpb, tq, LANES), F32), pltpu.VMEM((hpb, tq, LANES), F32),
                        pltpu.VMEM((hpb, tq, hd), F32)],
        compiler_params=_params(3, 56),
    )(q, k, v, fcol, frow)


def _fpast_kernel(pt_ref, lf_ref, o_ref, car_ref, *, n_pg):
    b, g = pl.program_id(0), pl.program_id(1)
    n_grp, prow = lf_ref.shape[1], lf_ref.shape[2]
    rows = n_pg * prow
    keys_per_row = LANES // HEAD_GROUP

    @pl.when(g == 0)
    def _():
        car_ref[...] = jnp.zeros_like(car_ref)

    li, lj = _iota((LANES, LANES), 0), _iota((LANES, LANES), 1)
    same_head = (li % HEAD_GROUP) == (lj % HEAD_GROUP)
    prefix = (same_head & (li // HEAD_GROUP <= lj // HEAD_GROUP)).astype(F32)
    last = (same_head & (li // HEAD_GROUP == keys_per_row - 1)).astype(F32)
    strict = (_iota((rows, rows), 0) > _iota((rows, rows), 1)).astype(F32)
    for a in range(n_grp):
        x = jnp.concatenate([lf_ref[pt_ref[b, g * n_pg + o], a] for o in range(n_pg)], axis=0)
        y = _dot(x, prefix, HIGHEST)
        tot = _dot(y, last, HIGHEST)
        o_ref[0, a] = y + _dot(strict, tot, HIGHEST) + car_ref[a]
        car_ref[a] += jnp.sum(tot, axis=0, keepdims=True)


def _fpast(page_table, cache_logf_grp):
    b, n_pages = page_table.shape
    n_phys, n_grp, prow, _ = cache_logf_grp.shape
    n_pg = _tile(n_pages, FPAST_PAGES_PER_STEP)
    return pl.pallas_call(
        functools.partial(_fpast_kernel, n_pg=n_pg),
        grid_spec=pltpu.PrefetchScalarGridSpec(
            num_scalar_prefetch=1,
            grid=(b, n_pages // n_pg),
            in_specs=[pl.BlockSpec((n_phys, n_grp, prow, LANES), lambda bi, g, pt: (0, 0, 0, 0))],
            out_specs=pl.BlockSpec((1, n_grp, n_pg * prow, LANES), lambda bi, g, pt: (bi, 0, g, 0)),
            scratch_shapes=[pltpu.VMEM((n_grp, 1, LANES), F32)],
        ),
        out_shape=jax.ShapeDtypeStruct((b, n_grp, n_pages * prow, LANES), F32),
        compiler_params=_params(2, 40),
    )(page_table, cache_logf_grp)


def _fox_sample_kernel(pt_ref, q_ref, *refs, heads, n_pg):
    step, nsteps = pl.program_id(1), pl.num_programs(1)
    kc_refs, vc_refs = refs[:n_pg], refs[n_pg:2 * n_pg]
    fp_ref, kn_ref, vn_ref, fq_ref, fn_ref, o_ref, qa_ref, m_ref, l_ref, acc_ref = refs[2 * n_pg:]
    t, d = q_ref.shape
    hd = d // heads
    grp = HEAD_GROUP
    n_grp = heads // grp
    r = grp * t
    page = kc_refs[0].shape[1]
    prow = page * grp // LANES

    def pre():
        @pl.when(step == 0)
        def _():
            q = q_ref[...] * (hd ** -0.5 * LOG2E)
            for a in range(n_grp):
                qa_ref[a] = jnp.concatenate(
                    [q[:, (a * grp + j) * hd:(a * grp + j + 1) * hd] for j in range(grp)], axis=0)
            m_ref[...] = jnp.full_like(m_ref, MASK_VALUE)
            l_ref[...] = jnp.zeros_like(l_ref)
            acc_ref[...] = jnp.zeros_like(acc_ref)

    head_match = (_iota((r, LANES), 0) // t) == (_iota((r, LANES), 1) % grp)

    def update(a, k_list, v_list, fk_list, mask):
        fq = jnp.broadcast_to(fq_ref[0, a] * LOG2E, (r, LANES))
        qa = qa_ref[a]
        us = []
        for k_flat, fk in zip(k_list, fk_list):
            s = _dot_nt(qa, k_flat)
            for c in range(k_flat.shape[0] // LANES):
                u = s[:, c * LANES:(c + 1) * LANES] - fk[c:c + 1] * LOG2E
                us.append(jnp.where(mask, u, MASK_VALUE))
        row_max = jnp.max(functools.reduce(jnp.maximum, us), axis=-1, keepdims=True)
        m_old = m_ref[a]
        m_new = jnp.maximum(m_old, jnp.broadcast_to(row_max, (r, LANES)) + fq)
        shift = fq - m_new
        ps = [jnp.exp2(u + shift) for u in us]
        alpha = jnp.exp2(m_old - m_new)
        m_ref[a] = m_new
        l_ref[a] = alpha * l_ref[a] + functools.reduce(jnp.add, ps)
        pv, at = None, 0
        for v_flat in v_list:
            n_t = v_flat.shape[0] // LANES
            p_pg = jnp.concatenate(ps[at:at + n_t], axis=1) if n_t > 1 else ps[at]
            at += n_t
            part = _dot(p_pg, v_flat)
            pv = part if pv is None else pv + part
        acc_ref[a] = alpha * acc_ref[a] + pv

    def group_rows(ref, a):
        blk = ref[0, :, a * grp:(a + 1) * grp, :]
        return blk.reshape(blk.shape[0] * grp, hd)

    def main():
        for a in range(n_grp):
            update(a, [group_rows(kc, a) for kc in kc_refs], [group_rows(vc, a) for vc in vc_refs],
                   [fp_ref[0, a, o * prow:(o + 1) * prow] for o in range(n_pg)], head_match)

    def post():
        @pl.when(step == nsteps - 1)
        def _():
            lane = _iota((r, LANES), 1)
            new_mask = head_match & (lane // grp <= _iota((r, LANES), 0) % t) & (lane < t * grp)
            zeros = jnp.zeros((LANES - t * grp, hd), F32)
            for a in range(n_grp):
                kn = jnp.concatenate([group_rows(kn_ref, a), zeros], axis=0)
                vn = jnp.concatenate([group_rows(vn_ref, a), zeros], axis=0)
                update(a, [kn], [vn], [fn_ref[0, a]], new_mask)
                res = acc_ref[a] / jnp.sum(l_ref[a], axis=-1, keepdims=True)
                for j in range(grp):
                    h = a * grp + j
                    o_ref[:, h * hd:(h + 1) * hd] = res[j * t:(j + 1) * t]

    pre()
    main()
    post()


def _fox_sample(q, cache_k, cache_v, page_table, fpast, k_new, v_new, fq_col, fnew_flat):
    b, n_pages = page_table.shape
    m, d = q.shape
    t = m // b
    _, page, heads, hd = cache_k.shape
    grp = HEAD_GROUP
    n_grp = heads // grp
    assert hd == LANES and heads % grp == 0 and t * grp <= LANES and (page * grp) % LANES == 0
    r = grp * t
    n_pg = _tile(n_pages, PAGES_PER_STEP)
    prow = page * grp // LANES
    page_spec = lambda o: pl.BlockSpec((1, page, heads, hd), lambda bi, s, pt: (pt[bi, s * n_pg + o], 0, 0, 0))
    new_spec = pl.BlockSpec((1, t, heads, hd), lambda bi, s, pt: (bi, 0, 0, 0))
    in_specs = ([pl.BlockSpec((t, d), lambda bi, s, pt: (bi, 0))]
                + [page_spec(o) for o in range(n_pg)] * 2
                + [pl.BlockSpec((1, n_grp, n_pg * prow, LANES), lambda bi, s, pt: (bi, 0, s, 0)),
                   new_spec, new_spec,
                   pl.BlockSpec((1, n_grp, r, 1), lambda bi, s, pt: (bi, 0, 0, 0)),
                   pl.BlockSpec((1, n_grp, 1, LANES), lambda bi, s, pt: (bi, 0, 0, 0))])
    return pl.pallas_call(
        functools.partial(_fox_sample_kernel, heads=heads, n_pg=n_pg),
        grid_spec=pltpu.PrefetchScalarGridSpec(
            num_scalar_prefetch=1,
            grid=(b, n_pages // n_pg),
            in_specs=in_specs,
            out_specs=pl.BlockSpec((t, d), lambda bi, s, pt: (bi, 0)),
            scratch_shapes=[pltpu.VMEM((n_grp, r, hd), F32), pltpu.VMEM((n_grp, r, LANES), F32),
                            pltpu.VMEM((n_grp, r, LANES), F32), pltpu.VMEM((n_grp, r, hd), F32)],
        ),
        out_shape=jax.ShapeDtypeStruct((m, d), F32),
        compiler_params=_params(2, 56),
    )(page_table, q, *([cache_k] * n_pg), *([cache_v] * n_pg), fpast, k_new, v_new, fq_col, fnew_flat)


def _trunk(x3, mods, s0, attend_ctx, wts):
    (norm_pre, norm_post, a_w_in, a_w_out, a_lb_logits, a_out_norm, kv_norm, w_kv, wft, b_forget,
     b_w_q, b_w_o, mlp_w_in, mlp_w_out) = wts
    b, l, d = x3.shape
    m = b * l
    depth = norm_pre.shape[0]
    n_a = a_w_out.shape[0]
    a_heads = s0.shape[2]
    pick = lambda w, layer: (w[layer], 0) if isinstance(w, list) else (w, layer)
    casts = {"a_w_in": [], "mlp_w_in": [], "mlp_w_out": []}
    b_heads = b_forget.shape[0]
    x = x3.reshape(m, d)
    if m <= 512:
        rows_per_mod = m
        expand = lambda t: jnp.repeat(t, l, axis=0)[None]
    else:
        rows_per_mod = l
        expand = lambda t: t[:, None, :]
    row = lambda t: t.reshape(1, -1)
    states = []
    shared = None
    for layer in range(depth):
        mod = expand(mods[layer, 0])
        if layer < n_a:
            w_in, w_idx = pick(a_w_in, layer)
            proj = _norm_mm(x, row(norm_pre[layer, 0]), mod, w_in, w_idx, rows_per_mod=rows_per_mod)
            if w_in.dtype != BF16:
                proj, w_in_b = proj
                casts["a_w_in"].append(w_in_b)
            o, s_new = _hgrn(proj.reshape(b, l, -1), a_lb_logits, row(a_out_norm[layer]), s0[layer],
                             layer=layer, heads=a_heads)
            states.append(s_new)
            y_in, w_o, w_idx = o.reshape(m, d), a_w_out, layer
        else:
            j = layer - n_a
            if shared is None:
                kvg = row(kv_norm)
                k, k16 = _norm_mm(x, kvg, None, w_kv, 0, rows_per_mod=rows_per_mod, n_cols=d, col_off=0,
                                  out_dtypes=(F32, BF16))
                v, v16 = _norm_mm(x, kvg, None, w_kv, 0, rows_per_mod=rows_per_mod, n_cols=d, col_off=d,
                                  out_dtypes=(F32, BF16))
                shared = attend_ctx["make"](x, kvg, wft, b_forget.reshape(b_heads, 1), k, v, k16, v16)
            q = _norm_mm(x, row(norm_pre[layer, 0]), mod, b_w_q, j, rows_per_mod=rows_per_mod,
                         out_dtypes=(attend_ctx["q_dtype"],))
            y_in, w_o, w_idx = attend_ctx["attend"](q, shared), b_w_o, j
        x = _mm_post(y_in, w_o, w_idx, x, mod, row(norm_post[layer, 0]), rows_per_mod=rows_per_mod)
        mod = expand(mods[layer, 1])
        w1, w_idx = pick(mlp_w_in, layer)
        w2, _ = pick(mlp_w_out, layer)
        x = _mlp(x, row(norm_pre[layer, 1]), mod, w1, w2, w_idx, row(norm_post[layer, 1]),
                 rows_per_mod=rows_per_mod)
        if w1.dtype != BF16:
            x, w1_b, w2_b = x
            casts["mlp_w_in"].append(w1_b)
            casts["mlp_w_out"].append(w2_b)
    k, v, lft = shared["rows"]
    kv_shape = (b, l, b_heads, d // b_heads)
    return (x.reshape(b, l, d), jnp.stack(states), k.reshape(kv_shape), v.reshape(kv_shape),
            jnp.swapaxes(lft, 1, 2), casts)


def kernel(x_prompt, x_sample, c_prompt, c_sample, state_hgrn, cache_k, cache_v, cache_logf, page_table,
           norm_pre, norm_post, w_ada, b_ada, a_w_in, a_w_out, a_lb_logits, a_out_norm,
           kv_norm, w_kv, b_forget, b_w_q, b_w_o, mlp_w_in, mlp_w_out):
    bp, _, d = x_prompt.shape
    bs, ts, _ = x_sample.shape
    depth = norm_pre.shape[0]
    n_phys, page, heads, hd = cache_k.shape
    wts_s = (norm_pre, norm_post, a_w_in, a_w_out.astype(BF16), a_lb_logits, a_out_norm, kv_norm,
             w_kv.astype(BF16)[None], w_kv[:, 2 * d:].T, b_forget,
             b_w_q.astype(BF16), b_w_o.astype(BF16), mlp_w_in, mlp_w_out)

    n_c = bp + bs
    c_rows = -(-n_c // 8) * 8
    c_all = jnp.concatenate([c_prompt, c_sample, jnp.zeros((c_rows - n_c, d), F32)], axis=0)
    mods = _ada_all(c_all, w_ada.reshape(depth * 2, d, 3 * d), b_ada.reshape(depth * 2, 1, 3 * d))
    mods = mods.reshape(depth, 2, c_rows, 3 * d)
    mods_p, mods_s = mods[:, :, :bp], mods[:, :, bp:n_c]

    grp = HEAD_GROUP
    n_grp = heads // grp
    prow = page * grp // LANES
    lf_grp = cache_logf.astype(F32).reshape(n_phys, page, n_grp, grp).transpose(0, 2, 1, 3)
    fpast = _fpast(page_table, lf_grp.reshape(n_phys, n_grp, prow, LANES))
    f_total = fpast[:, :, -1, LANES - grp:].reshape(bs, heads, 1)

    def make_sample(x, kvg, wft, bft, k, v, k16, v16):
        lft, fnew_t = _logf(x, kvg, wft, bft, f_total, batch=bs)
        fq_col = fnew_t.reshape(bs, n_grp, grp * ts, 1)
        fnew_flat = fnew_t.reshape(bs, n_grp, grp, ts).transpose(0, 1, 3, 2).reshape(bs, n_grp, 1, ts * grp)
        fnew_flat = jnp.pad(fnew_flat, ((0, 0), (0, 0), (0, 0), (0, LANES - ts * grp)))
        shape4 = (bs, ts, heads, hd)
        return {"rows": (k, v, lft), "k": k.reshape(shape4), "v": v.reshape(shape4), "fq": fq_col, "fn": fnew_flat}

    def attend_sample(q, sh):
        return _fox_sample(q, cache_k, cache_v, page_table, fpast, sh["k"], sh["v"], sh["fq"], sh["fn"])

    y_s, st_s, k_s, v_s, lf_s, casts = _trunk(
        x_sample, mods_s, state_hgrn, {"make": make_sample, "attend": attend_sample, "q_dtype": F32}, wts_s)

    def make_prompt(x, kvg, wft, bft, k, v, k16, v16):
        lft, ft = _logf(x, kvg, wft, bft, jnp.zeros((bp, heads, 1), F32), batch=bp)
        return {"rows": (k, v, lft), "k": k16, "v": v16, "ft": ft}

    def attend_prompt(q, sh):
        return _fox_prompt(q, sh["k"], sh["v"], sh["ft"], batch=bp, heads=heads)

    wts_p = wts_s[:2] + (casts["a_w_in"],) + wts_s[3:12] + (casts["mlp_w_in"], casts["mlp_w_out"])
    s0_prompt = jnp.zeros((a_w_out.shape[0], bp) + state_hgrn.shape[2:], F32)
    y_p, st_p, k_p, v_p, lf_p, _ = _trunk(
        x_prompt, mods_p, s0_prompt, {"make": make_prompt, "attend": attend_prompt, "q_dtype": BF16}, wts_p)

    return (y_p, y_s, st_p, st_s, k_p, v_p, lf_p, k_s, v_s, lf_s)
```

```python
import functools
import math

import jax
import jax.numpy as jnp
from jax import lax
from jax.experimental import pallas as pl
from jax.experimental.pallas import tpu as pltpu

EPS = 1e-6
MASK_VALUE = -1e30
TINY = 1e-30
LOG2E = math.log2(math.e)
F32 = jnp.float32
BF16 = jnp.bfloat16
HIGHEST = lax.Precision.HIGHEST

LANES = 128
HGRN_CHUNK = 128
HGRN_SUB = 8
HGRN_CHUNKS_PER_STEP = 8
SUBLANES = 8
BF16_ROWS = 16
HEAD_GROUP = SUBLANES
PAGES_PER_STEP = 8
FPAST_PAGES_PER_STEP = 16
HGRN_HEADS_PER_STEP = 8
ATTN_Q_TILE = 1024
ATTN_DIAG_TILE = 512
ATTN_HEADS_PER_STEP = 4
MIB = 2 ** 20


def _params(n_axes, vmem_mib):
    return pltpu.CompilerParams(
        dimension_semantics=("arbitrary",) * n_axes, vmem_limit_bytes=vmem_mib * MIB)


def _tile(n, pref):
    t = min(n, pref)
    while n % t:
        t -= 1
    return t


def _sigmoid(x):
    return 1.0 / (1.0 + jnp.exp(-x))


def _log_sigmoid(x):
    return jnp.minimum(x, 0.0) - jnp.log1p(jnp.exp(-jnp.abs(x)))


def _rms(x, g):
    return x * lax.rsqrt(jnp.mean(x * x, axis=-1, keepdims=True) + EPS) * g


def _dot(a, b, precision=None):
    return jnp.dot(a, b, preferred_element_type=F32, precision=precision)


def _dot_nt(a, b, precision=None):
    return lax.dot_general(a, b, (((1,), (1,)), ((), ())), preferred_element_type=F32, precision=precision)


def _iota(shape, dim):
    return lax.broadcasted_iota(jnp.int32, shape, dim)


def _ada_kernel(c_ref, w_ref, b_ref, o_ref):
    c = c_ref[...]
    s = (c * _sigmoid(c)).astype(BF16)
    o_ref[0] = _dot(s, w_ref[0].astype(BF16)) + b_ref[0]


def _ada_all(c_all, w_ada, b_ada):
    n, d, n3 = w_ada.shape
    r = c_all.shape[0]
    tn = _tile(n3, 1024)
    return pl.pallas_call(
        _ada_kernel,
        grid=(n, n3 // tn),
        in_specs=[
            pl.BlockSpec((r, d), lambda i, j: (0, 0)),
            pl.BlockSpec((1, d, tn), lambda i, j: (i, 0, j)),
            pl.BlockSpec((1, 1, tn), lambda i, j: (i, 0, j)),
        ],
        out_specs=pl.BlockSpec((1, r, tn), lambda i, j: (i, 0, j)),
        out_shape=jax.ShapeDtypeStruct((n, r, n3), F32),
        compiler_params=_params(2, 40),
    )(c_all, w_ada, b_ada)


def _norm_mm_kernel(*refs, has_mod, n_out, emit):
    if has_mod:
        x_ref, g_ref, sh_ref, sc_ref, w_ref = refs[:5]
        rest = refs[5:]
    else:
        x_ref, g_ref, w_ref = refs[:3]
        rest = refs[3:]
    o_refs, h_ref = rest[:n_out], rest[-1]

    @pl.when(pl.program_id(1) == 0)
    def _():
        h = _rms(x_ref[...], g_ref[...])
        if has_mod:
            h = h * (1.0 + sc_ref[0]) + sh_ref[0]
        h_ref[...] = h.astype(BF16)

    w = w_ref[0].astype(BF16)
    if emit:
        rest[n_out][0] = w
    y = _dot(h_ref[...], w)
    for o_ref in o_refs:
        o_ref[...] = y.astype(o_ref.dtype)


def _norm_mm(x, g, mod, w, layer, *, rows_per_mod, n_cols=None, col_off=0, out_dtypes=(F32,),
             tm_pref=1024, tn_pref=1024):
    m, d = x.shape
    n_cols = w.shape[2] if n_cols is None else n_cols
    tm = _tile(m, tm_pref)
    tn = _tile(n_cols, tn_pref)
    assert col_off % tn == 0
    joff = col_off // tn
    emit = w.dtype != BF16
    assert not emit or (col_off == 0 and n_cols == w.shape[2])
    in_specs = [pl.BlockSpec((tm, d), lambda i, j: (i, 0)), pl.BlockSpec((1, d), lambda i, j: (0, 0))]
    args = [x, g]
    if mod is not None:
        r = mod.shape[1]
        assert rows_per_mod % tm == 0 and r in (1, tm)
        bidx = lambda i: (i * tm) // rows_per_mod
        in_specs += [pl.BlockSpec((1, r, d), lambda i, j: (bidx(i), 0, 0)),
                     pl.BlockSpec((1, r, d), lambda i, j: (bidx(i), 0, 1))]
        args += [mod, mod]
    in_specs.append(pl.BlockSpec((1, d, tn), lambda i, j: (layer, 0, j + joff)))
    args.append(w)
    out_specs = [pl.BlockSpec((tm, tn), lambda i, j: (i, j)) for _ in out_dtypes]
    out_shape = [jax.ShapeDtypeStruct((m, n_cols), dt) for dt in out_dtypes]
    if emit:
        out_specs.append(pl.BlockSpec((1, d, tn), lambda i, j: (0, 0, j)))
        out_shape.append(jax.ShapeDtypeStruct((1, d, n_cols), BF16))
    outs = pl.pallas_call(
        functools.partial(_norm_mm_kernel, has_mod=mod is not None, n_out=len(out_dtypes), emit=emit),
        grid=(m // tm, n_cols // tn),
        in_specs=in_specs,
        out_specs=out_specs,
        out_shape=out_shape,
        scratch_shapes=[pltpu.VMEM((tm, d), BF16)],
        compiler_params=_params(2, 56),
    )(*args)
    return outs if len(outs) > 1 else outs[0]


def _mm_post_kernel(a_ref, w_ref, x_ref, gate_ref, g_ref, o_ref):
    y = _dot(a_ref[...].astype(BF16), w_ref[...])
    o_ref[...] = x_ref[...] + gate_ref[0] * _rms(y, g_ref[...])


def _mm_post(a, w, layer, x, mod, g, *, rows_per_mod, tm_pref=512):
    m, kdim = a.shape
    d = w.shape[2]
    tm = _tile(m, tm_pref)
    r = mod.shape[1]
    assert rows_per_mod % tm == 0 and r in (1, tm)
    bidx = lambda i: (i * tm) // rows_per_mod
    return pl.pallas_call(
        _mm_post_kernel,
        grid=(m // tm,),
        in_specs=[
            pl.BlockSpec((tm, kdim), lambda i: (i, 0)),
            pl.BlockSpec((None, kdim, d), lambda i: (layer, 0, 0)),
            pl.BlockSpec((tm, d), lambda i: (i, 0)),
            pl.BlockSpec((1, r, d), lambda i: (bidx(i), 0, 2)),
            pl.BlockSpec((1, d), lambda i: (0, 0)),
        ],
        out_specs=pl.BlockSpec((tm, d), lambda i: (i, 0)),
        out_shape=jax.ShapeDtypeStruct((m, d), F32),
        compiler_params=_params(1, 56),
    )(a, w, x, mod, g)


def _mlp_kernel(*refs, emit):
    x_ref, gpre_ref, sh_ref, sc_ref, gate_ref, w1_ref, w2_ref, gpost_ref, o_ref = refs[:9]
    h_ref, acc_ref = refs[-2:]
    f = pl.program_id(1)

    @pl.when(f == 0)
    def _():
        h = _rms(x_ref[...], gpre_ref[...]) * (1.0 + sc_ref[0]) + sh_ref[0]
        h_ref[...] = h.astype(BF16)
        acc_ref[...] = jnp.zeros_like(acc_ref)

    w1 = w1_ref[0].astype(BF16)
    w2 = w2_ref[0].astype(BF16)
    if emit:
        w1b_ref, w2b_ref = refs[9:11]
        w1b_ref[0] = w1
        w2b_ref[0] = w2
    u = jnp.maximum(_dot(h_ref[...], w1), 0.0)
    acc_ref[...] += _dot((u * u).astype(BF16), w2)

    @pl.when(f == pl.num_programs(1) - 1)
    def _():
        o_ref[...] = x_ref[...] + gate_ref[0] * _rms(acc_ref[...], gpost_ref[...])


def _mlp(x, gpre, mod, w1, w2, layer, gpost, *, rows_per_mod, tm_pref=512, tf_pref=1024):
    m, d = x.shape
    ff = w1.shape[2]
    tm = _tile(m, tm_pref)
    tf = _tile(ff, tf_pref)
    r = mod.shape[1]
    assert rows_per_mod % tm == 0 and r in (1, tm)
    emit = w1.dtype != BF16
    bidx = lambda i: (i * tm) // rows_per_mod
    out_specs = [pl.BlockSpec((tm, d), lambda i, f: (i, 0))]
    out_shape = [jax.ShapeDtypeStruct((m, d), F32)]
    if emit:
        out_specs += [pl.BlockSpec((1, d, tf), lambda i, f: (0, 0, f)),
                      pl.BlockSpec((1, tf, d), lambda i, f: (0, f, 0))]
        out_shape += [jax.ShapeDtypeStruct((1, d, ff), BF16), jax.ShapeDtypeStruct((1, ff, d), BF16)]
    outs = pl.pallas_call(
        functools.partial(_mlp_kernel, emit=emit),
        grid=(m // tm, ff // tf),
        in_specs=[
            pl.BlockSpec((tm, d), lambda i, f: (i, 0)),
            pl.BlockSpec((1, d), lambda i, f: (0, 0)),
            pl.BlockSpec((1, r, d), lambda i, f: (bidx(i), 0, 0)),
            pl.BlockSpec((1, r, d), lambda i, f: (bidx(i), 0, 1)),
            pl.BlockSpec((1, r, d), lambda i, f: (bidx(i), 0, 2)),
            pl.BlockSpec((1, d, tf), lambda i, f: (layer, 0, f)),
            pl.BlockSpec((1, tf, d), lambda i, f: (layer, f, 0)),
            pl.BlockSpec((1, d), lambda i, f: (0, 0)),
        ],
        out_specs=out_specs,
        out_shape=out_shape,
        scratch_shapes=[pltpu.VMEM((tm, d), BF16), pltpu.VMEM((tm, d), F32)],
        compiler_params=_params(2, 56),
    )(x, gpre, mod, mod, mod, w1, w2, gpost)
    return outs if emit else outs[0]


def _hgrn_kernel(q_ref, f_ref, i_ref, g_ref, lbl_ref, og_ref, s0_ref, o_ref, s_ref, st_ref,
                 *, layer, c, rows, n_inner, hpb):
    sub = HGRN_SUB
    dk = q_ref.shape[-1] // hpb
    step = pl.program_id(2)

    @pl.when(step == 0)
    def _():
        for h in range(hpb):
            st_ref[h] = s0_ref[0, h].T

    z = lbl_ref[...]
    e = jnp.exp(z - jnp.max(z, axis=0, keepdims=True))
    p = e / jnp.sum(e, axis=0, keepdims=True)
    cs = p[0:1]
    for r in range(1, layer + 1):
        cs = cs + p[r:r + 1]
    lb_all = cs - p[0:1]
    og_all = og_ref[...]

    tril = (_iota((c, c), 0) >= _iota((c, c), 1)).astype(BF16)
    lane = _iota((sub, c), 1)
    sub_row = _iota((sub, c), 0)
    scale = dk ** -0.5

    def pad(t, fill=0.0):
        if rows == c:
            return t
        return jnp.concatenate([t, jnp.full((c - rows, t.shape[1]), fill, t.dtype)], axis=0)

    def cumsum_rows(x):
        hi = x.astype(BF16)
        r1 = x - hi.astype(F32)
        mid = r1.astype(BF16)
        lo = (r1 - mid.astype(F32)).astype(BF16)
        return _dot(tril, hi) + _dot(tril, mid) + _dot(tril, lo)

    def head_chunk(h, sl):
        hs = slice(h * dk, (h + 1) * dk)
        lb = lb_all[:, hs]
        lb_floor = jnp.maximum(lb, TINY)
        one_m_lb = 1.0 - lb
        log2_one_m_lb = jnp.log(one_m_lb) * LOG2E
        qz = q_ref[0, sl, hs]
        fz = f_ref[0, sl, hs]
        q = pad(qz * _sigmoid(qz))
        e = jnp.exp(-jnp.abs(fz))
        t = 1.0 + e
        r = 1.0 / t
        sig = jnp.where(fz >= 0.0, r, e * r)
        logf = pad(jnp.log(lb_floor + one_m_lb * sig))
        lk = pad(log2_one_m_lb + (jnp.minimum(-fz, 0.0) - jnp.log(t)) * LOG2E, -jnp.inf)
        v = pad(i_ref[0, sl, hs])
        b2 = cumsum_rows(logf) * LOG2E
        c2 = b2 - lk
        st = st_ref[h]
        inter = _dot_nt((q * jnp.exp2(b2)).astype(BF16), st.astype(BF16))
        blocks = []
        for i in range(c // sub):
            lo = i * sub
            qi, bi, ci = q[lo:lo + sub], b2[lo:lo + sub], c2[lo:lo + sub]
            if i > 0:
                ref_b = bi[0:1]
                qs = (qi * jnp.exp2(bi - ref_b)).astype(BF16)
                kt = jnp.exp2(ref_b - c2[:lo]).astype(BF16)
                kt = jnp.concatenate([kt, jnp.zeros((c - lo, dk), BF16)], axis=0)
                s_i = _dot_nt(qs, kt)
            else:
                s_i = jnp.zeros((sub, c), F32)
            for s in range(sub):
                w = jnp.sum(qi * jnp.exp2(bi - ci[s:s + 1]), axis=-1, keepdims=True)
                s_i = jnp.where(lane == lo + s, w, s_i)
            blocks.append(jnp.where(lane <= lo + sub_row, s_i, 0.0))
        scores = jnp.concatenate(blocks, axis=0).astype(BF16)
        o = (inter + _dot(scores, v.astype(BF16))) * scale
        blast = b2[c - 1:c]
        kd = jnp.exp2(blast - c2).astype(BF16)
        v_sq = v if c == dk else jnp.concatenate([v, jnp.zeros((dk - c, dk), F32)], axis=0)
        kd_sq = kd if c == dk else jnp.concatenate([kd, jnp.zeros((dk - c, dk), BF16)], axis=0)
        st_ref[h] = st * jnp.exp2(blast) + _dot(v_sq.T.astype(BF16), kd_sq)
        gz = g_ref[0, sl, hs]
        o_ref[0, sl, hs] = _rms(o[:rows], og_all[:, hs]) * (gz * _sigmoid(gz))

    def chunk(j, carry):
        if n_inner == 1:
            sl = pl.ds(0, rows)
        else:
            sl = pl.ds(pl.multiple_of(j * c, c), c)
        for h in range(hpb):
            head_chunk(h, sl)
        return carry

    lax.fori_loop(0, n_inner, chunk, 0)

    @pl.when(step == pl.num_programs(2) - 1)
    def _():
        for h in range(hpb):
            s_ref[0, h] = st_ref[h].T


def _hgrn(proj, lb_logits, out_g, s0, *, layer, heads):
    b, l, _ = proj.shape
    dk = s0.shape[2]
    dv = s0.shape[3]
    assert dk == LANES and dv == LANES
    if l >= HGRN_CHUNK:
        assert l % HGRN_CHUNK == 0
        c = rows = HGRN_CHUNK
        n_inner = _tile(l // c, HGRN_CHUNKS_PER_STEP)
        blk = rows * n_inner
    else:
        assert l % SUBLANES == 0
        c = -(-l // BF16_ROWS) * BF16_ROWS
        rows, n_inner, blk = l, 1, l
    nl = lb_logits.shape[0]
    hpb = _tile(heads, HGRN_HEADS_PER_STEP)
    ng = heads // hpb
    col = lambda part: pl.BlockSpec((1, blk, hpb * dk), lambda bi, h, s: (bi, s, part * ng + h))
    o, s_new = pl.pallas_call(
        functools.partial(_hgrn_kernel, layer=layer, c=c, rows=rows, n_inner=n_inner, hpb=hpb),
        grid=(b, ng, l // blk),
        in_specs=[
            col(0), col(1), col(2), col(3),
            pl.BlockSpec((nl, hpb * dk), lambda bi, h, s: (0, h)),
            pl.BlockSpec((1, hpb * dv), lambda bi, h, s: (0, h)),
            pl.BlockSpec((1, hpb, dk, dv), lambda bi, h, s: (bi, h, 0, 0)),
        ],
        out_specs=[
            pl.BlockSpec((1, blk, hpb * dv), lambda bi, h, s: (bi, s, h)),
            pl.BlockSpec((1, hpb, dk, dv), lambda bi, h, s: (bi, h, 0, 0)),
        ],
        out_shape=[jax.ShapeDtypeStruct((b, l, heads * dv), F32),
                   jax.ShapeDtypeStruct((b, heads, dk, dv), F32)],
        scratch_shapes=[pltpu.VMEM((hpb, dv, dk), F32)],
        compiler_params=_params(3, 48),
    )(proj, proj, proj, proj, lb_logits, out_g, s0)
    return o, s_new


def _logf_kernel(x_ref, g_ref, wft_ref, bft_ref, c0_ref, lf_ref, ft_ref, car_ref):
    tm = x_ref.shape[0]

    @pl.when(pl.program_id(1) == 0)
    def _():
        car_ref[...] = c0_ref[0]

    h = _rms(x_ref[...], g_ref[...])
    zt = _dot_nt(wft_ref[...], h, HIGHEST) + bft_ref[...]
    lft = _log_sigmoid(zt)
    lf_ref[0] = lft
    triu = (_iota((tm, tm), 0) <= _iota((tm, tm), 1)).astype(F32)
    ft = _dot(lft, triu, HIGHEST) + car_ref[...]
    ft_ref[0] = ft
    car_ref[...] = ft[:, tm - 1:tm]


def _logf(x, g, wft, bft, c0t, *, batch):
    m, d = x.shape
    l = m // batch
    hh = wft.shape[0]
    tm = _tile(l, 512)
    nt = l // tm
    out = jax.ShapeDtypeStruct((batch, hh, l), F32)
    return pl.pallas_call(
        _logf_kernel,
        grid=(batch, nt),
        in_specs=[
            pl.BlockSpec((tm, d), lambda b, i: (b * nt + i, 0)),
            pl.BlockSpec((1, d), lambda b, i: (0, 0)),
            pl.BlockSpec((hh, d), lambda b, i: (0, 0)),
            pl.BlockSpec((hh, 1), lambda b, i: (0, 0)),
            pl.BlockSpec((1, hh, 1), lambda b, i: (b, 0, 0)),
        ],
        out_specs=[pl.BlockSpec((1, hh, tm), lambda b, i: (b, 0, i)),
                   pl.BlockSpec((1, hh, tm), lambda b, i: (b, 0, i))],
        out_shape=[out, out],
        scratch_shapes=[pltpu.VMEM((hh, 1), F32)],
        compiler_params=_params(2, 32),
    )(x, g, wft, bft, c0t)


def _fox_prompt_kernel(q_ref, k_ref, v_ref, fq_ref, fk_ref, o_ref, m_ref, l_ref, acc_ref, *, tq, td, hpb):
    i = pl.program_id(2)
    hd = q_ref.shape[-1] // hpb
    m_ref[...] = jnp.full_like(m_ref, MASK_VALUE)
    l_ref[...] = jnp.zeros_like(l_ref)
    acc_ref[...] = jnp.zeros_like(acc_ref)
    qs, fqs = [], []
    for h in range(hpb):
        qs.append((q_ref[:, h * hd:(h + 1) * hd].astype(F32) * (hd ** -0.5 * LOG2E)).astype(BF16))
        fqs.append(jnp.broadcast_to(fq_ref[0, h] * LOG2E, (tq, LANES)))

    def tile(j, tk, row_lo, masked):
        nr = tq - row_lo
        rs = slice(row_lo, tq)
        ks = pl.ds(pl.multiple_of(j * tk, tk), tk)
        for h in range(hpb):
            cs = slice(h * hd, (h + 1) * hd)
            s = _dot_nt(qs[h][rs], k_ref[ks, cs])
            fk = fk_ref[0, h, :, ks] * LOG2E
            us = []
            for c in range(tk // LANES):
                ls = slice(c * LANES, (c + 1) * LANES)
                u = s[:, ls] - fk[:, ls]
                if masked:
                    u = jnp.where(_iota((nr, LANES), 0) >= c * LANES + _iota((nr, LANES), 1), u, MASK_VALUE)
                us.append(u)
            row_max = jnp.max(functools.reduce(jnp.maximum, us), axis=-1, keepdims=True)
            fq = fqs[h][rs]
            m_old = m_ref[h, rs]
            m_new = jnp.maximum(m_old, jnp.broadcast_to(row_max, (nr, LANES)) + fq)
            shift = fq - m_new
            ps = [jnp.exp2(u + shift) for u in us]
            alpha = jnp.exp2(m_old - m_new)
            m_ref[h, rs] = m_new
            l_ref[h, rs] = alpha * l_ref[h, rs] + functools.reduce(jnp.add, ps)
            p_all = jnp.concatenate([p.astype(BF16) for p in ps], axis=1)
            acc_ref[h, rs] = alpha * acc_ref[h, rs] + _dot(p_all, v_ref[ks, cs])

    def body(j, carry):
        tile(j, tq, 0, False)
        return carry

    lax.fori_loop(0, i, body, 0)
    dpq = tq // td
    for mt in range(dpq):
        tile(i * dpq + mt, td, mt * td, True)
    for h in range(hpb):
        o_ref[:, h * hd:(h + 1) * hd] = acc_ref[h] / jnp.sum(l_ref[h], axis=-1, keepdims=True)


def _fox_prompt(q, k, v, ft, *, batch, heads):
    m, d = q.shape
    l = m // batch
    hd = d // heads
    assert hd == LANES
    hpb = ATTN_HEADS_PER_STEP
    tq = _tile(l, ATTN_Q_TILE)
    td = _tile(tq, ATTN_DIAG_TILE)
    nq = l // tq
    fcol = ft[..., None]
    frow = ft[:, :, None, :]
    return pl.pallas_call(
        functools.partial(_fox_prompt_kernel, tq=tq, td=td, hpb=hpb),
        grid=(batch, heads // hpb, nq),
        in_specs=[
            pl.BlockSpec((tq, hpb * hd), lambda b, h, i: (b * nq + i, h)),
            pl.BlockSpec((l, hpb * hd), lambda b, h, i: (b, h)),
            pl.BlockSpec((l, hpb * hd), lambda b, h, i: (b, h)),
            pl.BlockSpec((1, hpb, tq, 1), lambda b, h, i: (b, h, i, 0)),
            pl.BlockSpec((1, hpb, 1, l), lambda b, h, i: (b, h, 0, 0)),
        ],
        out_specs=pl.BlockSpec((tq, hpb * hd), lambda b, h, i: (b * nq + i, h)),
        out_shape=jax.ShapeDtypeStruct((m, d), F32),
        scratch_shapes=[pltpu.VMEM((hpb, tq, LANES), F32), pltpu.VMEM((hpb, tq, LANES), F32),
                        pltpu.VMEM((hpb, tq, hd), F32)],
        compiler_params=_params(3, 56),
    )(q, k, v, fcol, frow)


def _fpast_kernel(pt_ref, lf_ref, o_ref, car_ref, *, n_pg):
    b, g = pl.program_id(0), pl.program_id(1)
    n_grp, prow = lf_ref.shape[1], lf_ref.shape[2]
    rows = n_pg * prow
    keys_per_row = LANES // HEAD_GROUP

    @pl.when(g == 0)
    def _():
        car_ref[...] = jnp.zeros_like(car_ref)

    li, lj = _iota((LANES, LANES), 0), _iota((LANES, LANES), 1)
    same_head = (li % HEAD_GROUP) == (lj % HEAD_GROUP)
    prefix = (same_head & (li // HEAD_GROUP <= lj // HEAD_GROUP)).astype(F32)
    last = (same_head & (li // HEAD_GROUP == keys_per_row - 1)).astype(F32)
    strict = (_iota((rows, rows), 0) > _iota((rows, rows), 1)).astype(F32)
    for a in range(n_grp):
        x = jnp.concatenate([lf_ref[pt_ref[b, g * n_pg + o], a] for o in range(n_pg)], axis=0)
        y = _dot(x, prefix, HIGHEST)
        tot = _dot(y, last, HIGHEST)
        o_ref[0, a] = y + _dot(strict, tot, HIGHEST) + car_ref[a]
        car_ref[a] += jnp.sum(tot, axis=0, keepdims=True)


def _fpast(page_table, cache_logf_grp):
    b, n_pages = page_table.shape
    n_phys, n_grp, prow, _ = cache_logf_grp.shape
    n_pg = _tile(n_pages, FPAST_PAGES_PER_STEP)
    return pl.pallas_call(
        functools.partial(_fpast_kernel, n_pg=n_pg),
        grid_spec=pltpu.PrefetchScalarGridSpec(
            num_scalar_prefetch=1,
            grid=(b, n_pages // n_pg),
            in_specs=[pl.BlockSpec((n_phys, n_grp, prow, LANES), lambda bi, g, pt: (0, 0, 0, 0))],
            out_specs=pl.BlockSpec((1, n_grp, n_pg * prow, LANES), lambda bi, g, pt: (bi, 0, g, 0)),
            scratch_shapes=[pltpu.VMEM((n_grp, 1, LANES), F32)],
        ),
        out_shape=jax.ShapeDtypeStruct((b, n_grp, n_pages * prow, LANES), F32),
        compiler_params=_params(2, 40),
    )(page_table, cache_logf_grp)


def _fox_sample_kernel(pt_ref, q_ref, *refs, heads, n_pg):
    step, nsteps = pl.program_id(1), pl.num_programs(1)
    kc_refs, vc_refs = refs[:n_pg], refs[n_pg:2 * n_pg]
    fp_ref, kn_ref, vn_ref, fq_ref, fn_ref, o_ref, qa_ref, m_ref, l_ref, acc_ref = refs[2 * n_pg:]
    t, d = q_ref.shape
    hd = d // heads
    grp = HEAD_GROUP
    n_grp = heads // grp
    r = grp * t
    page = kc_refs[0].shape[1]
    prow = page * grp // LANES

    def pre():
        @pl.when(step == 0)
        def _():
            q = q_ref[...] * (hd ** -0.5 * LOG2E)
            for a in range(n_grp):
                qa_ref[a] = jnp.concatenate(
                    [q[:, (a * grp + j) * hd:(a * grp + j + 1) * hd] for j in range(grp)], axis=0)
            m_ref[...] = jnp.full_like(m_ref, MASK_VALUE)
            l_ref[...] = jnp.zeros_like(l_ref)
            acc_ref[...] = jnp.zeros_like(acc_ref)

    head_match = (_iota((r, LANES), 0) // t) == (_iota((r, LANES), 1) % grp)

    def update(a, k_list, v_list, fk_list, mask):
        fq = jnp.broadcast_to(fq_ref[0, a] * LOG2E, (r, LANES))
        qa = qa_ref[a]
        us = []
        for k_flat, fk in zip(k_list, fk_list):
            s = _dot_nt(qa, k_flat)
            for c in range(k_flat.shape[0] // LANES):
                u = s[:, c * LANES:(c + 1) * LANES] - fk[c:c + 1] * LOG2E
                us.append(jnp.where(mask, u, MASK_VALUE))
        row_max = jnp.max(functools.reduce(jnp.maximum, us), axis=-1, keepdims=True)
        m_old = m_ref[a]
        m_new = jnp.maximum(m_old, jnp.broadcast_to(row_max, (r, LANES)) + fq)
        shift = fq - m_new
        ps = [jnp.exp2(u + shift) for u in us]
        alpha = jnp.exp2(m_old - m_new)
        m_ref[a] = m_new
        l_ref[a] = alpha * l_ref[a] + functools.reduce(jnp.add, ps)
        pv, at = None, 0
        for v_flat in v_list:
            n_t = v_flat.shape[0] // LANES
            p_pg = jnp.concatenate(ps[at:at + n_t], axis=1) if n_t > 1 else ps[at]
            at += n_t
            part = _dot(p_pg, v_flat)
            pv = part if pv is None else pv + part
        acc_ref[a] = alpha * acc_ref[a] + pv

    def group_rows(ref, a):
        blk = ref[0, :, a * grp:(a + 1) * grp, :]
        return blk.reshape(blk.shape[0] * grp, hd)

    def main():
        for a in range(n_grp):
            update(a, [group_rows(kc, a) for kc in kc_refs], [group_rows(vc, a) for vc in vc_refs],
                   [fp_ref[0, a, o * prow:(o + 1) * prow] for o in range(n_pg)], head_match)

    def post():
        @pl.when(step == nsteps - 1)
        def _():
            lane = _iota((r, LANES), 1)
            new_mask = head_match & (lane // grp <= _iota((r, LANES), 0) % t) & (lane < t * grp)
            zeros = jnp.zeros((LANES - t * grp, hd), F32)
            for a in range(n_grp):
                kn = jnp.concatenate([group_rows(kn_ref, a), zeros], axis=0)
                vn = jnp.concatenate([group_rows(vn_ref, a), zeros], axis=0)
                update(a, [kn], [vn], [fn_ref[0, a]], new_mask)
                res = acc_ref[a] / jnp.sum(l_ref[a], axis=-1, keepdims=True)
                for j in range(grp):
                    h = a * grp + j
                    o_ref[:, h * hd:(h + 1) * hd] = res[j * t:(j + 1) * t]

    pre()
    main()
    post()


def _fox_sample(q, cache_k, cache_v, page_table, fpast, k_new, v_new, fq_col, fnew_flat):
    b, n_pages = page_table.shape
    m, d = q.shape
    t = m // b
    _, page, heads, hd = cache_k.shape
    grp = HEAD_GROUP
    n_grp = heads // grp
    assert hd == LANES and heads % grp == 0 and t * grp <= LANES and (page * grp) % LANES == 0
    r = grp * t
    n_pg = _tile(n_pages, PAGES_PER_STEP)
    prow = page * grp // LANES
    page_spec = lambda o: pl.BlockSpec((1, page, heads, hd), lambda bi, s, pt: (pt[bi, s * n_pg + o], 0, 0, 0))
    new_spec = pl.BlockSpec((1, t, heads, hd), lambda bi, s, pt: (bi, 0, 0, 0))
    in_specs = ([pl.BlockSpec((t, d), lambda bi, s, pt: (bi, 0))]
                + [page_spec(o) for o in range(n_pg)] * 2
                + [pl.BlockSpec((1, n_grp, n_pg * prow, LANES), lambda bi, s, pt: (bi, 0, s, 0)),
                   new_spec, new_spec,
                   pl.BlockSpec((1, n_grp, r, 1), lambda bi, s, pt: (bi, 0, 0, 0)),
                   pl.BlockSpec((1, n_grp, 1, LANES), lambda bi, s, pt: (bi, 0, 0, 0))])
    return pl.pallas_call(
        functools.partial(_fox_sample_kernel, heads=heads, n_pg=n_pg),
        grid_spec=pltpu.PrefetchScalarGridSpec(
            num_scalar_prefetch=1,
            grid=(b, n_pages // n_pg),
            in_specs=in_specs,
            out_specs=pl.BlockSpec((t, d), lambda bi, s, pt: (bi, 0)),
            scratch_shapes=[pltpu.VMEM((n_grp, r, hd), F32), pltpu.VMEM((n_grp, r, LANES), F32),
                            pltpu.VMEM((n_grp, r, LANES), F32), pltpu.VMEM((n_grp, r, hd), F32)],
        ),
        out_shape=jax.ShapeDtypeStruct((m, d), F32),
        compiler_params=_params(2, 56),
    )(page_table, q, *([cache_k] * n_pg), *([cache_v] * n_pg), fpast, k_new, v_new, fq_col, fnew_flat)


def _trunk(x3, mods, s0, attend_ctx, wts):
    (norm_pre, norm_post, a_w_in, a_w_out, a_lb_logits, a_out_norm, kv_norm, w_kv, wft, b_forget,
     b_w_q, b_w_o, mlp_w_in, mlp_w_out) = wts
    b, l, d = x3.shape
    m = b * l
    depth = norm_pre.shape[0]
    n_a = a_w_out.shape[0]
    a_heads = s0.shape[2]
    pick = lambda w, layer: (w[layer], 0) if isinstance(w, list) else (w, layer)
    casts = {"a_w_in": [], "mlp_w_in": [], "mlp_w_out": []}
    b_heads = b_forget.shape[0]
    x = x3.reshape(m, d)
    if m <= 512:
        rows_per_mod = m
        expand = lambda t: jnp.repeat(t, l, axis=0)[None]
    else:
        rows_per_mod = l
        expand = lambda t: t[:, None, :]
    row = lambda t: t.reshape(1, -1)
    states = []
    shared = None
    for layer in range(depth):
        mod = expand(mods[layer, 0])
        if layer < n_a:
            w_in, w_idx = pick(a_w_in, layer)
            proj = _norm_mm(x, row(norm_pre[layer, 0]), mod, w_in, w_idx, rows_per_mod=rows_per_mod)
            if w_in.dtype != BF16:
                proj, w_in_b = proj
                casts["a_w_in"].append(w_in_b)
            o, s_new = _hgrn(proj.reshape(b, l, -1), a_lb_logits, row(a_out_norm[layer]), s0[layer],
                             layer=layer, heads=a_heads)
            states.append(s_new)
            y_in, w_o, w_idx = o.reshape(m, d), a_w_out, layer
        else:
            j = layer - n_a
            if shared is None:
                kvg = row(kv_norm)
                k, k16 = _norm_mm(x, kvg, None, w_kv, 0, rows_per_mod=rows_per_mod, n_cols=d, col_off=0,
                                  out_dtypes=(F32, BF16))
                v, v16 = _norm_mm(x, kvg, None, w_kv, 0, rows_per_mod=rows_per_mod, n_cols=d, col_off=d,
                                  out_dtypes=(F32, BF16))
                shared = attend_ctx["make"](x, kvg, wft, b_forget.reshape(b_heads, 1), k, v, k16, v16)
            q = _norm_mm(x, row(norm_pre[layer, 0]), mod, b_w_q, j, rows_per_mod=rows_per_mod,
                         out_dtypes=(attend_ctx["q_dtype"],))
            y_in, w_o, w_idx = attend_ctx["attend"](q, shared), b_w_o, j
        x = _mm_post(y_in, w_o, w_idx, x, mod, row(norm_post[layer, 0]), rows_per_mod=rows_per_mod)
        mod = expand(mods[layer, 1])
        w1, w_idx = pick(mlp_w_in, layer)
        w2, _ = pick(mlp_w_out, layer)
        x = _mlp(x, row(norm_pre[layer, 1]), mod, w1, w2, w_idx, row(norm_post[layer, 1]),
                 rows_per_mod=rows_per_mod)
        if w1.dtype != BF16:
            x, w1_b, w2_b = x
            casts["mlp_w_in"].append(w1_b)
            casts["mlp_w_out"].append(w2_b)
    k, v, lft = shared["rows"]
    kv_shape = (b, l, b_heads, d // b_heads)
    return (x.reshape(b, l, d), jnp.stack(states), k.reshape(kv_shape), v.reshape(kv_shape),
            jnp.swapaxes(lft, 1, 2), casts)


def kernel(x_prompt, x_sample, c_prompt, c_sample, state_hgrn, cache_k, cache_v, cache_logf, page_table,
           norm_pre, norm_post, w_ada, b_ada, a_w_in, a_w_out, a_lb_logits, a_out_norm,
           kv_norm, w_kv, b_forget, b_w_q, b_w_o, mlp_w_in, mlp_w_out):
    bp, _, d = x_prompt.shape
    bs, ts, _ = x_sample.shape
    depth = norm_pre.shape[0]
    n_phys, page, heads, hd = cache_k.shape
    wts_s = (norm_pre, norm_post, a_w_in, a_w_out.astype(BF16), a_lb_logits, a_out_norm, kv_norm,
             w_kv.astype(BF16)[None], w_kv[:, 2 * d:].T, b_forget,
             b_w_q.astype(BF16), b_w_o.astype(BF16), mlp_w_in, mlp_w_out)

    n_c = bp + bs
    c_rows = -(-n_c // 8) * 8
    c_all = jnp.concatenate([c_prompt, c_sample, jnp.zeros((c_rows - n_c, d), F32)], axis=0)
    mods = _ada_all(c_all, w_ada.reshape(depth * 2, d, 3 * d), b_ada.reshape(depth * 2, 1, 3 * d))
    mods = mods.reshape(depth, 2, c_rows, 3 * d)
    mods_p, mods_s = mods[:, :, :bp], mods[:, :, bp:n_c]

    grp = HEAD_GROUP
    n_grp = heads // grp
    prow = page * grp // LANES
    lf_grp = cache_logf.astype(F32).reshape(n_phys, page, n_grp, grp).transpose(0, 2, 1, 3)
    fpast = _fpast(page_table, lf_grp.reshape(n_phys, n_grp, prow, LANES))
    f_total = fpast[:, :, -1, LANES - grp:].reshape(bs, heads, 1)

    def make_sample(x, kvg, wft, bft, k, v, k16, v16):
        lft, fnew_t = _logf(x, kvg, wft, bft, f_total, batch=bs)
        fq_col = fnew_t.reshape(bs, n_grp, grp * ts, 1)
        fnew_flat = fnew_t.reshape(bs, n_grp, grp, ts).transpose(0, 1, 3, 2).reshape(bs, n_grp, 1, ts * grp)
        fnew_flat = jnp.pad(fnew_flat, ((0, 0), (0, 0), (0, 0), (0, LANES - ts * grp)))
        shape4 = (bs, ts, heads, hd)
        return {"rows": (k, v, lft), "k": k.reshape(shape4), "v": v.reshape(shape4), "fq": fq_col, "fn": fnew_flat}

    def attend_sample(q, sh):
        return _fox_sample(q, cache_k, cache_v, page_table, fpast, sh["k"], sh["v"], sh["fq"], sh["fn"])

    y_s, st_s, k_s, v_s, lf_s, casts = _trunk(
        x_sample, mods_s, state_hgrn, {"make": make_sample, "attend": attend_sample, "q_dtype": F32}, wts_s)

    def make_prompt(x, kvg, wft, bft, k, v, k16, v16):
        lft, ft = _logf(x, kvg, wft, bft, jnp.zeros((bp, heads, 1), F32), batch=bp)
        return {"rows": (k, v, lft), "k": k16, "v": v16, "ft": ft}

    def attend_prompt(q, sh):
        return _fox_prompt(q, sh["k"], sh["v"], sh["ft"], batch=bp, heads=heads)

    wts_p = wts_s[:2] + (casts["a_w_in"],) + wts_s[3:12] + (casts["mlp_w_in"], casts["mlp_w_out"])
    s0_prompt = jnp.zeros((a_w_out.shape[0], bp) + state_hgrn.shape[2:], F32)
    y_p, st_p, k_p, v_p, lf_p, _ = _trunk(
        x_prompt, mods_p, s0_prompt, {"make": make_prompt, "attend": attend_prompt, "q_dtype": BF16}, wts_p)

    return (y_p, y_s, st_p, st_s, k_p, v_p, lf_p, k_s, v_s, lf_s)
```

```python
import functools
import math

import jax
import jax.numpy as jnp
from jax import lax
from jax.experimental import pallas as pl
from jax.experimental.pallas import tpu as pltpu

EPS = 1e-6
MASK_VALUE = -1e30
TINY = 1e-30
LOG2E = math.log2(math.e)
F32 = jnp.float32
BF16 = jnp.bfloat16
HIGHEST = lax.Precision.HIGHEST

LANES = 128
HGRN_CHUNK = 128
HGRN_SUB = 8
HGRN_CHUNKS_PER_STEP = 8
SUBLANES = 8
BF16_ROWS = 16
HEAD_GROUP = SUBLANES
PAGES_PER_STEP = 8
FPAST_PAGES_PER_STEP = 16
HGRN_HEADS_PER_STEP = 8
ATTN_Q_TILE = 1024
ATTN_DIAG_TILE = 512
ATTN_HEADS_PER_STEP = 4
MIB = 2 ** 20


def _params(n_axes, vmem_mib):
    return pltpu.CompilerParams(
        dimension_semantics=("arbitrary",) * n_axes, vmem_limit_bytes=vmem_mib * MIB)


def _tile(n, pref):
    t = min(n, pref)
    while n % t:
        t -= 1
    return t


def _sigmoid(x):
    return 1.0 / (1.0 + jnp.exp(-x))


def _log_sigmoid(x):
    return jnp.minimum(x, 0.0) - jnp.log1p(jnp.exp(-jnp.abs(x)))


def _rms(x, g):
    return x * lax.rsqrt(jnp.mean(x * x, axis=-1, keepdims=True) + EPS) * g


def _dot(a, b, precision=None):
    return jnp.dot(a, b, preferred_element_type=F32, precision=precision)


def _dot_nt(a, b, precision=None):
    return lax.dot_general(a, b, (((1,), (1,)), ((), ())), preferred_element_type=F32, precision=precision)


def _iota(shape, dim):
    return lax.broadcasted_iota(jnp.int32, shape, dim)


def _ada_kernel(c_ref, w_ref, b_ref, o_ref):
    c = c_ref[...]
    s = (c * _sigmoid(c)).astype(BF16)
    o_ref[0] = _dot(s, w_ref[0].astype(BF16)) + b_ref[0]


def _ada_all(c_all, w_ada, b_ada):
    n, d, n3 = w_ada.shape
    r = c_all.shape[0]
    tn = _tile(n3, 1024)
    return pl.pallas_call(
        _ada_kernel,
        grid=(n, n3 // tn),
        in_specs=[
            pl.BlockSpec((r, d), lambda i, j: (0, 0)),
            pl.BlockSpec((1, d, tn), lambda i, j: (i, 0, j)),
            pl.BlockSpec((1, 1, tn), lambda i, j: (i, 0, j)),
        ],
        out_specs=pl.BlockSpec((1, r, tn), lambda i, j: (i, 0, j)),
        out_shape=jax.ShapeDtypeStruct((n, r, n3), F32),
        compiler_params=_params(2, 40),
    )(c_all, w_ada, b_ada)


def _norm_mm_kernel(*refs, has_mod, n_out, emit):
    if has_mod:
        x_ref, g_ref, sh_ref, sc_ref, w_ref = refs[:5]
        rest = refs[5:]
    else:
        x_ref, g_ref, w_ref = refs[:3]
        rest = refs[3:]
    o_refs, h_ref = rest[:n_out], rest[-1]

    @pl.when(pl.program_id(1) == 0)
    def _():
        h = _rms(x_ref[...], g_ref[...])
        if has_mod:
            h = h * (1.0 + sc_ref[0]) + sh_ref[0]
        h_ref[...] = h.astype(BF16)

    w = w_ref[0].astype(BF16)
    if emit:
        rest[n_out][0] = w
    y = _dot(h_ref[...], w)
    for o_ref in o_refs:
        o_ref[...] = y.astype(o_ref.dtype)


def _norm_mm(x, g, mod, w, layer, *, rows_per_mod, out_dtypes=(F32,), tm_pref=1024, tn_pref=1024):
    m, d = x.shape
    n_cols = w.shape[2]
    tm = _tile(m, tm_pref)
    tn = _tile(n_cols, tn_pref)
    emit = w.dtype != BF16
    in_specs = [pl.BlockSpec((tm, d), lambda i, j: (i, 0)), pl.BlockSpec((1, d), lambda i, j: (0, 0))]
    args = [x, g]
    if mod is not None:
        r = mod.shape[1]
        assert rows_per_mod % tm == 0 and r in (1, tm)
        bidx = lambda i: (i * tm) // rows_per_mod
        in_specs += [pl.BlockSpec((1, r, d), lambda i, j: (bidx(i), 0, 0)),
                     pl.BlockSpec((1, r, d), lambda i, j: (bidx(i), 0, 1))]
        args += [mod, mod]
    in_specs.append(pl.BlockSpec((1, d, tn), lambda i, j: (layer, 0, j)))
    args.append(w)
    out_specs = [pl.BlockSpec((tm, tn), lambda i, j: (i, j)) for _ in out_dtypes]
    out_shape = [jax.ShapeDtypeStruct((m, n_cols), dt) for dt in out_dtypes]
    if emit:
        out_specs.append(pl.BlockSpec((1, d, tn), lambda i, j: (0, 0, j)))
        out_shape.append(jax.ShapeDtypeStruct((1, d, n_cols), BF16))
    outs = pl.pallas_call(
        functools.partial(_norm_mm_kernel, has_mod=mod is not None, n_out=len(out_dtypes), emit=emit),
        grid=(m // tm, n_cols // tn),
        in_specs=in_specs,
        out_specs=out_specs,
        out_shape=out_shape,
        scratch_shapes=[pltpu.VMEM((tm, d), BF16)],
        compiler_params=_params(2, 56),
    )(*args)
    return outs if len(outs) > 1 else outs[0]


def _mm_post_kernel(a_ref, w_ref, x_ref, gate_ref, g_ref, o_ref):
    y = _dot(a_ref[...].astype(BF16), w_ref[...])
    o_ref[...] = x_ref[...] + gate_ref[0] * _rms(y, g_ref[...])


def _mm_post(a, w, layer, x, mod, g, *, rows_per_mod, tm_pref=512):
    m, kdim = a.shape
    d = w.shape[2]
    tm = _tile(m, tm_pref)
    r = mod.shape[1]
    assert rows_per_mod % tm == 0 and r in (1, tm)
    bidx = lambda i: (i * tm) // rows_per_mod
    return pl.pallas_call(
        _mm_post_kernel,
        grid=(m // tm,),
        in_specs=[
            pl.BlockSpec((tm, kdim), lambda i: (i, 0)),
            pl.BlockSpec((None, kdim, d), lambda i: (layer, 0, 0)),
            pl.BlockSpec((tm, d), lambda i: (i, 0)),
            pl.BlockSpec((1, r, d), lambda i: (bidx(i), 0, 2)),
            pl.BlockSpec((1, d), lambda i: (0, 0)),
        ],
        out_specs=pl.BlockSpec((tm, d), lambda i: (i, 0)),
        out_shape=jax.ShapeDtypeStruct((m, d), F32),
        compiler_params=_params(1, 56),
    )(a, w, x, mod, g)


def _mlp_kernel(*refs, emit):
    x_ref, gpre_ref, sh_ref, sc_ref, gate_ref, w1_ref, w2_ref, gpost_ref, o_ref = refs[:9]
    h_ref, acc_ref = refs[-2:]
    f = pl.program_id(1)

    @pl.when(f == 0)
    def _():
        h = _rms(x_ref[...], gpre_ref[...]) * (1.0 + sc_ref[0]) + sh_ref[0]
        h_ref[...] = h.astype(BF16)
        acc_ref[...] = jnp.zeros_like(acc_ref)

    w1 = w1_ref[0].astype(BF16)
    w2 = w2_ref[0].astype(BF16)
    if emit:
        w1b_ref, w2b_ref = refs[9:11]
        w1b_ref[0] = w1
        w2b_ref[0] = w2
    u = jnp.maximum(_dot(h_ref[...], w1), 0.0)
    acc_ref[...] += _dot((u * u).astype(BF16), w2)

    @pl.when(f == pl.num_programs(1) - 1)
    def _():
        o_ref[...] = x_ref[...] + gate_ref[0] * _rms(acc_ref[...], gpost_ref[...])


def _mlp(x, gpre, mod, w1, w2, layer, gpost, *, rows_per_mod, tm_pref=512, tf_pref=1024):
    m, d = x.shape
    ff = w1.shape[2]
    tm = _tile(m, tm_pref)
    tf = _tile(ff, tf_pref)
    r = mod.shape[1]
    assert rows_per_mod % tm == 0 and r in (1, tm)
    emit = w1.dtype != BF16
    bidx = lambda i: (i * tm) // rows_per_mod
    out_specs = [pl.BlockSpec((tm, d), lambda i, f: (i, 0))]
    out_shape = [jax.ShapeDtypeStruct((m, d), F32)]
    if emit:
        out_specs += [pl.BlockSpec((1, d, tf), lambda i, f: (0, 0, f)),
                      pl.BlockSpec((1, tf, d), lambda i, f: (0, f, 0))]
        out_shape += [jax.ShapeDtypeStruct((1, d, ff), BF16), jax.ShapeDtypeStruct((1, ff, d), BF16)]
    outs = pl.pallas_call(
        functools.partial(_mlp_kernel, emit=emit),
        grid=(m // tm, ff // tf),
        in_specs=[
            pl.BlockSpec((tm, d), lambda i, f: (i, 0)),
            pl.BlockSpec((1, d), lambda i, f: (0, 0)),
            pl.BlockSpec((1, r, d), lambda i, f: (bidx(i), 0, 0)),
            pl.BlockSpec((1, r, d), lambda i, f: (bidx(i), 0, 1)),
            pl.BlockSpec((1, r, d), lambda i, f: (bidx(i), 0, 2)),
            pl.BlockSpec((1, d, tf), lambda i, f: (layer, 0, f)),
            pl.BlockSpec((1, tf, d), lambda i, f: (layer, f, 0)),
            pl.BlockSpec((1, d), lambda i, f: (0, 0)),
        ],
        out_specs=out_specs,
        out_shape=out_shape,
        scratch_shapes=[pltpu.VMEM((tm, d), BF16), pltpu.VMEM((tm, d), F32)],
        compiler_params=_params(2, 56),
    )(x, gpre, mod, mod, mod, w1, w2, gpost)
    return outs if emit else outs[0]


def _hgrn_kernel(q_ref, f_ref, i_ref, g_ref, lbl_ref, og_ref, s0_ref, o_ref, s_ref, st_ref,
                 *, layer, c, rows, n_inner, hpb):
    sub = HGRN_SUB
    dk = q_ref.shape[-1] // hpb
    step = pl.program_id(2)

    @pl.when(step == 0)
    def _():
        for h in range(hpb):
            st_ref[h] = s0_ref[0, h].T

    z = lbl_ref[...]
    e = jnp.exp(z - jnp.max(z, axis=0, keepdims=True))
    p = e / jnp.sum(e, axis=0, keepdims=True)
    cs = p[0:1]
    for r in range(1, layer + 1):
        cs = cs + p[r:r + 1]
    lb_all = cs - p[0:1]
    og_all = og_ref[...]

    tril = (_iota((c, c), 0) >= _iota((c, c), 1)).astype(BF16)
    lane = _iota((sub, c), 1)
    sub_row = _iota((sub, c), 0)
    scale = dk ** -0.5

    def pad(t, fill=0.0):
        if rows == c:
            return t
        return jnp.concatenate([t, jnp.full((c - rows, t.shape[1]), fill, t.dtype)], axis=0)

    def cumsum_rows(x):
        hi = x.astype(BF16)
        r1 = x - hi.astype(F32)
        mid = r1.astype(BF16)
        lo = (r1 - mid.astype(F32)).astype(BF16)
        return _dot(tril, hi) + _dot(tril, mid) + _dot(tril, lo)

    def head_chunk(h, sl):
        hs = slice(h * dk, (h + 1) * dk)
        lb = lb_all[:, hs]
        lb_floor = jnp.maximum(lb, TINY)
        one_m_lb = 1.0 - lb
        log2_one_m_lb = jnp.log(one_m_lb) * LOG2E
        qz = q_ref[0, sl, hs]
        fz = f_ref[0, sl, hs]
        q = pad(qz * _sigmoid(qz))
        e = jnp.exp(-jnp.abs(fz))
        t = 1.0 + e
        r = 1.0 / t
        sig = jnp.where(fz >= 0.0, r, e * r)
        logf = pad(jnp.log(lb_floor + one_m_lb * sig))
        lk = pad(log2_one_m_lb + (jnp.minimum(-fz, 0.0) - jnp.log(t)) * LOG2E, -jnp.inf)
        v = pad(i_ref[0, sl, hs])
        b2 = cumsum_rows(logf) * LOG2E
        c2 = b2 - lk
        st = st_ref[h]
        inter = _dot_nt((q * jnp.exp2(b2)).astype(BF16), st.astype(BF16))
        blocks = []
        for i in range(c // sub):
            lo = i * sub
            qi, bi, ci = q[lo:lo + sub], b2[lo:lo + sub], c2[lo:lo + sub]
            if i > 0:
                ref_b = bi[0:1]
                qs = (qi * jnp.exp2(bi - ref_b)).astype(BF16)
                kt = jnp.exp2(ref_b - c2[:lo]).astype(BF16)
                kt = jnp.concatenate([kt, jnp.zeros((c - lo, dk), BF16)], axis=0)
                s_i = _dot_nt(qs, kt)
            else:
                s_i = jnp.zeros((sub, c), F32)
            for s in range(sub):
                w = jnp.sum(qi * jnp.exp2(bi - ci[s:s + 1]), axis=-1, keepdims=True)
                s_i = jnp.where(lane == lo + s, w, s_i)
            blocks.append(jnp.where(lane <= lo + sub_row, s_i, 0.0))
        scores = jnp.concatenate(blocks, axis=0).astype(BF16)
        o = (inter + _dot(scores, v.astype(BF16))) * scale
        blast = b2[c - 1:c]
        kd = jnp.exp2(blast - c2).astype(BF16)
        v_sq = v if c == dk else jnp.concatenate([v, jnp.zeros((dk - c, dk), F32)], axis=0)
        kd_sq = kd if c == dk else jnp.concatenate([kd, jnp.zeros((dk - c, dk), BF16)], axis=0)
        st_ref[h] = st * jnp.exp2(blast) + _dot(v_sq.T.astype(BF16), kd_sq)
        gz = g_ref[0, sl, hs]
        o_ref[0, sl, hs] = _rms(o[:rows], og_all[:, hs]) * (gz * _sigmoid(gz))

    def chunk(j, carry):
        if n_inner == 1:
            sl = pl.ds(0, rows)
        else:
            sl = pl.ds(pl.multiple_of(j * c, c), c)
        for h in range(hpb):
            head_chunk(h, sl)
        return carry

    lax.fori_loop(0, n_inner, chunk, 0)

    @pl.when(step == pl.num_programs(2) - 1)
    def _():
        for h in range(hpb):
            s_ref[0, h] = st_ref[h].T


def _hgrn(proj, lb_logits, out_g, s0, *, layer, heads):
    b, l, _ = proj.shape
    dk = s0.shape[2]
    dv = s0.shape[3]
    assert dk == LANES and dv == LANES
    if l >= HGRN_CHUNK:
        assert l % HGRN_CHUNK == 0
        c = rows = HGRN_CHUNK
        n_inner = _tile(l // c, HGRN_CHUNKS_PER_STEP)
        blk = rows * n_inner
    else:
        assert l % SUBLANES == 0
        c = -(-l // BF16_ROWS) * BF16_ROWS
        rows, n_inner, blk = l, 1, l
    nl = lb_logits.shape[0]
    hpb = _tile(heads, HGRN_HEADS_PER_STEP)
    ng = heads // hpb
    col = lambda part: pl.BlockSpec((1, blk, hpb * dk), lambda bi, h, s: (bi, s, part * ng + h))
    o, s_new = pl.pallas_call(
        functools.partial(_hgrn_kernel, layer=layer, c=c, rows=rows, n_inner=n_inner, hpb=hpb),
        grid=(b, ng, l // blk),
        in_specs=[
            col(0), col(1), col(2), col(3),
            pl.BlockSpec((nl, hpb * dk), lambda bi, h, s: (0, h)),
            pl.BlockSpec((1, hpb * dv), lambda bi, h, s: (0, h)),
            pl.BlockSpec((1, hpb, dk, dv), lambda bi, h, s: (bi, h, 0, 0)),
        ],
        out_specs=[
            pl.BlockSpec((1, blk, hpb * dv), lambda bi, h, s: (bi, s, h)),
            pl.BlockSpec((1, hpb, dk, dv), lambda bi, h, s: (bi, h, 0, 0)),
        ],
        out_shape=[jax.ShapeDtypeStruct((b, l, heads * dv), F32),
                   jax.ShapeDtypeStruct((b, heads, dk, dv), F32)],
        scratch_shapes=[pltpu.VMEM((hpb, dv, dk), F32)],
        compiler_params=_params(3, 48),
    )(proj, proj, proj, proj, lb_logits, out_g, s0)
    return o, s_new


def _logf_kernel(x_ref, g_ref, wft_ref, bft_ref, c0_ref, lf_ref, ft_ref, car_ref):
    tm = x_ref.shape[0]

    @pl.when(pl.program_id(1) == 0)
    def _():
        car_ref[...] = c0_ref[0]

    h = _rms(x_ref[...], g_ref[...])
    zt = _dot_nt(wft_ref[...], h, HIGHEST) + bft_ref[...]
    lft = _log_sigmoid(zt)
    lf_ref[0] = lft
    triu = (_iota((tm, tm), 0) <= _iota((tm, tm), 1)).astype(F32)
    ft = _dot(lft, triu, HIGHEST) + car_ref[...]
    ft_ref[0] = ft
    car_ref[...] = ft[:, tm - 1:tm]


def _logf(x, g, wft, bft, c0t, *, batch):
    m, d = x.shape
    l = m // batch
    hh = wft.shape[0]
    tm = _tile(l, 512)
    nt = l // tm
    out = jax.ShapeDtypeStruct((batch, hh, l), F32)
    return pl.pallas_call(
        _logf_kernel,
        grid=(batch, nt),
        in_specs=[
            pl.BlockSpec((tm, d), lambda b, i: (b * nt + i, 0)),
            pl.BlockSpec((1, d), lambda b, i: (0, 0)),
            pl.BlockSpec((hh, d), lambda b, i: (0, 0)),
            pl.BlockSpec((hh, 1), lambda b, i: (0, 0)),
            pl.BlockSpec((1, hh, 1), lambda b, i: (b, 0, 0)),
        ],
        out_specs=[pl.BlockSpec((1, hh, tm), lambda b, i: (b, 0, i)),
                   pl.BlockSpec((1, hh, tm), lambda b, i: (b, 0, i))],
        out_shape=[out, out],
        scratch_shapes=[pltpu.VMEM((hh, 1), F32)],
        compiler_params=_params(2, 32),
    )(x, g, wft, bft, c0t)


def _fox_prompt_kernel(q_ref, k_ref, v_ref, fq_ref, fk_ref, o_ref, m_ref, l_ref, acc_ref, *, tq, td, hpb):
    i = pl.program_id(2)
    hd = q_ref.shape[-1] // hpb
    m_ref[...] = jnp.full_like(m_ref, MASK_VALUE)
    l_ref[...] = jnp.zeros_like(l_ref)
    acc_ref[...] = jnp.zeros_like(acc_ref)
    qs, fqs = [], []
    for h in range(hpb):
        qs.append((q_ref[:, h * hd:(h + 1) * hd].astype(F32) * (hd ** -0.5 * LOG2E)).astype(BF16))
        fqs.append(jnp.broadcast_to(fq_ref[0, h] * LOG2E, (tq, LANES)))

    def tile(j, tk, row_lo, masked):
        nr = tq - row_lo
        rs = slice(row_lo, tq)
        ks = pl.ds(pl.multiple_of(j * tk, tk), tk)
        for h in range(hpb):
            cs = slice(h * hd, (h + 1) * hd)
            s = _dot_nt(qs[h][rs], k_ref[ks, cs])
            fk = fk_ref[0, h, :, ks] * LOG2E
            us = []
            for c in range(tk // LANES):
                ls = slice(c * LANES, (c + 1) * LANES)
                u = s[:, ls] - fk[:, ls]
                if masked:
                    u = jnp.where(_iota((nr, LANES), 0) >= c * LANES + _iota((nr, LANES), 1), u, MASK_VALUE)
                us.append(u)
            row_max = jnp.max(functools.reduce(jnp.maximum, us), axis=-1, keepdims=True)
            fq = fqs[h][rs]
            m_old = m_ref[h, rs]
            m_new = jnp.maximum(m_old, jnp.broadcast_to(row_max, (nr, LANES)) + fq)
            shift = fq - m_new
            ps = [jnp.exp2(u + shift) for u in us]
            alpha = jnp.exp2(m_old - m_new)
            m_ref[h, rs] = m_new
            l_ref[h, rs] = alpha * l_ref[h, rs] + functools.reduce(jnp.add, ps)
            p_all = jnp.concatenate([p.astype(BF16) for p in ps], axis=1)
            acc_ref[h, rs] = alpha * acc_ref[h, rs] + _dot(p_all, v_ref[ks, cs])

    def body(j, carry):
        tile(j, tq, 0, False)
        return carry

    lax.fori_loop(0, i, body, 0)
    dpq = tq // td
    for mt in range(dpq):
        tile(i * dpq + mt, td, mt * td, True)
    for h in range(hpb):
        o_ref[:, h * hd:(h + 1) * hd] = acc_ref[h] / jnp.sum(l_ref[h], axis=-1, keepdims=True)


def _fox_prompt(q, k, v, ft, *, batch, heads):
    m, d = q.shape
    l = m // batch
    hd = d // heads
    assert hd == LANES
    hpb = ATTN_HEADS_PER_STEP
    tq = _tile(l, ATTN_Q_TILE)
    td = _tile(tq, ATTN_DIAG_TILE)
    nq = l // tq
    fcol = ft[..., None]
    frow = ft[:, :, None, :]
    return pl.pallas_call(
        functools.partial(_fox_prompt_kernel, tq=tq, td=td, hpb=hpb),
        grid=(batch, heads // hpb, nq),
        in_specs=[
            pl.BlockSpec((tq, hpb * hd), lambda b, h, i: (b * nq + i, h)),
            pl.BlockSpec((l, hpb * hd), lambda b, h, i: (b, h)),
            pl.BlockSpec((l, hpb * hd), lambda b, h, i: (b, h)),
            pl.BlockSpec((1, hpb, tq, 1), lambda b, h, i: (b, h, i, 0)),
            pl.BlockSpec((1, hpb, 1, l), lambda b, h, i: (b, h, 0, 0)),
        ],
        out_specs=pl.BlockSpec((tq, hpb * hd), lambda b, h, i: (b * nq + i, h)),
        out_shape=jax.ShapeDtypeStruct((m, d), F32),
        scratch_shapes=[pltpu.VMEM((hpb, tq, LANES), F32), pltpu.VMEM((hpb, tq, LANES), F32),
                        pltpu.VMEM((hpb, tq, hd), F32)],
        compiler_params=_params(3, 56),
    )(q, k, v, fcol, frow)


def _fpast_kernel(pt_ref, lf_ref, o_ref, car_ref, *, n_pg):
    b, g = pl.program_id(0), pl.program_id(1)
    n_grp, prow = lf_ref.shape[1], lf_ref.shape[2]
    rows = n_pg * prow
    keys_per_row = LANES // HEAD_GROUP

    @pl.when(g == 0)
    def _():
        car_ref[...] = jnp.zeros_like(car_ref)

    li, lj = _iota((LANES, LANES), 0), _iota((LANES, LANES), 1)
    same_head = (li % HEAD_GROUP) == (lj % HEAD_GROUP)
    prefix = (same_head & (li // HEAD_GROUP <= lj // HEAD_GROUP)).astype(F32)
    last = (same_head & (li // HEAD_GROUP == keys_per_row - 1)).astype(F32)
    strict = (_iota((rows, rows), 0) > _iota((rows, rows), 1)).astype(F32)
    for a in range(n_grp):
        x = jnp.concatenate([lf_ref[pt_ref[b, g * n_pg + o], a] for o in range(n_pg)], axis=0)
        y = _dot(x, prefix, HIGHEST)
        tot = _dot(y, last, HIGHEST)
        o_ref[0, a] = y + _dot(strict, tot, HIGHEST) + car_ref[a]
        car_ref[a] += jnp.sum(tot, axis=0, keepdims=True)


def _fpast(page_table, cache_logf_grp):
    b, n_pages = page_table.shape
    n_phys, n_grp, prow, _ = cache_logf_grp.shape
    n_pg = _tile(n_pages, FPAST_PAGES_PER_STEP)
    return pl.pallas_call(
        functools.partial(_fpast_kernel, n_pg=n_pg),
        grid_spec=pltpu.PrefetchScalarGridSpec(
            num_scalar_prefetch=1,
            grid=(b, n_pages // n_pg),
            in_specs=[pl.BlockSpec((n_phys, n_grp, prow, LANES), lambda bi, g, pt: (0, 0, 0, 0))],
            out_specs=pl.BlockSpec((1, n_grp, n_pg * prow, LANES), lambda bi, g, pt: (bi, 0, g, 0)),
            scratch_shapes=[pltpu.VMEM((n_grp, 1, LANES), F32)],
        ),
        out_shape=jax.ShapeDtypeStruct((b, n_grp, n_pages * prow, LANES), F32),
        compiler_params=_params(2, 40),
    )(page_table, cache_logf_grp)


def _fox_sample_kernel(pt_ref, q_ref, *refs, heads, n_pg):
    step, nsteps = pl.program_id(1), pl.num_programs(1)
    kc_refs, vc_refs = refs[:n_pg], refs[n_pg:2 * n_pg]
    fp_ref, kn_ref, vn_ref, fq_ref, fn_ref, o_ref, qa_ref, m_ref, l_ref, acc_ref = refs[2 * n_pg:]
    t, d = q_ref.shape
    hd = d // heads
    grp = HEAD_GROUP
    n_grp = heads // grp
    r = grp * t
    page = kc_refs[0].shape[1]
    prow = page * grp // LANES

    def pre():
        @pl.when(step == 0)
        def _():
            q = q_ref[...] * (hd ** -0.5 * LOG2E)
            for a in range(n_grp):
                qa_ref[a] = jnp.concatenate(
                    [q[:, (a * grp + j) * hd:(a * grp + j + 1) * hd] for j in range(grp)], axis=0)
            m_ref[...] = jnp.full_like(m_ref, MASK_VALUE)
            l_ref[...] = jnp.zeros_like(l_ref)
            acc_ref[...] = jnp.zeros_like(acc_ref)

    head_match = (_iota((r, LANES), 0) // t) == (_iota((r, LANES), 1) % grp)

    def update(a, k_list, v_list, fk_list, mask):
        fq = jnp.broadcast_to(fq_ref[0, a] * LOG2E, (r, LANES))
        qa = qa_ref[a]
        us = []
        for k_flat, fk in zip(k_list, fk_list):
            s = _dot_nt(qa, k_flat)
            for c in range(k_flat.shape[0] // LANES):
                u = s[:, c * LANES:(c + 1) * LANES] - fk[c:c + 1] * LOG2E
                us.append(jnp.where(mask, u, MASK_VALUE))
        row_max = jnp.max(functools.reduce(jnp.maximum, us), axis=-1, keepdims=True)
        m_old = m_ref[a]
        m_new = jnp.maximum(m_old, jnp.broadcast_to(row_max, (r, LANES)) + fq)
        shift = fq - m_new
        ps = [jnp.exp2(u + shift) for u in us]
        alpha = jnp.exp2(m_old - m_new)
        m_ref[a] = m_new
        l_ref[a] = alpha * l_ref[a] + functools.reduce(jnp.add, ps)
        pv, at = None, 0
        for v_flat in v_list:
            n_t = v_flat.shape[0] // LANES
            p_pg = jnp.concatenate(ps[at:at + n_t], axis=1) if n_t > 1 else ps[at]
            at += n_t
            part = _dot(p_pg, v_flat)
            pv = part if pv is None else pv + part
        acc_ref[a] = alpha * acc_ref[a] + pv

    def group_rows(ref, a):
        blk = ref[0, :, a * grp:(a + 1) * grp, :]
        return blk.reshape(blk.shape[0] * grp, hd)

    def main():
        for a in range(n_grp):
            update(a, [group_rows(kc, a) for kc in kc_refs], [group_rows(vc, a) for vc in vc_refs],
                   [fp_ref[0, a, o * prow:(o + 1) * prow] for o in range(n_pg)], head_match)

    def post():
        @pl.when(step == nsteps - 1)
        def _():
            lane = _iota((r, LANES), 1)
            new_mask = head_match & (lane // grp <= _iota((r, LANES), 0) % t) & (lane < t * grp)
            zeros = jnp.zeros((LANES - t * grp, hd), F32)
            for a in range(n_grp):
                kn = jnp.concatenate([group_rows(kn_ref, a), zeros], axis=0)
                vn = jnp.concatenate([group_rows(vn_ref, a), zeros], axis=0)
                update(a, [kn], [vn], [fn_ref[0, a]], new_mask)
                res = acc_ref[a] / jnp.sum(l_ref[a], axis=-1, keepdims=True)
                for j in range(grp):
                    h = a * grp + j
                    o_ref[:, h * hd:(h + 1) * hd] = res[j * t:(j + 1) * t]

    pre()
    main()
    post()


def _fox_sample(q, cache_k, cache_v, page_table, fpast, k_new, v_new, fq_col, fnew_flat):
    b, n_pages = page_table.shape
    m, d = q.shape
    t = m // b
    _, page, heads, hd = cache_k.shape
    grp = HEAD_GROUP
    n_grp = heads // grp
    assert hd == LANES and heads % grp == 0 and t * grp <= LANES and (page * grp) % LANES == 0
    r = grp * t
    n_pg = _tile(n_pages, PAGES_PER_STEP)
    prow = page * grp // LANES
    page_spec = lambda o: pl.BlockSpec((1, page, heads, hd), lambda bi, s, pt: (pt[bi, s * n_pg + o], 0, 0, 0))
    new_spec = pl.BlockSpec((1, t, heads, hd), lambda bi, s, pt: (bi, 0, 0, 0))
    in_specs = ([pl.BlockSpec((t, d), lambda bi, s, pt: (bi, 0))]
                + [page_spec(o) for o in range(n_pg)] * 2
                + [pl.BlockSpec((1, n_grp, n_pg * prow, LANES), lambda bi, s, pt: (bi, 0, s, 0)),
                   new_spec, new_spec,
                   pl.BlockSpec((1, n_grp, r, 1), lambda bi, s, pt: (bi, 0, 0, 0)),
                   pl.BlockSpec((1, n_grp, 1, LANES), lambda bi, s, pt: (bi, 0, 0, 0))])
    return pl.pallas_call(
        functools.partial(_fox_sample_kernel, heads=heads, n_pg=n_pg),
        grid_spec=pltpu.PrefetchScalarGridSpec(
            num_scalar_prefetch=1,
            grid=(b, n_pages // n_pg),
            in_specs=in_specs,
            out_specs=pl.BlockSpec((t, d), lambda bi, s, pt: (bi, 0)),
            scratch_shapes=[pltpu.VMEM((n_grp, r, hd), F32), pltpu.VMEM((n_grp, r, LANES), F32),
                            pltpu.VMEM((n_grp, r, LANES), F32), pltpu.VMEM((n_grp, r, hd), F32)],
        ),
        out_shape=jax.ShapeDtypeStruct((m, d), F32),
        compiler_params=_params(2, 56),
    )(page_table, q, *([cache_k] * n_pg), *([cache_v] * n_pg), fpast, k_new, v_new, fq_col, fnew_flat)


def _trunk(x3, mods, s0, attend_ctx, wts):
    (norm_pre, norm_post, a_w_in, a_w_out, a_lb_logits, a_out_norm, kv_norm, w_kv, wft, b_forget,
     b_w_q, b_w_o, mlp_w_in, mlp_w_out) = wts
    b, l, d = x3.shape
    m = b * l
    depth = norm_pre.shape[0]
    n_a = a_w_out.shape[0]
    a_heads = s0.shape[2]
    pick = lambda w, layer: (w[layer], 0) if isinstance(w, list) else (w, layer)
    casts = {"a_w_in": [], "mlp_w_in": [], "mlp_w_out": []}
    b_heads = b_forget.shape[0]
    x = x3.reshape(m, d)
    if m <= 512:
        rows_per_mod = m
        expand = lambda t: jnp.repeat(t, l, axis=0)[None]
    else:
        rows_per_mod = l
        expand = lambda t: t[:, None, :]
    row = lambda t: t.reshape(1, -1)
    states = []
    shared = None
    for layer in range(depth):
        mod = expand(mods[layer, 0])
        if layer < n_a:
            w_in, w_idx = pick(a_w_in, layer)
            proj = _norm_mm(x, row(norm_pre[layer, 0]), mod, w_in, w_idx, rows_per_mod=rows_per_mod)
            if w_in.dtype != BF16:
                proj, w_in_b = proj
                casts["a_w_in"].append(w_in_b)
            o, s_new = _hgrn(proj.reshape(b, l, -1), a_lb_logits, row(a_out_norm[layer]), s0[layer],
                             layer=layer, heads=a_heads)
            states.append(s_new)
            y_in, w_o, w_idx = o.reshape(m, d), a_w_out, layer
        else:
            j = layer - n_a
            if shared is None:
                kvg = row(kv_norm)
                k, k16 = _norm_mm(x, kvg, None, w_kv, 0, rows_per_mod=rows_per_mod, out_dtypes=(F32, BF16))
                v, v16 = _norm_mm(x, kvg, None, w_kv, 1, rows_per_mod=rows_per_mod, out_dtypes=(F32, BF16))
                shared = attend_ctx["make"](x, kvg, wft, b_forget.reshape(b_heads, 1), k, v, k16, v16)
            q = _norm_mm(x, row(norm_pre[layer, 0]), mod, b_w_q, j, rows_per_mod=rows_per_mod,
                         out_dtypes=(attend_ctx["q_dtype"],))
            y_in, w_o, w_idx = attend_ctx["attend"](q, shared), b_w_o, j
        x = _mm_post(y_in, w_o, w_idx, x, mod, row(norm_post[layer, 0]), rows_per_mod=rows_per_mod)
        mod = expand(mods[layer, 1])
        w1, w_idx = pick(mlp_w_in, layer)
        w2, _ = pick(mlp_w_out, layer)
        x = _mlp(x, row(norm_pre[layer, 1]), mod, w1, w2, w_idx, row(norm_post[layer, 1]),
                 rows_per_mod=rows_per_mod)
        if w1.dtype != BF16:
            x, w1_b, w2_b = x
            casts["mlp_w_in"].append(w1_b)
            casts["mlp_w_out"].append(w2_b)
    k, v, lft = shared["rows"]
    kv_shape = (b, l, b_heads, d // b_heads)
    return (x.reshape(b, l, d), jnp.stack(states), k.reshape(kv_shape), v.reshape(kv_shape),
            jnp.swapaxes(lft, 1, 2), casts)


def kernel(x_prompt, x_sample, c_prompt, c_sample, state_hgrn, cache_k, cache_v, cache_logf, page_table,
           norm_pre, norm_post, w_ada, b_ada, a_w_in, a_w_out, a_lb_logits, a_out_norm,
           kv_norm, w_kv, b_forget, b_w_q, b_w_o, mlp_w_in, mlp_w_out):
    bp, _, d = x_prompt.shape
    bs, ts, _ = x_sample.shape
    depth = norm_pre.shape[0]
    n_phys, page, heads, hd = cache_k.shape
    wts_s = (norm_pre, norm_post, a_w_in, a_w_out.astype(BF16), a_lb_logits, a_out_norm, kv_norm,
             jnp.stack([w_kv[:, :d], w_kv[:, d:2 * d]]).astype(BF16), w_kv[:, 2 * d:].T, b_forget,
             b_w_q.astype(BF16), b_w_o.astype(BF16), mlp_w_in, mlp_w_out)

    n_c = bp + bs
    c_rows = -(-n_c // 8) * 8
    c_all = jnp.concatenate([c_prompt, c_sample, jnp.zeros((c_rows - n_c, d), F32)], axis=0)
    mods = _ada_all(c_all, w_ada.reshape(depth * 2, d, 3 * d), b_ada.reshape(depth * 2, 1, 3 * d))
    mods = mods.reshape(depth, 2, c_rows, 3 * d)
    mods_p, mods_s = mods[:, :, :bp], mods[:, :, bp:n_c]

    grp = HEAD_GROUP
    n_grp = heads // grp
    prow = page * grp // LANES
    lf_grp = cache_logf.astype(F32).reshape(n_phys, page, n_grp, grp).transpose(0, 2, 1, 3)
    fpast = _fpast(page_table, lf_grp.reshape(n_phys, n_grp, prow, LANES))
    f_total = fpast[:, :, -1, LANES - grp:].reshape(bs, heads, 1)

    def make_sample(x, kvg, wft, bft, k, v, k16, v16):
        lft, fnew_t = _logf(x, kvg, wft, bft, f_total, batch=bs)
        fq_col = fnew_t.reshape(bs, n_grp, grp * ts, 1)
        fnew_flat = fnew_t.reshape(bs, n_grp, grp, ts).transpose(0, 1, 3, 2).reshape(bs, n_grp, 1, ts * grp)
        fnew_flat = jnp.pad(fnew_flat, ((0, 0), (0, 0), (0, 0), (0, LANES - ts * grp)))
        shape4 = (bs, ts, heads, hd)
        return {"rows": (k, v, lft), "k": k.reshape(shape4), "v": v.reshape(shape4), "fq": fq_col, "fn": fnew_flat}

    def attend_sample(q, sh):
        return _fox_sample(q, cache_k, cache_v, page_table, fpast, sh["k"], sh["v"], sh["fq"], sh["fn"])

    y_s, st_s, k_s, v_s, lf_s, casts = _trunk(
        x_sample, mods_s, state_hgrn, {"make": make_sample, "attend": attend_sample, "q_dtype": F32}, wts_s)

    def make_prompt(x, kvg, wft, bft, k, v, k16, v16):
        lft, ft = _logf(x, kvg, wft, bft, jnp.zeros((bp, heads, 1), F32), batch=bp)
        return {"rows": (k, v, lft), "k": k16, "v": v16, "ft": ft}

    def attend_prompt(q, sh):
        return _fox_prompt(q, sh["k"], sh["v"], sh["ft"], batch=bp, heads=heads)

    wts_p = wts_s[:2] + (casts["a_w_in"],) + wts_s[3:12] + (casts["mlp_w_in"], casts["mlp_w_out"])
    s0_prompt = jnp.zeros((a_w_out.shape[0], bp) + state_hgrn.shape[2:], F32)
    y_p, st_p, k_p, v_p, lf_p, _ = _trunk(
        x_prompt, mods_p, s0_prompt, {"make": make_prompt, "attend": attend_prompt, "q_dtype": BF16}, wts_p)

    return (y_p, y_s, st_p, st_s, k_p, v_p, lf_p, k_s, v_s, lf_s)
```

```python
import functools
import math

import jax
import jax.numpy as jnp
from jax import lax
from jax.experimental import pallas as pl
from jax.experimental.pallas import tpu as pltpu

EPS = 1e-6
MASK_VALUE = -1e30
TINY = 1e-30
LOG2E = math.log2(math.e)
F32 = jnp.float32
BF16 = jnp.bfloat16
HIGHEST = lax.Precision.HIGHEST

LANES = 128
HGRN_CHUNK = 128
HGRN_SUB = 8
HGRN_CHUNKS_PER_STEP = 8
SUBLANES = 8
BF16_ROWS = 16
HEAD_GROUP = SUBLANES
PAGES_PER_STEP = 8
FPAST_PAGES_PER_STEP = 16
HGRN_HEADS_PER_STEP = 8
ATTN_Q_TILE = 1024
ATTN_DIAG_TILE = 256
ATTN_HEADS_PER_STEP = 4
MIB = 2 ** 20


def _params(n_axes, vmem_mib):
    return pltpu.CompilerParams(
        dimension_semantics=("arbitrary",) * n_axes, vmem_limit_bytes=vmem_mib * MIB)


def _tile(n, pref):
    t = min(n, pref)
    while n % t:
        t -= 1
    return t


def _sigmoid(x):
    return 1.0 / (1.0 + jnp.exp(-x))


def _log_sigmoid(x):
    return jnp.minimum(x, 0.0) - jnp.log1p(jnp.exp(-jnp.abs(x)))


def _rms(x, g):
    return x * lax.rsqrt(jnp.mean(x * x, axis=-1, keepdims=True) + EPS) * g


def _dot(a, b, precision=None):
    return jnp.dot(a, b, preferred_element_type=F32, precision=precision)


def _dot_nt(a, b, precision=None):
    return lax.dot_general(a, b, (((1,), (1,)), ((), ())), preferred_element_type=F32, precision=precision)


def _iota(shape, dim):
    return lax.broadcasted_iota(jnp.int32, shape, dim)


def _ada_kernel(c_ref, w_ref, b_ref, o_ref):
    c = c_ref[...]
    s = (c * _sigmoid(c)).astype(BF16)
    o_ref[0] = _dot(s, w_ref[0].astype(BF16)) + b_ref[0]


def _ada_all(c_all, w_ada, b_ada):
    n, d, n3 = w_ada.shape
    r = c_all.shape[0]
    tn = _tile(n3, 1024)
    return pl.pallas_call(
        _ada_kernel,
        grid=(n, n3 // tn),
        in_specs=[
            pl.BlockSpec((r, d), lambda i, j: (0, 0)),
            pl.BlockSpec((1, d, tn), lambda i, j: (i, 0, j)),
            pl.BlockSpec((1, 1, tn), lambda i, j: (i, 0, j)),
        ],
        out_specs=pl.BlockSpec((1, r, tn), lambda i, j: (i, 0, j)),
        out_shape=jax.ShapeDtypeStruct((n, r, n3), F32),
        compiler_params=_params(2, 40),
    )(c_all, w_ada, b_ada)


def _norm_mm_kernel(*refs, has_mod, n_out, emit):
    if has_mod:
        x_ref, g_ref, sh_ref, sc_ref, w_ref = refs[:5]
        rest = refs[5:]
    else:
        x_ref, g_ref, w_ref = refs[:3]
        rest = refs[3:]
    o_refs, h_ref = rest[:n_out], rest[-1]

    @pl.when(pl.program_id(1) == 0)
    def _():
        h = _rms(x_ref[...], g_ref[...])
        if has_mod:
            h = h * (1.0 + sc_ref[0]) + sh_ref[0]
        h_ref[...] = h.astype(BF16)

    w = w_ref[0].astype(BF16)
    if emit:
        rest[n_out][0] = w
    y = _dot(h_ref[...], w)
    for o_ref in o_refs:
        o_ref[...] = y.astype(o_ref.dtype)


def _norm_mm(x, g, mod, w, layer, *, rows_per_mod, n_cols=None, col_off=0, out_dtypes=(F32,),
             tm_pref=1024, tn_pref=1024):
    m, d = x.shape
    n_cols = w.shape[2] if n_cols is None else n_cols
    tm = _tile(m, tm_pref)
    tn = _tile(n_cols, tn_pref)
    assert col_off % tn == 0
    joff = col_off // tn
    emit = w.dtype != BF16
    assert not emit or (col_off == 0 and n_cols == w.shape[2])
    in_specs = [pl.BlockSpec((tm, d), lambda i, j: (i, 0)), pl.BlockSpec((1, d), lambda i, j: (0, 0))]
    args = [x, g]
    if mod is not None:
        r = mod.shape[1]
        assert rows_per_mod % tm == 0 and r in (1, tm)
        bidx = lambda i: (i * tm) // rows_per_mod
        in_specs += [pl.BlockSpec((1, r, d), lambda i, j: (bidx(i), 0, 0)),
                     pl.BlockSpec((1, r, d), lambda i, j: (bidx(i), 0, 1))]
        args += [mod, mod]
    in_specs.append(pl.BlockSpec((1, d, tn), lambda i, j: (layer, 0, j + joff)))
    args.append(w)
    out_specs = [pl.BlockSpec((tm, tn), lambda i, j: (i, j)) for _ in out_dtypes]
    out_shape = [jax.ShapeDtypeStruct((m, n_cols), dt) for dt in out_dtypes]
    if emit:
        out_specs.append(pl.BlockSpec((1, d, tn), lambda i, j: (0, 0, j)))
        out_shape.append(jax.ShapeDtypeStruct((1, d, n_cols), BF16))
    outs = pl.pallas_call(
        functools.partial(_norm_mm_kernel, has_mod=mod is not None, n_out=len(out_dtypes), emit=emit),
        grid=(m // tm, n_cols // tn),
        in_specs=in_specs,
        out_specs=out_specs,
        out_shape=out_shape,
        scratch_shapes=[pltpu.VMEM((tm, d), BF16)],
        compiler_params=_params(2, 56),
    )(*args)
    return outs if len(outs) > 1 else outs[0]


def _mm_post_kernel(a_ref, w_ref, x_ref, gate_ref, g_ref, o_ref):
    y = _dot(a_ref[...].astype(BF16), w_ref[...])
    o_ref[...] = x_ref[...] + gate_ref[0] * _rms(y, g_ref[...])


def _mm_post(a, w, layer, x, mod, g, *, rows_per_mod, tm_pref=512):
    m, kdim = a.shape
    d = w.shape[2]
    tm = _tile(m, tm_pref)
    r = mod.shape[1]
    assert rows_per_mod % tm == 0 and r in (1, tm)
    bidx = lambda i: (i * tm) // rows_per_mod
    return pl.pallas_call(
        _mm_post_kernel,
        grid=(m // tm,),
        in_specs=[
            pl.BlockSpec((tm, kdim), lambda i: (i, 0)),
            pl.BlockSpec((None, kdim, d), lambda i: (layer, 0, 0)),
            pl.BlockSpec((tm, d), lambda i: (i, 0)),
            pl.BlockSpec((1, r, d), lambda i: (bidx(i), 0, 2)),
            pl.BlockSpec((1, d), lambda i: (0, 0)),
        ],
        out_specs=pl.BlockSpec((tm, d), lambda i: (i, 0)),
        out_shape=jax.ShapeDtypeStruct((m, d), F32),
        compiler_params=_params(1, 56),
    )(a, w, x, mod, g)


def _mlp_kernel(*refs, emit):
    x_ref, gpre_ref, sh_ref, sc_ref, gate_ref, w1_ref, w2_ref, gpost_ref, o_ref = refs[:9]
    h_ref, acc_ref = refs[-2:]
    f = pl.program_id(1)

    @pl.when(f == 0)
    def _():
        h = _rms(x_ref[...], gpre_ref[...]) * (1.0 + sc_ref[0]) + sh_ref[0]
        h_ref[...] = h.astype(BF16)
        acc_ref[...] = jnp.zeros_like(acc_ref)

    w1 = w1_ref[0].astype(BF16)
    w2 = w2_ref[0].astype(BF16)
    if emit:
        w1b_ref, w2b_ref = refs[9:11]
        w1b_ref[0] = w1
        w2b_ref[0] = w2
    u = jnp.maximum(_dot(h_ref[...], w1), 0.0)
    acc_ref[...] += _dot((u * u).astype(BF16), w2)

    @pl.when(f == pl.num_programs(1) - 1)
    def _():
        o_ref[...] = x_ref[...] + gate_ref[0] * _rms(acc_ref[...], gpost_ref[...])


def _mlp(x, gpre, mod, w1, w2, layer, gpost, *, rows_per_mod, tm_pref=512, tf_pref=1024):
    m, d = x.shape
    ff = w1.shape[2]
    tm = _tile(m, tm_pref)
    tf = _tile(ff, tf_pref)
    r = mod.shape[1]
    assert rows_per_mod % tm == 0 and r in (1, tm)
    emit = w1.dtype != BF16
    bidx = lambda i: (i * tm) // rows_per_mod
    out_specs = [pl.BlockSpec((tm, d), lambda i, f: (i, 0))]
    out_shape = [jax.ShapeDtypeStruct((m, d), F32)]
    if emit:
        out_specs += [pl.BlockSpec((1, d, tf), lambda i, f: (0, 0, f)),
                      pl.BlockSpec((1, tf, d), lambda i, f: (0, f, 0))]
        out_shape += [jax.ShapeDtypeStruct((1, d, ff), BF16), jax.ShapeDtypeStruct((1, ff, d), BF16)]
    outs = pl.pallas_call(
        functools.partial(_mlp_kernel, emit=emit),
        grid=(m // tm, ff // tf),
        in_specs=[
            pl.BlockSpec((tm, d), lambda i, f: (i, 0)),
            pl.BlockSpec((1, d), lambda i, f: (0, 0)),
            pl.BlockSpec((1, r, d), lambda i, f: (bidx(i), 0, 0)),
            pl.BlockSpec((1, r, d), lambda i, f: (bidx(i), 0, 1)),
            pl.BlockSpec((1, r, d), lambda i, f: (bidx(i), 0, 2)),
            pl.BlockSpec((1, d, tf), lambda i, f: (layer, 0, f)),
            pl.BlockSpec((1, tf, d), lambda i, f: (layer, f, 0)),
            pl.BlockSpec((1, d), lambda i, f: (0, 0)),
        ],
        out_specs=out_specs,
        out_shape=out_shape,
        scratch_shapes=[pltpu.VMEM((tm, d), BF16), pltpu.VMEM((tm, d), F32)],
        compiler_params=_params(2, 56),
    )(x, gpre, mod, mod, mod, w1, w2, gpost)
    return outs if emit else outs[0]


def _hgrn_kernel(q_ref, f_ref, i_ref, g_ref, lbl_ref, og_ref, s0_ref, o_ref, s_ref, st_ref,
                 *, layer, c, rows, n_inner, hpb):
    sub = HGRN_SUB
    dk = q_ref.shape[-1] // hpb
    step = pl.program_id(2)

    @pl.when(step == 0)
    def _():
        for h in range(hpb):
            st_ref[h] = s0_ref[0, h].T

    z = lbl_ref[...]
    e = jnp.exp(z - jnp.max(z, axis=0, keepdims=True))
    p = e / jnp.sum(e, axis=0, keepdims=True)
    cs = p[0:1]
    for r in range(1, layer + 1):
        cs = cs + p[r:r + 1]
    lb_all = cs - p[0:1]
    og_all = og_ref[...]

    tril = (_iota((c, c), 0) >= _iota((c, c), 1)).astype(BF16)
    lane = _iota((sub, c), 1)
    sub_row = _iota((sub, c), 0)
    scale = dk ** -0.5

    def pad(t, fill=0.0):
        if rows == c:
            return t
        return jnp.concatenate([t, jnp.full((c - rows, t.shape[1]), fill, t.dtype)], axis=0)

    def cumsum_rows(x):
        hi = x.astype(BF16)
        r1 = x - hi.astype(F32)
        mid = r1.astype(BF16)
        lo = (r1 - mid.astype(F32)).astype(BF16)
        return _dot(tril, hi) + _dot(tril, mid) + _dot(tril, lo)

    def head_chunk(h, sl):
        hs = slice(h * dk, (h + 1) * dk)
        lb = lb_all[:, hs]
        lb_floor = jnp.maximum(lb, TINY)
        one_m_lb = 1.0 - lb
        log2_one_m_lb = jnp.log(one_m_lb) * LOG2E
        qz = q_ref[0, sl, hs]
        fz = f_ref[0, sl, hs]
        q = pad(qz * _sigmoid(qz))
        e = jnp.exp(-jnp.abs(fz))
        t = 1.0 + e
        r = 1.0 / t
        sig = jnp.where(fz >= 0.0, r, e * r)
        logf = pad(jnp.log(lb_floor + one_m_lb * sig))
        lk = pad(log2_one_m_lb + (jnp.minimum(-fz, 0.0) - jnp.log(t)) * LOG2E, -jnp.inf)
        v = pad(i_ref[0, sl, hs])
        b2 = cumsum_rows(logf) * LOG2E
        c2 = b2 - lk
        st = st_ref[h]
        inter = _dot_nt((q * jnp.exp2(b2)).astype(BF16), st.astype(BF16))
        blocks = []
        for i in range(c // sub):
            lo = i * sub
            qi, bi, ci = q[lo:lo + sub], b2[lo:lo + sub], c2[lo:lo + sub]
            if i > 0:
                ref_b = bi[0:1]
                qs = (qi * jnp.exp2(bi - ref_b)).astype(BF16)
                kt = jnp.exp2(ref_b - c2[:lo]).astype(BF16)
                kt = jnp.concatenate([kt, jnp.zeros((c - lo, dk), BF16)], axis=0)
                s_i = _dot_nt(qs, kt)
            else:
                s_i = jnp.zeros((sub, c), F32)
            for s in range(sub):
                w = jnp.sum(qi * jnp.exp2(bi - ci[s:s + 1]), axis=-1, keepdims=True)
                s_i = jnp.where(lane == lo + s, w, s_i)
            blocks.append(jnp.where(lane <= lo + sub_row, s_i, 0.0))
        scores = jnp.concatenate(blocks, axis=0).astype(BF16)
        o = (inter + _dot(scores, v.astype(BF16))) * scale
        blast = b2[c - 1:c]
        kd = jnp.exp2(blast - c2).astype(BF16)
        v_sq = v if c == dk else jnp.concatenate([v, jnp.zeros((dk - c, dk), F32)], axis=0)
        kd_sq = kd if c == dk else jnp.concatenate([kd, jnp.zeros((dk - c, dk), BF16)], axis=0)
        st_ref[h] = st * jnp.exp2(blast) + _dot(v_sq.T.astype(BF16), kd_sq)
        gz = g_ref[0, sl, hs]
        o_ref[0, sl, hs] = _rms(o[:rows], og_all[:, hs]) * (gz * _sigmoid(gz))

    def chunk(j, carry):
        if n_inner == 1:
            sl = pl.ds(0, rows)
        else:
            sl = pl.ds(pl.multiple_of(j * c, c), c)
        for h in range(hpb):
            head_chunk(h, sl)
        return carry

    lax.fori_loop(0, n_inner, chunk, 0)

    @pl.when(step == pl.num_programs(2) - 1)
    def _():
        for h in range(hpb):
            s_ref[0, h] = st_ref[h].T


def _hgrn(proj, lb_logits, out_g, s0, *, layer, heads):
    b, l, _ = proj.shape
    dk = s0.shape[2]
    dv = s0.shape[3]
    assert dk == LANES and dv == LANES
    if l >= HGRN_CHUNK:
        assert l % HGRN_CHUNK == 0
        c = rows = HGRN_CHUNK
        n_inner = _tile(l // c, HGRN_CHUNKS_PER_STEP)
        blk = rows * n_inner
    else:
        assert l % SUBLANES == 0
        c = -(-l // BF16_ROWS) * BF16_ROWS
        rows, n_inner, blk = l, 1, l
    nl = lb_logits.shape[0]
    hpb = _tile(heads, HGRN_HEADS_PER_STEP)
    ng = heads // hpb
    col = lambda part: pl.BlockSpec((1, blk, hpb * dk), lambda bi, h, s: (bi, s, part * ng + h))
    o, s_new = pl.pallas_call(
        functools.partial(_hgrn_kernel, layer=layer, c=c, rows=rows, n_inner=n_inner, hpb=hpb),
        grid=(b, ng, l // blk),
        in_specs=[
            col(0), col(1), col(2), col(3),
            pl.BlockSpec((nl, hpb * dk), lambda bi, h, s: (0, h)),
            pl.BlockSpec((1, hpb * dv), lambda bi, h, s: (0, h)),
            pl.BlockSpec((1, hpb, dk, dv), lambda bi, h, s: (bi, h, 0, 0)),
        ],
        out_specs=[
            pl.BlockSpec((1, blk, hpb * dv), lambda bi, h, s: (bi, s, h)),
            pl.BlockSpec((1, hpb, dk, dv), lambda bi, h, s: (bi, h, 0, 0)),
        ],
        out_shape=[jax.ShapeDtypeStruct((b, l, heads * dv), F32),
                   jax.ShapeDtypeStruct((b, heads, dk, dv), F32)],
        scratch_shapes=[pltpu.VMEM((hpb, dv, dk), F32)],
        compiler_params=_params(3, 48),
    )(proj, proj, proj, proj, lb_logits, out_g, s0)
    return o, s_new


def _logf_kernel(x_ref, g_ref, wft_ref, bft_ref, c0_ref, lf_ref, ft_ref, car_ref):
    tm = x_ref.shape[0]

    @pl.when(pl.program_id(1) == 0)
    def _():
        car_ref[...] = c0_ref[0]

    h = _rms(x_ref[...], g_ref[...])
    zt = _dot_nt(wft_ref[...], h, HIGHEST) + bft_ref[...]
    lft = _log_sigmoid(zt)
    lf_ref[0] = lft
    triu = (_iota((tm, tm), 0) <= _iota((tm, tm), 1)).astype(F32)
    ft = _dot(lft, triu, HIGHEST) + car_ref[...]
    ft_ref[0] = ft
    car_ref[...] = ft[:, tm - 1:tm]


def _logf(x, g, wft, bft, c0t, *, batch):
    m, d = x.shape
    l = m // batch
    hh = wft.shape[0]
    tm = _tile(l, 512)
    nt = l // tm
    out = jax.ShapeDtypeStruct((batch, hh, l), F32)
    return pl.pallas_call(
        _logf_kernel,
        grid=(batch, nt),
        in_specs=[
            pl.BlockSpec((tm, d), lambda b, i: (b * nt + i, 0)),
            pl.BlockSpec((1, d), lambda b, i: (0, 0)),
            pl.BlockSpec((hh, d), lambda b, i: (0, 0)),
            pl.BlockSpec((hh, 1), lambda b, i: (0, 0)),
            pl.BlockSpec((1, hh, 1), lambda b, i: (b, 0, 0)),
        ],
        out_specs=[pl.BlockSpec((1, hh, tm), lambda b, i: (b, 0, i)),
                   pl.BlockSpec((1, hh, tm), lambda b, i: (b, 0, i))],
        out_shape=[out, out],
        scratch_shapes=[pltpu.VMEM((hh, 1), F32)],
        compiler_params=_params(2, 32),
    )(x, g, wft, bft, c0t)


def _fox_prompt_kernel(q_ref, k_ref, v_ref, fq_ref, fk_ref, o_ref, m_ref, l_ref, acc_ref, *, tq, td, hpb):
    i = pl.program_id(2)
    hd = q_ref.shape[-1] // hpb
    m_ref[...] = jnp.full_like(m_ref, MASK_VALUE)
    l_ref[...] = jnp.zeros_like(l_ref)
    acc_ref[...] = jnp.zeros_like(acc_ref)
    qs, fqs = [], []
    for h in range(hpb):
        qs.append((q_ref[:, h * hd:(h + 1) * hd].astype(F32) * (hd ** -0.5 * LOG2E)).astype(BF16))
        fqs.append(jnp.broadcast_to(fq_ref[0, h] * LOG2E, (tq, LANES)))

    def tile(j, tk, row_lo, masked):
        nr = tq - row_lo
        rs = slice(row_lo, tq)
        ks = pl.ds(pl.multiple_of(j * tk, tk), tk)
        for h in range(hpb):
            cs = slice(h * hd, (h + 1) * hd)
            s = _dot_nt(qs[h][rs], k_ref[ks, cs])
            fk = fk_ref[0, h, :, ks] * LOG2E
            us = []
            for c in range(tk // LANES):
                ls = slice(c * LANES, (c + 1) * LANES)
                u = s[:, ls] - fk[:, ls]
                if masked:
                    u = jnp.where(_iota((nr, LANES), 0) >= c * LANES + _iota((nr, LANES), 1), u, MASK_VALUE)
                us.append(u)
            row_max = jnp.max(functools.reduce(jnp.maximum, us), axis=-1, keepdims=True)
            fq = fqs[h][rs]
            m_old = m_ref[h, rs]
            m_new = jnp.maximum(m_old, jnp.broadcast_to(row_max, (nr, LANES)) + fq)
            shift = fq - m_new
            ps = [jnp.exp2(u + shift) for u in us]
            alpha = jnp.exp2(m_old - m_new)
            m_ref[h, rs] = m_new
            l_ref[h, rs] = alpha * l_ref[h, rs] + functools.reduce(jnp.add, ps)
            p_all = jnp.concatenate([p.astype(BF16) for p in ps], axis=1)
            acc_ref[h, rs] = alpha * acc_ref[h, rs] + _dot(p_all, v_ref[ks, cs])

    def body(j, carry):
        tile(j, tq, 0, False)
        return carry

    lax.fori_loop(0, i, body, 0)
    dpq = tq // td
    for mt in range(dpq):
        tile(i * dpq + mt, td, mt * td, True)
    for h in range(hpb):
        o_ref[:, h * hd:(h + 1) * hd] = acc_ref[h] / jnp.sum(l_ref[h], axis=-1, keepdims=True)


def _fox_prompt(q, k, v, ft, *, batch, heads):
    m, d = q.shape
    l = m // batch
    hd = d // heads
    assert hd == LANES
    hpb = ATTN_HEADS_PER_STEP
    tq = _tile(l, ATTN_Q_TILE)
    td = _tile(tq, ATTN_DIAG_TILE)
    nq = l // tq
    fcol = ft[..., None]
    frow = ft[:, :, None, :]
    return pl.pallas_call(
        functools.partial(_fox_prompt_kernel, tq=tq, td=td, hpb=hpb),
        grid=(batch, heads // hpb, nq),
        in_specs=[
            pl.BlockSpec((tq, hpb * hd), lambda b, h, i: (b * nq + i, h)),
            pl.BlockSpec((l, hpb * hd), lambda b, h, i: (b, h)),
            pl.BlockSpec((l, hpb * hd), lambda b, h, i: (b, h)),
            pl.BlockSpec((1, hpb, tq, 1), lambda b, h, i: (b, h, i, 0)),
            pl.BlockSpec((1, hpb, 1, l), lambda b, h, i: (b, h, 0, 0)),
        ],
        out_specs=pl.BlockSpec((tq, hpb * hd), lambda b, h, i: (b * nq + i, h)),
        out_shape=jax.ShapeDtypeStruct((m, d), F32),
        scratch_shapes=[pltpu.VMEM((hpb, tq, LANES), F32), pltpu.VMEM((hpb, tq, LANES), F32),
                        pltpu.VMEM((hpb, tq, hd), F32)],
        compiler_params=_params(3, 56),
    )(q, k, v, fcol, frow)


def _fpast_kernel(pt_ref, lf_ref, o_ref, car_ref, *, n_pg):
    b, g = pl.program_id(0), pl.program_id(1)
    n_grp, prow = lf_ref.shape[1], lf_ref.shape[2]
    rows = n_pg * prow
    keys_per_row = LANES // HEAD_GROUP

    @pl.when(g == 0)
    def _():
        car_ref[...] = jnp.zeros_like(car_ref)

    li, lj = _iota((LANES, LANES), 0), _iota((LANES, LANES), 1)
    same_head = (li % HEAD_GROUP) == (lj % HEAD_GROUP)
    prefix = (same_head & (li // HEAD_GROUP <= lj // HEAD_GROUP)).astype(F32)
    last = (same_head & (li // HEAD_GROUP == keys_per_row - 1)).astype(F32)
    strict = (_iota((rows, rows), 0) > _iota((rows, rows), 1)).astype(F32)
    for a in range(n_grp):
        x = jnp.concatenate([lf_ref[pt_ref[b, g * n_pg + o], a] for o in range(n_pg)], axis=0)
        y = _dot(x, prefix, HIGHEST)
        tot = _dot(y, last, HIGHEST)
        o_ref[0, a] = y + _dot(strict, tot, HIGHEST) + car_ref[a]
        car_ref[a] += jnp.sum(tot, axis=0, keepdims=True)


def _fpast(page_table, cache_logf_grp):
    b, n_pages = page_table.shape
    n_phys, n_grp, prow, _ = cache_logf_grp.shape
    n_pg = _tile(n_pages, FPAST_PAGES_PER_STEP)
    return pl.pallas_call(
        functools.partial(_fpast_kernel, n_pg=n_pg),
        grid_spec=pltpu.PrefetchScalarGridSpec(
            num_scalar_prefetch=1,
            grid=(b, n_pages // n_pg),
            in_specs=[pl.BlockSpec((n_phys, n_grp, prow, LANES), lambda bi, g, pt: (0, 0, 0, 0))],
            out_specs=pl.BlockSpec((1, n_grp, n_pg * prow, LANES), lambda bi, g, pt: (bi, 0, g, 0)),
            scratch_shapes=[pltpu.VMEM((n_grp, 1, LANES), F32)],
        ),
        out_shape=jax.ShapeDtypeStruct((b, n_grp, n_pages * prow, LANES), F32),
        compiler_params=_params(2, 40),
    )(page_table, cache_logf_grp)


def _fox_sample_kernel(pt_ref, q_ref, *refs, heads, n_pg):
    step, nsteps = pl.program_id(1), pl.num_programs(1)
    kc_refs, vc_refs = refs[:n_pg], refs[n_pg:2 * n_pg]
    fp_ref, kn_ref, vn_ref, fq_ref, fn_ref, o_ref, qa_ref, m_ref, l_ref, acc_ref = refs[2 * n_pg:]
    t, d = q_ref.shape
    hd = d // heads
    grp = HEAD_GROUP
    n_grp = heads // grp
    r = grp * t
    page = kc_refs[0].shape[1]
    prow = page * grp // LANES

    def pre():
        @pl.when(step == 0)
        def _():
            q = q_ref[...] * (hd ** -0.5 * LOG2E)
            for a in range(n_grp):
                qa_ref[a] = jnp.concatenate(
                    [q[:, (a * grp + j) * hd:(a * grp + j + 1) * hd] for j in range(grp)], axis=0)
            m_ref[...] = jnp.full_like(m_ref, MASK_VALUE)
            l_ref[...] = jnp.zeros_like(l_ref)
            acc_ref[...] = jnp.zeros_like(acc_ref)

    head_match = (_iota((r, LANES), 0) // t) == (_iota((r, LANES), 1) % grp)

    def update(a, k_list, v_list, fk_list, mask):
        fq = jnp.broadcast_to(fq_ref[0, a] * LOG2E, (r, LANES))
        qa = qa_ref[a]
        us = []
        for k_flat, fk in zip(k_list, fk_list):
            s = _dot_nt(qa, k_flat)
            for c in range(k_flat.shape[0] // LANES):
                u = s[:, c * LANES:(c + 1) * LANES] - fk[c:c + 1] * LOG2E
                us.append(jnp.where(mask, u, MASK_VALUE))
        row_max = jnp.max(functools.reduce(jnp.maximum, us), axis=-1, keepdims=True)
        m_old = m_ref[a]
        m_new = jnp.maximum(m_old, jnp.broadcast_to(row_max, (r, LANES)) + fq)
        shift = fq - m_new
        ps = [jnp.exp2(u + shift) for u in us]
        alpha = jnp.exp2(m_old - m_new)
        m_ref[a] = m_new
        l_ref[a] = alpha * l_ref[a] + functools.reduce(jnp.add, ps)
        pv, at = None, 0
        for v_flat in v_list:
            n_t = v_flat.shape[0] // LANES
            p_pg = jnp.concatenate(ps[at:at + n_t], axis=1) if n_t > 1 else ps[at]
            at += n_t
            part = _dot(p_pg, v_flat)
            pv = part if pv is None else pv + part
        acc_ref[a] = alpha * acc_ref[a] + pv

    def group_rows(ref, a):
        blk = ref[0, :, a * grp:(a + 1) * grp, :]
        return blk.reshape(blk.shape[0] * grp, hd)

    def main():
        for a in range(n_grp):
            update(a, [group_rows(kc, a) for kc in kc_refs], [group_rows(vc, a) for vc in vc_refs],
                   [fp_ref[0, a, o * prow:(o + 1) * prow] for o in range(n_pg)], head_match)

    def post():
        @pl.when(step == nsteps - 1)
        def _():
            lane = _iota((r, LANES), 1)
            new_mask = head_match & (lane // grp <= _iota((r, LANES), 0) % t) & (lane < t * grp)
            zeros = jnp.zeros((LANES - t * grp, hd), F32)
            for a in range(n_grp):
                kn = jnp.concatenate([group_rows(kn_ref, a), zeros], axis=0)
                vn = jnp.concatenate([group_rows(vn_ref, a), zeros], axis=0)
                update(a, [kn], [vn], [fn_ref[0, a]], new_mask)
                res = acc_ref[a] / jnp.sum(l_ref[a], axis=-1, keepdims=True)
                for j in range(grp):
                    h = a * grp + j
                    o_ref[:, h * hd:(h + 1) * hd] = res[j * t:(j + 1) * t]

    pre()
    main()
    post()


def _fox_sample(q, cache_k, cache_v, page_table, fpast, k_new, v_new, fq_col, fnew_flat):
    b, n_pages = page_table.shape
    m, d = q.shape
    t = m // b
    _, page, heads, hd = cache_k.shape
    grp = HEAD_GROUP
    n_grp = heads // grp
    assert hd == LANES and heads % grp == 0 and t * grp <= LANES and (page * grp) % LANES == 0
    r = grp * t
    n_pg = _tile(n_pages, PAGES_PER_STEP)
    prow = page * grp // LANES
    page_spec = lambda o: pl.BlockSpec((1, page, heads, hd), lambda bi, s, pt: (pt[bi, s * n_pg + o], 0, 0, 0))
    new_spec = pl.BlockSpec((1, t, heads, hd), lambda bi, s, pt: (bi, 0, 0, 0))
    in_specs = ([pl.BlockSpec((t, d), lambda bi, s, pt: (bi, 0))]
                + [page_spec(o) for o in range(n_pg)] * 2
                + [pl.BlockSpec((1, n_grp, n_pg * prow, LANES), lambda bi, s, pt: (bi, 0, s, 0)),
                   new_spec, new_spec,
                   pl.BlockSpec((1, n_grp, r, 1), lambda bi, s, pt: (bi, 0, 0, 0)),
                   pl.BlockSpec((1, n_grp, 1, LANES), lambda bi, s, pt: (bi, 0, 0, 0))])
    return pl.pallas_call(
        functools.partial(_fox_sample_kernel, heads=heads, n_pg=n_pg),
        grid_spec=pltpu.PrefetchScalarGridSpec(
            num_scalar_prefetch=1,
            grid=(b, n_pages // n_pg),
            in_specs=in_specs,
            out_specs=pl.BlockSpec((t, d), lambda bi, s, pt: (bi, 0)),
            scratch_shapes=[pltpu.VMEM((n_grp, r, hd), F32), pltpu.VMEM((n_grp, r, LANES), F32),
                            pltpu.VMEM((n_grp, r, LANES), F32), pltpu.VMEM((n_grp, r, hd), F32)],
        ),
        out_shape=jax.ShapeDtypeStruct((m, d), F32),
        compiler_params=_params(2, 56),
    )(page_table, q, *([cache_k] * n_pg), *([cache_v] * n_pg), fpast, k_new, v_new, fq_col, fnew_flat)


def _trunk(x3, mods, s0, attend_ctx, wts):
    (norm_pre, norm_post, a_w_in, a_w_out, a_lb_logits, a_out_norm, kv_norm, w_kv, wft, b_forget,
     b_w_q, b_w_o, mlp_w_in, mlp_w_out) = wts
    b, l, d = x3.shape
    m = b * l
    depth = norm_pre.shape[0]
    n_a = a_w_out.shape[0]
    a_heads = s0.shape[2]
    pick = lambda w, layer: (w[layer], 0) if isinstance(w, list) else (w, layer)
    casts = {"a_w_in": [], "mlp_w_in": [], "mlp_w_out": []}
    b_heads = b_forget.shape[0]
    x = x3.reshape(m, d)
    if m <= 512:
        rows_per_mod = m
        expand = lambda t: jnp.repeat(t, l, axis=0)[None]
    else:
        rows_per_mod = l
        expand = lambda t: t[:, None, :]
    row = lambda t: t.reshape(1, -1)
    states = []
    shared = None
    for layer in range(depth):
        mod = expand(mods[layer, 0])
        if layer < n_a:
            w_in, w_idx = pick(a_w_in, layer)
            proj = _norm_mm(x, row(norm_pre[layer, 0]), mod, w_in, w_idx, rows_per_mod=rows_per_mod)
            if w_in.dtype != BF16:
                proj, w_in_b = proj
                casts["a_w_in"].append(w_in_b)
            o, s_new = _hgrn(proj.reshape(b, l, -1), a_lb_logits, row(a_out_norm[layer]), s0[layer],
                             layer=layer, heads=a_heads)
            states.append(s_new)
            y_in, w_o, w_idx = o.reshape(m, d), a_w_out, layer
        else:
            j = layer - n_a
            if shared is None:
                kvg = row(kv_norm)
                k, k16 = _norm_mm(x, kvg, None, w_kv, 0, rows_per_mod=rows_per_mod, n_cols=d, col_off=0,
                                  out_dtypes=(F32, BF16))
                v, v16 = _norm_mm(x, kvg, None, w_kv, 0, rows_per_mod=rows_per_mod, n_cols=d, col_off=d,
                                  out_dtypes=(F32, BF16))
                shared = attend_ctx["make"](x, kvg, wft, b_forget.reshape(b_heads, 1), k, v, k16, v16)
            q = _norm_mm(x, row(norm_pre[layer, 0]), mod, b_w_q, j, rows_per_mod=rows_per_mod,
                         out_dtypes=(attend_ctx["q_dtype"],))
            y_in, w_o, w_idx = attend_ctx["attend"](q, shared), b_w_o, j
        x = _mm_post(y_in, w_o, w_idx, x, mod, row(norm_post[layer, 0]), rows_per_mod=rows_per_mod)
        mod = expand(mods[layer, 1])
        w1, w_idx = pick(mlp_w_in, layer)
        w2, _ = pick(mlp_w_out, layer)
        x = _mlp(x, row(norm_pre[layer, 1]), mod, w1, w2, w_idx, row(norm_post[layer, 1]),
                 rows_per_mod=rows_per_mod)
        if w1.dtype != BF16:
            x, w1_b, w2_b = x
            casts["mlp_w_in"].append(w1_b)
            casts["mlp_w_out"].append(w2_b)
    k, v, lft = shared["rows"]
    kv_shape = (b, l, b_heads, d // b_heads)
    return (x.reshape(b, l, d), jnp.stack(states), k.reshape(kv_shape), v.reshape(kv_shape),
            jnp.swapaxes(lft, 1, 2), casts)


def kernel(x_prompt, x_sample, c_prompt, c_sample, state_hgrn, cache_k, cache_v, cache_logf, page_table,
           norm_pre, norm_post, w_ada, b_ada, a_w_in, a_w_out, a_lb_logits, a_out_norm,
           kv_norm, w_kv, b_forget, b_w_q, b_w_o, mlp_w_in, mlp_w_out):
    bp, _, d = x_prompt.shape
    bs, ts, _ = x_sample.shape
    depth = norm_pre.shape[0]
    n_phys, page, heads, hd = cache_k.shape
    wts_s = (norm_pre, norm_post, a_w_in, a_w_out.astype(BF16), a_lb_logits, a_out_norm, kv_norm,
             w_kv.astype(BF16)[None], w_kv[:, 2 * d:].T, b_forget,
             b_w_q.astype(BF16), b_w_o.astype(BF16), mlp_w_in, mlp_w_out)

    n_c = bp + bs
    c_rows = -(-n_c // 8) * 8
    c_all = jnp.concatenate([c_prompt, c_sample, jnp.zeros((c_rows - n_c, d), F32)], axis=0)
    mods = _ada_all(c_all, w_ada.reshape(depth * 2, d, 3 * d), b_ada.reshape(depth * 2, 1, 3 * d))
    mods = mods.reshape(depth, 2, c_rows, 3 * d)
    mods_p, mods_s = mods[:, :, :bp], mods[:, :, bp:n_c]

    grp = HEAD_GROUP
    n_grp = heads // grp
    prow = page * grp // LANES
    lf_grp = cache_logf.astype(F32).reshape(n_phys, page, n_grp, grp).transpose(0, 2, 1, 3)
    fpast = _fpast(page_table, lf_grp.reshape(n_phys, n_grp, prow, LANES))
    f_total = fpast[:, :, -1, LANES - grp:].reshape(bs, heads, 1)

    def make_sample(x, kvg, wft, bft, k, v, k16, v16):
        lft, fnew_t = _logf(x, kvg, wft, bft, f_total, batch=bs)
        fq_col = fnew_t.reshape(bs, n_grp, grp * ts, 1)
        fnew_flat = fnew_t.reshape(bs, n_grp, grp, ts).transpose(0, 1, 3, 2).reshape(bs, n_grp, 1, ts * grp)
        fnew_flat = jnp.pad(fnew_flat, ((0, 0), (0, 0), (0, 0), (0, LANES - ts * grp)))
        shape4 = (bs, ts, heads, hd)
        return {"rows": (k, v, lft), "k": k.reshape(shape4), "v": v.reshape(shape4), "fq": fq_col, "fn": fnew_flat}

    def attend_sample(q, sh):
        return _fox_sample(q, cache_k, cache_v, page_table, fpast, sh["k"], sh["v"], sh["fq"], sh["fn"])

    y_s, st_s, k_s, v_s, lf_s, casts = _trunk(
        x_sample, mods_s, state_hgrn, {"make": make_sample, "attend": attend_sample, "q_dtype": F32}, wts_s)

    def make_prompt(x, kvg, wft, bft, k, v, k16, v16):
        lft, ft = _logf(x, kvg, wft, bft, jnp.zeros((bp, heads, 1), F32), batch=bp)
        return {"rows": (k, v, lft), "k": k16, "v": v16, "ft": ft}

    def attend_prompt(q, sh):
        return _fox_prompt(q, sh["k"], sh["v"], sh["ft"], batch=bp, heads=heads)

    wts_p = wts_s[:2] + (casts["a_w_in"],) + wts_s[3:12] + (casts["mlp_w_in"], casts["mlp_w_out"])
    s0_prompt = jnp.zeros((a_w_out.shape[0], bp) + state_hgrn.shape[2:], F32)
    y_p, st_p, k_p, v_p, lf_p, _ = _trunk(
        x_prompt, mods_p, s0_prompt, {"make": make_prompt, "attend": attend_prompt, "q_dtype": BF16}, wts_p)

    return (y_p, y_s, st_p, st_s, k_p, v_p, lf_p, k_s, v_s, lf_s)
```

```python
import functools
import math

import jax
import jax.numpy as jnp
from jax import lax
from jax.experimental import pallas as pl
from jax.experimental.pallas import tpu as pltpu

EPS = 1e-6
MASK_VALUE = -1e30
TINY = 1e-30
LOG2E = math.log2(math.e)
F32 = jnp.float32
BF16 = jnp.bfloat16
HIGHEST = lax.Precision.HIGHEST

LANES = 128
HGRN_CHUNK = 128
HGRN_SUB = 8
HGRN_CHUNKS_PER_STEP = 8
SUBLANES = 8
BF16_ROWS = 16
HEAD_GROUP = SUBLANES
PAGES_PER_STEP = 8
FPAST_PAGES_PER_STEP = 16
HGRN_HEADS_PER_STEP = 8
ATTN_Q_TILE = 1024
ATTN_DIAG_TILE = 256
ATTN_HEADS_PER_STEP = 4
MIB = 2 ** 20


def _params(n_axes, vmem_mib):
    return pltpu.CompilerParams(
        dimension_semantics=("arbitrary",) * n_axes, vmem_limit_bytes=vmem_mib * MIB)


def _tile(n, pref):
    t = min(n, pref)
    while n % t:
        t -= 1
    return t


def _sigmoid(x):
    return 1.0 / (1.0 + jnp.exp(-x))


def _log_sigmoid(x):
    return jnp.minimum(x, 0.0) - jnp.log1p(jnp.exp(-jnp.abs(x)))


def _rms(x, g):
    return x * lax.rsqrt(jnp.mean(x * x, axis=-1, keepdims=True) + EPS) * g


def _dot(a, b, precision=None):
    return jnp.dot(a, b, preferred_element_type=F32, precision=precision)


def _dot_nt(a, b, precision=None):
    return lax.dot_general(a, b, (((1,), (1,)), ((), ())), preferred_element_type=F32, precision=precision)


def _iota(shape, dim):
    return lax.broadcasted_iota(jnp.int32, shape, dim)


def _ada_kernel(c_ref, w_ref, b_ref, o_ref):
    c = c_ref[...]
    s = (c * _sigmoid(c)).astype(BF16)
    o_ref[0] = _dot(s, w_ref[0].astype(BF16)) + b_ref[0]


def _ada_all(c_all, w_ada, b_ada):
    n, d, n3 = w_ada.shape
    r = c_all.shape[0]
    tn = _tile(n3, 1024)
    return pl.pallas_call(
        _ada_kernel,
        grid=(n, n3 // tn),
        in_specs=[
            pl.BlockSpec((r, d), lambda i, j: (0, 0)),
            pl.BlockSpec((1, d, tn), lambda i, j: (i, 0, j)),
            pl.BlockSpec((1, 1, tn), lambda i, j: (i, 0, j)),
        ],
        out_specs=pl.BlockSpec((1, r, tn), lambda i, j: (i, 0, j)),
        out_shape=jax.ShapeDtypeStruct((n, r, n3), F32),
        compiler_params=_params(2, 40),
    )(c_all, w_ada, b_ada)


def _norm_mm_kernel(*refs, has_mod, n_out, emit):
    if has_mod:
        x_ref, g_ref, sh_ref, sc_ref, w_ref = refs[:5]
        rest = refs[5:]
    else:
        x_ref, g_ref, w_ref = refs[:3]
        rest = refs[3:]
    o_refs, h_ref = rest[:n_out], rest[-1]

    @pl.when(pl.program_id(1) == 0)
    def _():
        h = _rms(x_ref[...], g_ref[...])
        if has_mod:
            h = h * (1.0 + sc_ref[0]) + sh_ref[0]
        h_ref[...] = h.astype(BF16)

    w = w_ref[0].astype(BF16)
    if emit:
        rest[n_out][0] = w
    y = _dot(h_ref[...], w)
    for o_ref in o_refs:
        o_ref[...] = y.astype(o_ref.dtype)


def _norm_mm(x, g, mod, w, layer, *, rows_per_mod, n_cols=None, col_off=0, out_dtypes=(F32,),
             tm_pref=1024, tn_pref=1024):
    m, d = x.shape
    n_cols = w.shape[2] if n_cols is None else n_cols
    tm = _tile(m, tm_pref)
    tn = _tile(n_cols, tn_pref)
    assert col_off % tn == 0
    joff = col_off // tn
    emit = w.dtype != BF16
    assert not emit or (col_off == 0 and n_cols == w.shape[2])
    in_specs = [pl.BlockSpec((tm, d), lambda i, j: (i, 0)), pl.BlockSpec((1, d), lambda i, j: (0, 0))]
    args = [x, g]
    if mod is not None:
        r = mod.shape[1]
        assert rows_per_mod % tm == 0 and r in (1, tm)
        bidx = lambda i: (i * tm) // rows_per_mod
        in_specs += [pl.BlockSpec((1, r, d), lambda i, j: (bidx(i), 0, 0)),
                     pl.BlockSpec((1, r, d), lambda i, j: (bidx(i), 0, 1))]
        args += [mod, mod]
    in_specs.append(pl.BlockSpec((1, d, tn), lambda i, j: (layer, 0, j + joff)))
    args.append(w)
    out_specs = [pl.BlockSpec((tm, tn), lambda i, j: (i, j)) for _ in out_dtypes]
    out_shape = [jax.ShapeDtypeStruct((m, n_cols), dt) for dt in out_dtypes]
    if emit:
        out_specs.append(pl.BlockSpec((1, d, tn), lambda i, j: (0, 0, j)))
        out_shape.append(jax.ShapeDtypeStruct((1, d, n_cols), BF16))
    outs = pl.pallas_call(
        functools.partial(_norm_mm_kernel, has_mod=mod is not None, n_out=len(out_dtypes), emit=emit),
        grid=(m // tm, n_cols // tn),
        in_specs=in_specs,
        out_specs=out_specs,
        out_shape=out_shape,
        scratch_shapes=[pltpu.VMEM((tm, d), BF16)],
        compiler_params=_params(2, 56),
    )(*args)
    return outs if len(outs) > 1 else outs[0]


def _mm_post_kernel(a_ref, w_ref, x_ref, gate_ref, g_ref, o_ref):
    y = _dot(a_ref[...].astype(BF16), w_ref[...])
    o_ref[...] = x_ref[...] + gate_ref[0] * _rms(y, g_ref[...])


def _mm_post(a, w, layer, x, mod, g, *, rows_per_mod, tm_pref=512):
    m, kdim = a.shape
    d = w.shape[2]
    tm = _tile(m, tm_pref)
    r = mod.shape[1]
    assert rows_per_mod % tm == 0 and r in (1, tm)
    bidx = lambda i: (i * tm) // rows_per_mod
    return pl.pallas_call(
        _mm_post_kernel,
        grid=(m // tm,),
        in_specs=[
            pl.BlockSpec((tm, kdim), lambda i: (i, 0)),
            pl.BlockSpec((None, kdim, d), lambda i: (layer, 0, 0)),
            pl.BlockSpec((tm, d), lambda i: (i, 0)),
            pl.BlockSpec((1, r, d), lambda i: (bidx(i), 0, 2)),
            pl.BlockSpec((1, d), lambda i: (0, 0)),
        ],
        out_specs=pl.BlockSpec((tm, d), lambda i: (i, 0)),
        out_shape=jax.ShapeDtypeStruct((m, d), F32),
        compiler_params=pltpu.CompilerParams(
            dimension_semantics=("arbitrary",), vmem_limit_bytes=56 * MIB,
            allow_input_fusion=[False, True, False, False, False]),
    )(a, w, x, mod, g)


def _mlp_kernel(*refs, emit):
    x_ref, gpre_ref, sh_ref, sc_ref, gate_ref, w1_ref, w2_ref, gpost_ref, o_ref = refs[:9]
    h_ref, acc_ref = refs[-2:]
    f = pl.program_id(1)

    @pl.when(f == 0)
    def _():
        h = _rms(x_ref[...], gpre_ref[...]) * (1.0 + sc_ref[0]) + sh_ref[0]
        h_ref[...] = h.astype(BF16)
        acc_ref[...] = jnp.zeros_like(acc_ref)

    w1 = w1_ref[0].astype(BF16)
    w2 = w2_ref[0].astype(BF16)
    if emit:
        w1b_ref, w2b_ref = refs[9:11]
        w1b_ref[0] = w1
        w2b_ref[0] = w2
    u = jnp.maximum(_dot(h_ref[...], w1), 0.0)
    acc_ref[...] += _dot((u * u).astype(BF16), w2)

    @pl.when(f == pl.num_programs(1) - 1)
    def _():
        o_ref[...] = x_ref[...] + gate_ref[0] * _rms(acc_ref[...], gpost_ref[...])


def _mlp(x, gpre, mod, w1, w2, layer, gpost, *, rows_per_mod, tm_pref=512, tf_pref=1024):
    m, d = x.shape
    ff = w1.shape[2]
    tm = _tile(m, tm_pref)
    tf = _tile(ff, tf_pref)
    r = mod.shape[1]
    assert rows_per_mod % tm == 0 and r in (1, tm)
    emit = w1.dtype != BF16
    bidx = lambda i: (i * tm) // rows_per_mod
    out_specs = [pl.BlockSpec((tm, d), lambda i, f: (i, 0))]
    out_shape = [jax.ShapeDtypeStruct((m, d), F32)]
    if emit:
        out_specs += [pl.BlockSpec((1, d, tf), lambda i, f: (0, 0, f)),
                      pl.BlockSpec((1, tf, d), lambda i, f: (0, f, 0))]
        out_shape += [jax.ShapeDtypeStruct((1, d, ff), BF16), jax.ShapeDtypeStruct((1, ff, d), BF16)]
    outs = pl.pallas_call(
        functools.partial(_mlp_kernel, emit=emit),
        grid=(m // tm, ff // tf),
        in_specs=[
            pl.BlockSpec((tm, d), lambda i, f: (i, 0)),
            pl.BlockSpec((1, d), lambda i, f: (0, 0)),
            pl.BlockSpec((1, r, d), lambda i, f: (bidx(i), 0, 0)),
            pl.BlockSpec((1, r, d), lambda i, f: (bidx(i), 0, 1)),
            pl.BlockSpec((1, r, d), lambda i, f: (bidx(i), 0, 2)),
            pl.BlockSpec((1, d, tf), lambda i, f: (layer, 0, f)),
            pl.BlockSpec((1, tf, d), lambda i, f: (layer, f, 0)),
            pl.BlockSpec((1, d), lambda i, f: (0, 0)),
        ],
        out_specs=out_specs,
        out_shape=out_shape,
        scratch_shapes=[pltpu.VMEM((tm, d), BF16), pltpu.VMEM((tm, d), F32)],
        compiler_params=_params(2, 56),
    )(x, gpre, mod, mod, mod, w1, w2, gpost)
    return outs if emit else outs[0]


def _hgrn_kernel(q_ref, f_ref, i_ref, g_ref, lbl_ref, og_ref, s0_ref, o_ref, s_ref, st_ref,
                 *, layer, c, rows, n_inner, hpb):
    sub = HGRN_SUB
    dk = q_ref.shape[-1] // hpb
    step = pl.program_id(2)

    @pl.when(step == 0)
    def _():
        for h in range(hpb):
            st_ref[h] = s0_ref[0, h].T

    z = lbl_ref[...]
    e = jnp.exp(z - jnp.max(z, axis=0, keepdims=True))
    p = e / jnp.sum(e, axis=0, keepdims=True)
    cs = p[0:1]
    for r in range(1, layer + 1):
        cs = cs + p[r:r + 1]
    lb_all = cs - p[0:1]
    og_all = og_ref[...]

    tril = (_iota((c, c), 0) >= _iota((c, c), 1)).astype(BF16)
    lane = _iota((sub, c), 1)
    sub_row = _iota((sub, c), 0)
    scale = dk ** -0.5

    def pad(t, fill=0.0):
        if rows == c:
            return t
        return jnp.concatenate([t, jnp.full((c - rows, t.shape[1]), fill, t.dtype)], axis=0)

    def cumsum_rows(x):
        hi = x.astype(BF16)
        r1 = x - hi.astype(F32)
        mid = r1.astype(BF16)
        lo = (r1 - mid.astype(F32)).astype(BF16)
        return _dot(tril, hi) + _dot(tril, mid) + _dot(tril, lo)

    def head_chunk(h, sl):
        hs = slice(h * dk, (h + 1) * dk)
        lb = lb_all[:, hs]
        lb_floor = jnp.maximum(lb, TINY)
        one_m_lb = 1.0 - lb
        log2_one_m_lb = jnp.log(one_m_lb) * LOG2E
        qz = q_ref[0, sl, hs]
        fz = f_ref[0, sl, hs]
        q = pad(qz * _sigmoid(qz))
        e = jnp.exp(-jnp.abs(fz))
        t = 1.0 + e
        r = 1.0 / t
        sig = jnp.where(fz >= 0.0, r, e * r)
        logf = pad(jnp.log(lb_floor + one_m_lb * sig))
        lk = pad(log2_one_m_lb + (jnp.minimum(-fz, 0.0) - jnp.log(t)) * LOG2E, -jnp.inf)
        v = pad(i_ref[0, sl, hs])
        b2 = cumsum_rows(logf) * LOG2E
        c2 = b2 - lk
        st = st_ref[h]
        inter = _dot_nt((q * jnp.exp2(b2)).astype(BF16), st.astype(BF16))
        blocks = []
        for i in range(c // sub):
            lo = i * sub
            qi, bi, ci = q[lo:lo + sub], b2[lo:lo + sub], c2[lo:lo + sub]
            if i > 0:
                ref_b = bi[0:1]
                qs = (qi * jnp.exp2(bi - ref_b)).astype(BF16)
                kt = jnp.exp2(ref_b - c2[:lo]).astype(BF16)
                kt = jnp.concatenate([kt, jnp.zeros((c - lo, dk), BF16)], axis=0)
                s_i = _dot_nt(qs, kt)
            else:
                s_i = jnp.zeros((sub, c), F32)
            for s in range(sub):
                w = jnp.sum(qi * jnp.exp2(bi - ci[s:s + 1]), axis=-1, keepdims=True)
                s_i = jnp.where(lane == lo + s, w, s_i)
            blocks.append(jnp.where(lane <= lo + sub_row, s_i, 0.0))
        scores = jnp.concatenate(blocks, axis=0).astype(BF16)
        o = (inter + _dot(scores, v.astype(BF16))) * scale
        blast = b2[c - 1:c]
        kd = jnp.exp2(blast - c2).astype(BF16)
        v_sq = v if c == dk else jnp.concatenate([v, jnp.zeros((dk - c, dk), F32)], axis=0)
        kd_sq = kd if c == dk else jnp.concatenate([kd, jnp.zeros((dk - c, dk), BF16)], axis=0)
        st_ref[h] = st * jnp.exp2(blast) + _dot(v_sq.T.astype(BF16), kd_sq)
        gz = g_ref[0, sl, hs]
        o_ref[0, sl, hs] = _rms(o[:rows], og_all[:, hs]) * (gz * _sigmoid(gz))

    def chunk(j, carry):
        if n_inner == 1:
            sl = pl.ds(0, rows)
        else:
            sl = pl.ds(pl.multiple_of(j * c, c), c)
        for h in range(hpb):
            head_chunk(h, sl)
        return carry

    lax.fori_loop(0, n_inner, chunk, 0)

    @pl.when(step == pl.num_programs(2) - 1)
    def _():
        for h in range(hpb):
            s_ref[0, h] = st_ref[h].T


def _hgrn(proj, lb_logits, out_g, s0, *, layer, heads):
    b, l, _ = proj.shape
    dk = s0.shape[2]
    dv = s0.shape[3]
    assert dk == LANES and dv == LANES
    if l >= HGRN_CHUNK:
        assert l % HGRN_CHUNK == 0
        c = rows = HGRN_CHUNK
        n_inner = _tile(l // c, HGRN_CHUNKS_PER_STEP)
        blk = rows * n_inner
    else:
        assert l % SUBLANES == 0
        c = -(-l // BF16_ROWS) * BF16_ROWS
        rows, n_inner, blk = l, 1, l
    nl = lb_logits.shape[0]
    hpb = _tile(heads, HGRN_HEADS_PER_STEP)
    ng = heads // hpb
    col = lambda part: pl.BlockSpec((1, blk, hpb * dk), lambda bi, h, s: (bi, s, part * ng + h))
    o, s_new = pl.pallas_call(
        functools.partial(_hgrn_kernel, layer=layer, c=c, rows=rows, n_inner=n_inner, hpb=hpb),
        grid=(b, ng, l // blk),
        in_specs=[
            col(0), col(1), col(2), col(3),
            pl.BlockSpec((nl, hpb * dk), lambda bi, h, s: (0, h)),
            pl.BlockSpec((1, hpb * dv), lambda bi, h, s: (0, h)),
            pl.BlockSpec((1, hpb, dk, dv), lambda bi, h, s: (bi, h, 0, 0)),
        ],
        out_specs=[
            pl.BlockSpec((1, blk, hpb * dv), lambda bi, h, s: (bi, s, h)),
            pl.BlockSpec((1, hpb, dk, dv), lambda bi, h, s: (bi, h, 0, 0)),
        ],
        out_shape=[jax.ShapeDtypeStruct((b, l, heads * dv), F32),
                   jax.ShapeDtypeStruct((b, heads, dk, dv), F32)],
        scratch_shapes=[pltpu.VMEM((hpb, dv, dk), F32)],
        compiler_params=_params(3, 48),
    )(proj, proj, proj, proj, lb_logits, out_g, s0)
    return o, s_new


def _logf_kernel(x_ref, g_ref, wft_ref, bft_ref, c0_ref, lf_ref, ft_ref, car_ref):
    tm = x_ref.shape[0]

    @pl.when(pl.program_id(1) == 0)
    def _():
        car_ref[...] = c0_ref[0]

    h = _rms(x_ref[...], g_ref[...])
    zt = _dot_nt(wft_ref[...], h, HIGHEST) + bft_ref[...]
    lft = _log_sigmoid(zt)
    lf_ref[0] = lft
    triu = (_iota((tm, tm), 0) <= _iota((tm, tm), 1)).astype(F32)
    ft = _dot(lft, triu, HIGHEST) + car_ref[...]
    ft_ref[0] = ft
    car_ref[...] = ft[:, tm - 1:tm]


def _logf(x, g, wft, bft, c0t, *, batch):
    m, d = x.shape
    l = m // batch
    hh = wft.shape[0]
    tm = _tile(l, 512)
    nt = l // tm
    out = jax.ShapeDtypeStruct((batch, hh, l), F32)
    return pl.pallas_call(
        _logf_kernel,
        grid=(batch, nt),
        in_specs=[
            pl.BlockSpec((tm, d), lambda b, i: (b * nt + i, 0)),
            pl.BlockSpec((1, d), lambda b, i: (0, 0)),
            pl.BlockSpec((hh, d), lambda b, i: (0, 0)),
            pl.BlockSpec((hh, 1), lambda b, i: (0, 0)),
            pl.BlockSpec((1, hh, 1), lambda b, i: (b, 0, 0)),
        ],
        out_specs=[pl.BlockSpec((1, hh, tm), lambda b, i: (b, 0, i)),
                   pl.BlockSpec((1, hh, tm), lambda b, i: (b, 0, i))],
        out_shape=[out, out],
        scratch_shapes=[pltpu.VMEM((hh, 1), F32)],
        compiler_params=_params(2, 32),
    )(x, g, wft, bft, c0t)


def _fox_prompt_kernel(q_ref, k_ref, v_ref, fq_ref, fk_ref, o_ref, m_ref, l_ref, acc_ref, *, tq, td, hpb):
    i = pl.program_id(2)
    hd = q_ref.shape[-1] // hpb
    m_ref[...] = jnp.full_like(m_ref, MASK_VALUE)
    l_ref[...] = jnp.zeros_like(l_ref)
    acc_ref[...] = jnp.zeros_like(acc_ref)
    qs, fqs = [], []
    for h in range(hpb):
        qs.append((q_ref[:, h * hd:(h + 1) * hd].astype(F32) * (hd ** -0.5 * LOG2E)).astype(BF16))
        fqs.append(jnp.broadcast_to(fq_ref[0, h] * LOG2E, (tq, LANES)))

    def tile(j, tk, row_lo, masked):
        nr = tq - row_lo
        rs = slice(row_lo, tq)
        ks = pl.ds(pl.multiple_of(j * tk, tk), tk)
        for h in range(hpb):
            cs = slice(h * hd, (h + 1) * hd)
            s = _dot_nt(qs[h][rs], k_ref[ks, cs])
            fk = fk_ref[0, h, :, ks] * LOG2E
            us = []
            for c in range(tk // LANES):
                ls = slice(c * LANES, (c + 1) * LANES)
                u = s[:, ls] - fk[:, ls]
                if masked:
                    u = jnp.where(_iota((nr, LANES), 0) >= c * LANES + _iota((nr, LANES), 1), u, MASK_VALUE)
                us.append(u)
            row_max = jnp.max(functools.reduce(jnp.maximum, us), axis=-1, keepdims=True)
            fq = fqs[h][rs]
            m_old = m_ref[h, rs]
            m_new = jnp.maximum(m_old, jnp.broadcast_to(row_max, (nr, LANES)) + fq)
            shift = fq - m_new
            ps = [jnp.exp2(u + shift) for u in us]
            alpha = jnp.exp2(m_old - m_new)
            m_ref[h, rs] = m_new
            l_ref[h, rs] = alpha * l_ref[h, rs] + functools.reduce(jnp.add, ps)
            p_all = jnp.concatenate([p.astype(BF16) for p in ps], axis=1)
            acc_ref[h, rs] = alpha * acc_ref[h, rs] + _dot(p_all, v_ref[ks, cs])

    def body(j, carry):
        tile(j, tq, 0, False)
        return carry

    lax.fori_loop(0, i, body, 0)
    dpq = tq // td
    for mt in range(dpq):
        tile(i * dpq + mt, td, mt * td, True)
    for h in range(hpb):
        o_ref[:, h * hd:(h + 1) * hd] = acc_ref[h] / jnp.sum(l_ref[h], axis=-1, keepdims=True)


def _fox_prompt(q, k, v, ft, *, batch, heads):
    m, d = q.shape
    l = m // batch
    hd = d // heads
    assert hd == LANES
    hpb = ATTN_HEADS_PER_STEP
    tq = _tile(l, ATTN_Q_TILE)
    td = _tile(tq, ATTN_DIAG_TILE)
    nq = l // tq
    fcol = ft[..., None]
    frow = ft[:, :, None, :]
    return pl.pallas_call(
        functools.partial(_fox_prompt_kernel, tq=tq, td=td, hpb=hpb),
        grid=(batch, heads // hpb, nq),
        in_specs=[
            pl.BlockSpec((tq, hpb * hd), lambda b, h, i: (b * nq + i, h)),
            pl.BlockSpec((l, hpb * hd), lambda b, h, i: (b, h)),
            pl.BlockSpec((l, hpb * hd), lambda b, h, i: (b, h)),
            pl.BlockSpec((1, hpb, tq, 1), lambda b, h, i: (b, h, i, 0)),
            pl.BlockSpec((1, hpb, 1, l), lambda b, h, i: (b, h, 0, 0)),
        ],
        out_specs=pl.BlockSpec((tq, hpb * hd), lambda b, h, i: (b * nq + i, h)),
        out_shape=jax.ShapeDtypeStruct((m, d), F32),
        scratch_shapes=[pltpu.VMEM((hpb, tq, LANES), F32), pltpu.VMEM((hpb, tq, LANES), F32),
                        pltpu.VMEM((hpb, tq, hd), F32)],
        compiler_params=_params(3, 56),
    )(q, k, v, fcol, frow)


def _fpast_kernel(pt_ref, lf_ref, o_ref, car_ref, *, n_pg):
    b, g = pl.program_id(0), pl.program_id(1)
    n_grp, prow = lf_ref.shape[1], lf_ref.shape[2]
    rows = n_pg * prow
    keys_per_row = LANES // HEAD_GROUP

    @pl.when(g == 0)
    def _():
        car_ref[...] = jnp.zeros_like(car_ref)

    li, lj = _iota((LANES, LANES), 0), _iota((LANES, LANES), 1)
    same_head = (li % HEAD_GROUP) == (lj % HEAD_GROUP)
    prefix = (same_head & (li // HEAD_GROUP <= lj // HEAD_GROUP)).astype(F32)
    last = (same_head & (li // HEAD_GROUP == keys_per_row - 1)).astype(F32)
    strict = (_iota((rows, rows), 0) > _iota((rows, rows), 1)).astype(F32)
    for a in range(n_grp):
        x = jnp.concatenate([lf_ref[pt_ref[b, g * n_pg + o], a] for o in range(n_pg)], axis=0)
        y = _dot(x, prefix, HIGHEST)
        tot = _dot(y, last, HIGHEST)
        o_ref[0, a] = y + _dot(strict, tot, HIGHEST) + car_ref[a]
        car_ref[a] += jnp.sum(tot, axis=0, keepdims=True)


def _fpast(page_table, cache_logf_grp):
    b, n_pages = page_table.shape
    n_phys, n_grp, prow, _ = cache_logf_grp.shape
    n_pg = _tile(n_pages, FPAST_PAGES_PER_STEP)
    return pl.pallas_call(
        functools.partial(_fpast_kernel, n_pg=n_pg),
        grid_spec=pltpu.PrefetchScalarGridSpec(
            num_scalar_prefetch=1,
            grid=(b, n_pages // n_pg),
            in_specs=[pl.BlockSpec((n_phys, n_grp, prow, LANES), lambda bi, g, pt: (0, 0, 0, 0))],
            out_specs=pl.BlockSpec((1, n_grp, n_pg * prow, LANES), lambda bi, g, pt: (bi, 0, g, 0)),
            scratch_shapes=[pltpu.VMEM((n_grp, 1, LANES), F32)],
        ),
        out_shape=jax.ShapeDtypeStruct((b, n_grp, n_pages * prow, LANES), F32),
        compiler_params=_params(2, 40),
    )(page_table, cache_logf_grp)


def _fox_sample_kernel(pt_ref, q_ref, *refs, heads, n_pg):
    step, nsteps = pl.program_id(1), pl.num_programs(1)
    kc_refs, vc_refs = refs[:n_pg], refs[n_pg:2 * n_pg]
    fp_ref, kn_ref, vn_ref, fq_ref, fn_ref, o_ref, qa_ref, m_ref, l_ref, acc_ref = refs[2 * n_pg:]
    t, d = q_ref.shape
    hd = d // heads
    grp = HEAD_GROUP
    n_grp = heads // grp
    r = grp * t
    page = kc_refs[0].shape[1]
    prow = page * grp // LANES

    def pre():
        @pl.when(step == 0)
        def _():
            q = q_ref[...] * (hd ** -0.5 * LOG2E)
            for a in range(n_grp):
                qa_ref[a] = jnp.concatenate(
                    [q[:, (a * grp + j) * hd:(a * grp + j + 1) * hd] for j in range(grp)], axis=0)
            m_ref[...] = jnp.full_like(m_ref, MASK_VALUE)
            l_ref[...] = jnp.zeros_like(l_ref)
            acc_ref[...] = jnp.zeros_like(acc_ref)

    head_match = (_iota((r, LANES), 0) // t) == (_iota((r, LANES), 1) % grp)

    def update(a, k_list, v_list, fk_list, mask):
        fq = jnp.broadcast_to(fq_ref[0, a] * LOG2E, (r, LANES))
        qa = qa_ref[a]
        us = []
        for k_flat, fk in zip(k_list, fk_list):
            s = _dot_nt(qa, k_flat)
            for c in range(k_flat.shape[0] // LANES):
                u = s[:, c * LANES:(c + 1) * LANES] - fk[c:c + 1] * LOG2E
                us.append(jnp.where(mask, u, MASK_VALUE))
        row_max = jnp.max(functools.reduce(jnp.maximum, us), axis=-1, keepdims=True)
        m_old = m_ref[a]
        m_new = jnp.maximum(m_old, jnp.broadcast_to(row_max, (r, LANES)) + fq)
        shift = fq - m_new
        ps = [jnp.exp2(u + shift) for u in us]
        alpha = jnp.exp2(m_old - m_new)
        m_ref[a] = m_new
        l_ref[a] = alpha * l_ref[a] + functools.reduce(jnp.add, ps)
        pv, at = None, 0
        for v_flat in v_list:
            n_t = v_flat.shape[0] // LANES
            p_pg = jnp.concatenate(ps[at:at + n_t], axis=1) if n_t > 1 else ps[at]
            at += n_t
            part = _dot(p_pg, v_flat)
            pv = part if pv is None else pv + part
        acc_ref[a] = alpha * acc_ref[a] + pv

    def group_rows(ref, a):
        blk = ref[0, :, a * grp:(a + 1) * grp, :]
        return blk.reshape(blk.shape[0] * grp, hd)

    def main():
        for a in range(n_grp):
            update(a, [group_rows(kc, a) for kc in kc_refs], [group_rows(vc, a) for vc in vc_refs],
                   [fp_ref[0, a, o * prow:(o + 1) * prow] for o in range(n_pg)], head_match)

    def post():
        @pl.when(step == nsteps - 1)
        def _():
            lane = _iota((r, LANES), 1)
            new_mask = head_match & (lane // grp <= _iota((r, LANES), 0) % t) & (lane < t * grp)
            zeros = jnp.zeros((LANES - t * grp, hd), F32)
            for a in range(n_grp):
                kn = jnp.concatenate([group_rows(kn_ref, a), zeros], axis=0)
                vn = jnp.concatenate([group_rows(vn_ref, a), zeros], axis=0)
                update(a, [kn], [vn], [fn_ref[0, a]], new_mask)
                res = acc_ref[a] / jnp.sum(l_ref[a], axis=-1, keepdims=True)
                for j in range(grp):
                    h = a * grp + j
                    o_ref[:, h * hd:(h + 1) * hd] = res[j * t:(j + 1) * t]

    pre()
    main()
    post()


def _fox_sample(q, cache_k, cache_v, page_table, fpast, k_new, v_new, fq_col, fnew_flat):
    b, n_pages = page_table.shape
    m, d = q.shape
    t = m // b
    _, page, heads, hd = cache_k.shape
    grp = HEAD_GROUP
    n_grp = heads // grp
    assert hd == LANES and heads % grp == 0 and t * grp <= LANES and (page * grp) % LANES == 0
    r = grp * t
    n_pg = _tile(n_pages, PAGES_PER_STEP)
    prow = page * grp // LANES
    page_spec = lambda o: pl.BlockSpec((1, page, heads, hd), lambda bi, s, pt: (pt[bi, s * n_pg + o], 0, 0, 0))
    new_spec = pl.BlockSpec((1, t, heads, hd), lambda bi, s, pt: (bi, 0, 0, 0))
    in_specs = ([pl.BlockSpec((t, d), lambda bi, s, pt: (bi, 0))]
                + [page_spec(o) for o in range(n_pg)] * 2
                + [pl.BlockSpec((1, n_grp, n_pg * prow, LANES), lambda bi, s, pt: (bi, 0, s, 0)),
                   new_spec, new_spec,
                   pl.BlockSpec((1, n_grp, r, 1), lambda bi, s, pt: (bi, 0, 0, 0)),
                   pl.BlockSpec((1, n_grp, 1, LANES), lambda bi, s, pt: (bi, 0, 0, 0))])
    return pl.pallas_call(
        functools.partial(_fox_sample_kernel, heads=heads, n_pg=n_pg),
        grid_spec=pltpu.PrefetchScalarGridSpec(
            num_scalar_prefetch=1,
            grid=(b, n_pages // n_pg),
            in_specs=in_specs,
            out_specs=pl.BlockSpec((t, d), lambda bi, s, pt: (bi, 0)),
            scratch_shapes=[pltpu.VMEM((n_grp, r, hd), F32), pltpu.VMEM((n_grp, r, LANES), F32),
                            pltpu.VMEM((n_grp, r, LANES), F32), pltpu.VMEM((n_grp, r, hd), F32)],
        ),
        out_shape=jax.ShapeDtypeStruct((m, d), F32),
        compiler_params=_params(2, 56),
    )(page_table, q, *([cache_k] * n_pg), *([cache_v] * n_pg), fpast, k_new, v_new, fq_col, fnew_flat)


def _trunk(x3, mods, s0, attend_ctx, wts):
    (norm_pre, norm_post, a_w_in, a_w_out, a_lb_logits, a_out_norm, kv_norm, w_kv, wft, b_forget,
     b_w_q, b_w_o, mlp_w_in, mlp_w_out) = wts
    b, l, d = x3.shape
    m = b * l
    depth = norm_pre.shape[0]
    n_a = a_w_out.shape[0]
    a_heads = s0.shape[2]
    pick = lambda w, layer: (w[layer], 0) if isinstance(w, list) else (w, layer)
    casts = {"a_w_in": [], "mlp_w_in": [], "mlp_w_out": []}
    b_heads = b_forget.shape[0]
    x = x3.reshape(m, d)
    if m <= 512:
        rows_per_mod = m
        expand = lambda t: jnp.repeat(t, l, axis=0)[None]
    else:
        rows_per_mod = l
        expand = lambda t: t[:, None, :]
    row = lambda t: t.reshape(1, -1)
    states = []
    shared = None
    for layer in range(depth):
        mod = expand(mods[layer, 0])
        if layer < n_a:
            w_in, w_idx = pick(a_w_in, layer)
            proj = _norm_mm(x, row(norm_pre[layer, 0]), mod, w_in, w_idx, rows_per_mod=rows_per_mod)
            if w_in.dtype != BF16:
                proj, w_in_b = proj
                casts["a_w_in"].append(w_in_b)
            o, s_new = _hgrn(proj.reshape(b, l, -1), a_lb_logits, row(a_out_norm[layer]), s0[layer],
                             layer=layer, heads=a_heads)
            states.append(s_new)
            y_in, w_o, w_idx = o.reshape(m, d), a_w_out, layer
        else:
            j = layer - n_a
            if shared is None:
                kvg = row(kv_norm)
                k, k16 = _norm_mm(x, kvg, None, w_kv, 0, rows_per_mod=rows_per_mod, n_cols=d, col_off=0,
                                  out_dtypes=(F32, BF16))
                v, v16 = _norm_mm(x, kvg, None, w_kv, 0, rows_per_mod=rows_per_mod, n_cols=d, col_off=d,
                                  out_dtypes=(F32, BF16))
                shared = attend_ctx["make"](x, kvg, wft, b_forget.reshape(b_heads, 1), k, v, k16, v16)
            q = _norm_mm(x, row(norm_pre[layer, 0]), mod, b_w_q, j, rows_per_mod=rows_per_mod,
                         out_dtypes=(attend_ctx["q_dtype"],))
            y_in, w_o, w_idx = attend_ctx["attend"](q, shared), b_w_o, j
        x = _mm_post(y_in, w_o, w_idx, x, mod, row(norm_post[layer, 0]), rows_per_mod=rows_per_mod)
        mod = expand(mods[layer, 1])
        w1, w_idx = pick(mlp_w_in, layer)
        w2, _ = pick(mlp_w_out, layer)
        x = _mlp(x, row(norm_pre[layer, 1]), mod, w1, w2, w_idx, row(norm_post[layer, 1]),
                 rows_per_mod=rows_per_mod)
        if w1.dtype != BF16:
            x, w1_b, w2_b = x
            casts["mlp_w_in"].append(w1_b)
            casts["mlp_w_out"].append(w2_b)
    k, v, lft = shared["rows"]
    kv_shape = (b, l, b_heads, d // b_heads)
    return (x.reshape(b, l, d), jnp.stack(states), k.reshape(kv_shape), v.reshape(kv_shape),
            jnp.swapaxes(lft, 1, 2), casts)


def kernel(x_prompt, x_sample, c_prompt, c_sample, state_hgrn, cache_k, cache_v, cache_logf, page_table,
           norm_pre, norm_post, w_ada, b_ada, a_w_in, a_w_out, a_lb_logits, a_out_norm,
           kv_norm, w_kv, b_forget, b_w_q, b_w_o, mlp_w_in, mlp_w_out):
    bp, _, d = x_prompt.shape
    bs, ts, _ = x_sample.shape
    depth = norm_pre.shape[0]
    n_phys, page, heads, hd = cache_k.shape
    wts_s = (norm_pre, norm_post, a_w_in, a_w_out.astype(BF16), a_lb_logits, a_out_norm, kv_norm,
             w_kv.astype(BF16)[None], w_kv[:, 2 * d:].T, b_forget,
             b_w_q.astype(BF16), b_w_o.astype(BF16), mlp_w_in, mlp_w_out)

    n_c = bp + bs
    c_rows = -(-n_c // 8) * 8
    c_all = jnp.concatenate([c_prompt, c_sample, jnp.zeros((c_rows - n_c, d), F32)], axis=0)
    mods = _ada_all(c_all, w_ada.reshape(depth * 2, d, 3 * d), b_ada.reshape(depth * 2, 1, 3 * d))
    mods = mods.reshape(depth, 2, c_rows, 3 * d)
    mods_p, mods_s = mods[:, :, :bp], mods[:, :, bp:n_c]

    grp = HEAD_GROUP
    n_grp = heads // grp
    prow = page * grp // LANES
    lf_grp = cache_logf.astype(F32).reshape(n_phys, page, n_grp, grp).transpose(0, 2, 1, 3)
    fpast = _fpast(page_table, lf_grp.reshape(n_phys, n_grp, prow, LANES))
    f_total = fpast[:, :, -1, LANES - grp:].reshape(bs, heads, 1)

    def make_sample(x, kvg, wft, bft, k, v, k16, v16):
        lft, fnew_t = _logf(x, kvg, wft, bft, f_total, batch=bs)
        fq_col = fnew_t.reshape(bs, n_grp, grp * ts, 1)
        fnew_flat = fnew_t.reshape(bs, n_grp, grp, ts).transpose(0, 1, 3, 2).reshape(bs, n_grp, 1, ts * grp)
        fnew_flat = jnp.pad(fnew_flat, ((0, 0), (0, 0), (0, 0), (0, LANES - ts * grp)))
        shape4 = (bs, ts, heads, hd)
        return {"rows": (k, v, lft), "k": k.reshape(shape4), "v": v.reshape(shape4), "fq": fq_col, "fn": fnew_flat}

    def attend_sample(q, sh):
        return _fox_sample(q, cache_k, cache_v, page_table, fpast, sh["k"], sh["v"], sh["fq"], sh["fn"])

    y_s, st_s, k_s, v_s, lf_s, casts = _trunk(
        x_sample, mods_s, state_hgrn, {"make": make_sample, "attend": attend_sample, "q_dtype": F32}, wts_s)

    def make_prompt(x, kvg, wft, bft, k, v, k16, v16):
        lft, ft = _logf(x, kvg, wft, bft, jnp.zeros((bp, heads, 1), F32), batch=bp)
        return {"rows": (k, v, lft), "k": k16, "v": v16, "ft": ft}

    def attend_prompt(q, sh):
        return _fox_prompt(q, sh["k"], sh["v"], sh["ft"], batch=bp, heads=heads)

    wts_p = wts_s[:2] + (casts["a_w_in"],) + wts_s[3:12] + (casts["mlp_w_in"], casts["mlp_w_out"])
    s0_prompt = jnp.zeros((a_w_out.shape[0], bp) + state_hgrn.shape[2:], F32)
    y_p, st_p, k_p, v_p, lf_p, _ = _trunk(
        x_prompt, mods_p, s0_prompt, {"make": make_prompt, "attend": attend_prompt, "q_dtype": BF16}, wts_p)

    return (y_p, y_s, st_p, st_s, k_p, v_p, lf_p, k_s, v_s, lf_s)
```
